```python
import math
import jax, jax.numpy as jnp
from jax import lax
import numpy as np

D_MODEL = 1024
BATCH = 4
SEQ = 4096
DEPTH = 4

D_MIX = D_MODEL
MIX_CHUNK = 128
SPATIAL_CHUNK = 128
M_WIDTH = 3 * D_MODEL // 8
M_HEADS = 4
M_HEAD_DIM = M_WIDTH // M_HEADS
M_CONV = 4
M_BLOCK = 4
M_NBLOCKS = M_WIDTH // M_BLOCK
R_WIDTH = 3 * D_MODEL // 8
R_HEADS = 6
R_HEAD_DIM = R_WIDTH // R_HEADS
ROPE_THETA = 10000.0
G_WIDTH = D_MIX - M_WIDTH - R_WIDTH
G_GROUPS = 4
G_GROUP_DIM = G_WIDTH // G_GROUPS
D_FF = 4 * D_MODEL
EPS = 1e-6
IN_SIZES = (M_WIDTH, M_WIDTH, M_HEADS, M_HEADS, R_WIDTH, R_WIDTH, R_WIDTH, R_WIDTH, G_WIDTH, G_WIDTH)
N_IN = 2 * M_WIDTH + 2 * M_HEADS + 4 * R_WIDTH + 2 * G_WIDTH

kernel_name = "hybrid_mlstm_retention_gmlp_trunk"


def rmsnorm(x, g):
    xf = x.astype(jnp.float32)
    y = xf * lax.rsqrt(jnp.mean(jnp.square(xf), axis=-1, keepdims=True) + EPS)
    return (y * g.astype(jnp.float32)).astype(x.dtype)


def causal_depthwise_conv(x, w, b):
    k_w = w.shape[0]
    s = x.shape[1]
    xp = jnp.pad(x, ((0, 0), (k_w - 1, 0), (0, 0)))
    return sum(xp[:, j:j + s] * w[j] for j in range(k_w)) + b


def rotary(x):
    s, d = x.shape[1], x.shape[3]
    half = d // 2
    pos = jnp.arange(s, dtype=jnp.float32)
    freqs = ROPE_THETA ** (-jnp.arange(half, dtype=jnp.float32) / half)
    ang = pos[:, None] * freqs[None, :]
    cos = jnp.cos(ang)[None, :, None, :]
    sin = jnp.sin(ang)[None, :, None, :]
    xf = x.astype(jnp.float32)
    x1, x2 = xf[..., :half], xf[..., half:]
    return jnp.concatenate([x1 * cos - x2 * sin, x2 * cos + x1 * sin], axis=-1)


def to_chunks(x, chunk):
    b, s = x.shape[0], x.shape[1]
    xc = x.astype(jnp.float32).reshape((b, s // chunk, chunk) + x.shape[2:])
    perm = (0, 3, 1, 2) + tuple(range(4, xc.ndim))
    return xc.transpose(perm)


def from_chunks(x):
    b, h, nc, l, d = x.shape
    return x.transpose(0, 2, 3, 1, 4).reshape(b, nc * l, h, d)


def mlstm_chunkwise(q, k, v, i_pre, f_pre):
    d = q.shape[-1]
    l = MIX_CHUNK
    qc = to_chunks(q, l)
    kc = to_chunks(k, l) * (d ** -0.5)
    vc = to_chunks(v, l)
    ig = to_chunks(i_pre, l)
    logf = jax.nn.log_sigmoid(to_chunks(f_pre, l))
    bcum = jnp.cumsum(logf, axis=-1)
    b_end = bcum[..., -1]
    causal = jnp.tril(jnp.ones((l, l), dtype=bool))
    dmat = jnp.where(causal, bcum[..., :, None] - bcum[..., None, :] + ig[..., None, :], -jnp.inf)
    w_end = b_end[..., None] - bcum + ig
    a_c = jnp.max(w_end, axis=-1)
    e_end = jnp.exp(w_end - a_c[..., None])
    c_chunk = jnp.einsum('bhcl,bhcld,bhcle->bhcde', e_end, kc, vc)
    n_chunk = jnp.einsum('bhcl,bhcld->bhcd', e_end, kc)

    def step(carry, inp):
        c_st, n_st, m_st = carry
        be, ac, cc, nc = inp
        m_new = jnp.maximum(be + m_st, ac)
        s_old = jnp.exp(be + m_st - m_new)
        s_new = jnp.exp(ac - m_new)
        c_new = s_old[..., None, None] * c_st + s_new[..., None, None] * cc
        n_new = s_old[..., None] * n_st + s_new[..., None] * nc
        return (c_new, n_new, m_new), (c_st, n_st, m_st)

    bsz, h = qc.shape[0], qc.shape[1]
    init = (jnp.zeros((bsz, h, d, d), jnp.float32), jnp.zeros((bsz, h, d), jnp.float32),
            jnp.zeros((bsz, h), jnp.float32))
    xs = (jnp.moveaxis(b_end, 2, 0), jnp.moveaxis(a_c, 2, 0),
          jnp.moveaxis(c_chunk, 2, 0), jnp.moveaxis(n_chunk, 2, 0))
    _, (c_prev, n_prev, m_prev) = lax.scan(step, init, xs)
    c_prev = jnp.moveaxis(c_prev, 0, 2)
    n_prev = jnp.moveaxis(n_prev, 0, 2)
    m_prev = jnp.moveaxis(m_prev, 0, 2)

    inter_log = bcum + m_prev[..., None]
    m_t = jnp.maximum(inter_log, jnp.max(dmat, axis=-1))
    s_inter = jnp.exp(inter_log - m_t)
    wts = jnp.exp(dmat - m_t[..., None])
    scores = jnp.einsum('bhcld,bhcsd->bhcls', qc, kc) * wts
    num = (jnp.einsum('bhcls,bhcse->bhcle', scores, vc)
           + s_inter[..., None] * jnp.einsum('bhcld,bhcde->bhcle', qc, c_prev))
    den = jnp.sum(scores, axis=-1) + s_inter * jnp.einsum('bhcld,bhcd->bhcl', qc, n_prev)
    hout = num / jnp.maximum(jnp.abs(den), jnp.exp(-m_t))[..., None]
    return from_chunks(hout)


def retention_chunkwise(q, k, v):
    h, d = q.shape[2], q.shape[3]
    l = MIX_CHUNK
    qc = to_chunks(q, l)
    kc = to_chunks(k, l) * (d ** -0.5)
    vc = to_chunks(v, l)
    log_gamma = jnp.log(1.0 - 2.0 ** (-5.0 - jnp.arange(h, dtype=jnp.float32)))
    pos = jnp.arange(l, dtype=jnp.float32)
    causal = jnp.tril(jnp.ones((l, l), dtype=bool))
    rel = jnp.where(causal, pos[:, None] - pos[None, :], 0.0)
    decay = jnp.where(causal, jnp.exp(rel[None] * log_gamma[:, None, None]), 0.0)
    scores = jnp.einsum('bhcld,bhcsd->bhcls', qc, kc) * decay[None, :, None]
    intra = jnp.einsum('bhcls,bhcse->bhcle', scores, vc)
    zeta = jnp.exp((l - 1.0 - pos)[None, :] * log_gamma[:, None])
    kv_chunk = jnp.einsum('bhcld,bhcle,hl->bhcde', kc, vc, zeta)
    chunk_decay = jnp.exp(l * log_gamma)[None, :, None, None]

    def step(r_st, kv):
        return chunk_decay * r_st + kv, r_st

    bsz = qc.shape[0]
    _, r_prev = lax.scan(step, jnp.zeros((bsz, h, d, d), jnp.float32), jnp.moveaxis(kv_chunk, 2, 0))
    r_prev = jnp.moveaxis(r_prev, 0, 2)
    xi = jnp.exp((pos + 1.0)[None, :] * log_gamma[:, None])
    cross = jnp.einsum('bhcld,bhcde->bhcle', qc, r_prev) * xi[None, :, None, :, None]
    return from_chunks(intra + cross)


def chunked_spatial_gating(u, v, w_s, b_s):
    bsz, s, g, e = v.shape
    l = SPATIAL_CHUNK
    causal = jnp.tril(jnp.ones((l, l), dtype=bool))
    w = jnp.where(causal[None], w_s, 0.0).astype(v.dtype)
    vc = v.reshape(bsz, s // l, l, g, e)
    mixed = jnp.einsum('gts,bcsge->bctge', w, vc) + b_s.T.astype(v.dtype)[None, None, :, :, None]
    return u * mixed.reshape(bsz, s, g, e)


def hybrid_mixer(h, w_in, m_conv_w, m_conv_b, m_wq, m_wk, m_wv, m_i_bias, m_f_bias, m_norm_g,
                 m_skip, r_norm_g, g_norm_g, g_ws, g_bs, w_out):
    bsz, s, _ = h.shape
    dt = h.dtype
    proj = h @ w_in
    split_at = list(np.cumsum(IN_SIZES)[:-1])
    m_x, m_z, m_i, m_f, r_q, r_k, r_v, r_g, g_u, g_v = jnp.split(proj, split_at, axis=-1)

    x_c = jax.nn.silu(causal_depthwise_conv(m_x, m_conv_w, m_conv_b))
    xc_b = x_c.reshape(bsz, s, M_NBLOCKS, M_BLOCK)
    xm_b = m_x.reshape(bsz, s, M_NBLOCKS, M_BLOCK)
    q = jnp.einsum('bsni,nij->bsnj', xc_b, m_wq).reshape(bsz, s, M_HEADS, M_HEAD_DIM)
    k = jnp.einsum('bsni,nij->bsnj', xc_b, m_wk).reshape(bsz, s, M_HEADS, M_HEAD_DIM)
    v = jnp.einsum('bsni,nij->bsnj', xm_b, m_wv).reshape(bsz, s, M_HEADS, M_HEAD_DIM)
    hm = mlstm_chunkwise(q, k, v, m_i + m_i_bias, m_f + m_f_bias)
    hm = rmsnorm(hm, m_norm_g.reshape(M_HEADS, M_HEAD_DIM)).reshape(bsz, s, M_WIDTH).astype(dt)
    out_m = (hm + m_skip * x_c) * jax.nn.silu(m_z)

    rq = rotary(r_q.reshape(bsz, s, R_HEADS, R_HEAD_DIM))
    rk = rotary(r_k.reshape(bsz, s, R_HEADS, R_HEAD_DIM))
    hr = retention_chunkwise(rq, rk, r_v.reshape(bsz, s, R_HEADS, R_HEAD_DIM))
    hr = rmsnorm(hr, r_norm_g.reshape(R_HEADS, R_HEAD_DIM)).reshape(bsz, s, R_WIDTH).astype(dt)
    out_r = hr * jax.nn.silu(r_g)

    gu = jax.nn.gelu(g_u)
    gv = rmsnorm(jax.nn.gelu(g_v), g_norm_g)
    out_g = chunked_spatial_gating(gu.reshape(bsz, s, G_GROUPS, G_GROUP_DIM),
                                   gv.reshape(bsz, s, G_GROUPS, G_GROUP_DIM),
                                   g_ws, g_bs).reshape(bsz, s, G_WIDTH)

    return jnp.concatenate([out_m, out_r, out_g], axis=-1) @ w_out


def setup_inputs(seed: int = 0) -> dict:
    key = jax.random.key(seed)
    ks = jax.random.split(key, 24)
    f32 = jnp.float32

    def nrm(k, shape, scale):
        return jax.random.normal(k, shape, f32) * scale

    def gain(k, shape):
        return 1.0 + 0.05 * jax.random.normal(k, shape, f32)

    x = jax.random.normal(ks[0], (BATCH, SEQ, D_MODEL), f32)
    norm_mix_g = gain(ks[1], (DEPTH, D_MODEL))
    w_in = nrm(ks[2], (DEPTH, D_MODEL, N_IN), D_MODEL ** -0.5)
    m_conv_w = nrm(ks[3], (DEPTH, M_CONV, M_WIDTH), M_CONV ** -0.5)
    m_conv_b = nrm(ks[4], (DEPTH, M_WIDTH), 0.02)
    m_wq = nrm(ks[5], (DEPTH, M_NBLOCKS, M_BLOCK, M_BLOCK), M_BLOCK ** -0.5)
    m_wk = nrm(ks[6], (DEPTH, M_NBLOCKS, M_BLOCK, M_BLOCK), M_BLOCK ** -0.5)
    m_wv = nrm(ks[7], (DEPTH, M_NBLOCKS, M_BLOCK, M_BLOCK), M_BLOCK ** -0.5)
    m_i_bias = nrm(ks[8], (DEPTH, M_HEADS), 0.1)
    m_f_bias = (jnp.linspace(3.0, 6.0, M_HEADS, dtype=f32)[None, :]
                + nrm(ks[9], (DEPTH, M_HEADS), 0.1))
    m_norm_g = gain(ks[10], (DEPTH, M_WIDTH))
    m_skip = gain(ks[11], (DEPTH, M_WIDTH))
    r_norm_g = gain(ks[12], (DEPTH, R_WIDTH))
    g_norm_g = gain(ks[13], (DEPTH, G_WIDTH))
    g_ws = nrm(ks[14], (DEPTH, G_GROUPS, SPATIAL_CHUNK, SPATIAL_CHUNK), SPATIAL_CHUNK ** -0.5)
    g_bs = gain(ks[15], (DEPTH, G_GROUPS, SPATIAL_CHUNK))
    w_out = nrm(ks[16], (DEPTH, D_MIX, D_MODEL), D_MIX ** -0.5)
    norm_ff_g = gain(ks[17], (DEPTH, D_MODEL))
    w_ff1 = nrm(ks[18], (DEPTH, D_MODEL, D_FF), D_MODEL ** -0.5)
    w_ff2 = nrm(ks[19], (DEPTH, D_FF, D_MODEL), D_FF ** -0.5)
    final_norm_g = gain(ks[20], (D_MODEL,))
    return {"x": x, "norm_mix_g": norm_mix_g, "w_in": w_in, "m_conv_w": m_conv_w,
            "m_conv_b": m_conv_b, "m_wq": m_wq, "m_wk": m_wk, "m_wv": m_wv,
            "m_i_bias": m_i_bias, "m_f_bias": m_f_bias, "m_norm_g": m_norm_g,
            "m_skip": m_skip, "r_norm_g": r_norm_g, "g_norm_g": g_norm_g,
            "g_ws": g_ws, "g_bs": g_bs, "w_out": w_out, "norm_ff_g": norm_ff_g,
            "w_ff1": w_ff1, "w_ff2": w_ff2, "final_norm_g": final_norm_g}


def reference(x, norm_mix_g, w_in, m_conv_w, m_conv_b, m_wq, m_wk, m_wv, m_i_bias, m_f_bias,
              m_norm_g, m_skip, r_norm_g, g_norm_g, g_ws, g_bs, w_out, norm_ff_g, w_ff1, w_ff2,
              final_norm_g):
    for layer in range(DEPTH):
        h = rmsnorm(x, norm_mix_g[layer])
        x = x + hybrid_mixer(h, w_in[layer], m_conv_w[layer], m_conv_b[layer], m_wq[layer],
                             m_wk[layer], m_wv[layer], m_i_bias[layer], m_f_bias[layer],
                             m_norm_g[layer], m_skip[layer], r_norm_g[layer], g_norm_g[layer],
                             g_ws[layer], g_bs[layer], w_out[layer])
        h2 = rmsnorm(x, norm_ff_g[layer])
        x = x + jnp.square(jax.nn.relu(h2 @ w_ff1[layer])) @ w_ff2[layer]
    return rmsnorm(x, final_norm_g)
```

```python
import functools
import math

import numpy as np
import jax
import jax.numpy as jnp
from jax import lax
from jax.experimental import pallas as pl
from jax.experimental.pallas import tpu as pltpu

M_HEADS = 4
M_CONV = 4
M_BLOCK = 4
R_HEADS = 6
G_GROUPS = 4
CHUNK = 128
ROPE_THETA = 10000.0
EPS = 1e-6

LANES = 128
V7X_VMEM_BYTES = 64 * 1024 * 1024

TOKEN_TILE = 512
MIX_TILE = 512
FF_CHUNK = 1024

F32 = jnp.float32
BF16 = jnp.bfloat16


def _vmem_limit(nbytes):
    return int(min(nbytes + (16 << 20), V7X_VMEM_BYTES - (6 << 20)))


def _rms(x, g):
    return x * lax.rsqrt(jnp.mean(jnp.square(x), axis=-1, keepdims=True) + EPS) * g


def _dot(a, b):
    return jnp.dot(a, b, preferred_element_type=F32)


def _dot_nt(a, b):
    return lax.dot_general(a, b, (((1,), (1,)), ((), ())), preferred_element_type=F32)


def _dot_tn(a, b):
    return lax.dot_general(a, b, (((0,), (0,)), ((), ())), preferred_element_type=F32)


def _inproj_kernel(x_ref, g_ref, w_ref, o_ref):
    h = _rms(x_ref[...], g_ref[...]).astype(BF16)
    o_ref[...] = _dot(h, w_ref[...])


def _inproj(x2, g, w):
    t, d = x2.shape
    n = w.shape[1]
    tm = TOKEN_TILE
    est = 2 * tm * d * 4 + 2 * d * n * 2 + 2 * tm * n * 4
    return pl.pallas_call(
        _inproj_kernel,
        grid=(t // tm,),
        in_specs=[pl.BlockSpec((tm, d), lambda i: (i, 0)),
                  pl.BlockSpec((1, d), lambda i: (0, 0)),
                  pl.BlockSpec((d, n), lambda i: (0, 0))],
        out_specs=pl.BlockSpec((tm, n), lambda i: (i, 0)),
        out_shape=jax.ShapeDtypeStruct((t, n), F32),
        compiler_params=pltpu.CompilerParams(
            dimension_semantics=("arbitrary",), vmem_limit_bytes=_vmem_limit(est)),
        name="inproj",
    )(x2, g, w)


def _ffn_kernel(mix_ref, x_ref, wo_ref, g_ref, w1_ref, w2_ref, gf_ref, o_ref, *, final):
    x1 = x_ref[...] + _dot(mix_ref[...], wo_ref[...])
    h2 = _rms(x1, g_ref[...]).astype(BF16)
    acc = x1
    for j in range(w1_ref.shape[1] // FF_CHUNK):
        cols = slice(j * FF_CHUNK, (j + 1) * FF_CHUNK)
        a = jnp.square(jnp.maximum(_dot(h2, w1_ref[:, cols]), 0.0)).astype(BF16)
        acc = acc + _dot(a, w2_ref[cols, :])
    if final:
        acc = _rms(acc, gf_ref[...])
    o_ref[...] = acc


def _ffn(mix, x2, wo, g, w1, w2, gf, final):
    t, d = x2.shape
    dff = w1.shape[1]
    tm = TOKEN_TILE
    const = lambda i: (0, 0)
    est = (2 * tm * d * 2 + 4 * tm * d * 4 + 2 * (d * d + 2 * d * dff) * 2
           + tm * FF_CHUNK * 6 + 3 * tm * d * 4)
    return pl.pallas_call(
        functools.partial(_ffn_kernel, final=final),
        grid=(t // tm,),
        in_specs=[pl.BlockSpec((tm, d), lambda i: (i, 0)),
                  pl.BlockSpec((tm, d), lambda i: (i, 0)),
                  pl.BlockSpec((d, d), const),
                  pl.BlockSpec((1, d), const),
                  pl.BlockSpec((d, dff), const),
                  pl.BlockSpec((dff, d), const),
                  pl.BlockSpec((1, d), const)],
        out_specs=pl.BlockSpec((tm, d), lambda i: (i, 0)),
        out_shape=jax.ShapeDtypeStruct((t, d), F32),
        compiler_params=pltpu.CompilerParams(
            dimension_semantics=("arbitrary",), vmem_limit_bytes=_vmem_limit(est)),
        name="ffn_final" if final else "ffn",
    )(mix, x2, wo, g, w1, w2, gf)


class _Layout:
    def __init__(self, d_model):
        self.mw = 3 * d_model // 8
        self.rw = 3 * d_model // 8
        self.gw = d_model - self.mw - self.rw
        self.md = self.mw // M_HEADS
        self.rd = self.rw // R_HEADS
        assert self.md <= LANES and 2 * self.rd == LANES and self.gw // G_GROUPS * 2 == LANES
        assert self.mw % LANES == 0 and self.rw % LANES == 0 and self.gw % LANES == 0
        o = 0
        self.mx = o; o += self.mw
        self.mz = o; o += self.mw
        self.rq = o; o += self.rw
        self.rk = o; o += self.rw
        self.rv = o; o += self.rw
        self.rg = o; o += self.rw
        self.gu = o; o += self.gw
        self.gv = o; o += self.gw
        self.gate = o; o += LANES
        self.n = o
        self.mpad = M_HEADS * LANES

    def in_perm(self):
        mw, rw, gw, rd = self.mw, self.rw, self.gw, self.rd
        o_mx, o_mz = 0, mw
        o_i, o_f = 2 * mw, 2 * mw + M_HEADS
        o_rq = 2 * mw + 2 * M_HEADS
        o_rk, o_rv, o_rg = o_rq + rw, o_rq + 2 * rw, o_rq + 3 * rw
        o_gu = o_rq + 4 * rw
        o_gv = o_gu + gw
        perm = np.full((self.n,), -1, np.int64)
        perm[self.mx:self.mx + mw] = o_mx + np.arange(mw)
        perm[self.mz:self.mz + mw] = o_mz + np.arange(mw)
        half = rd // 2
        rot = np.empty((rw,), np.int64)
        for c in range(rw):
            pair, r = divmod(c, LANES)
            part, r = divmod(r, 2 * half)
            hin, i = divmod(r, half)
            rot[c] = rd * (2 * pair + hin) + half * part + i
        perm[self.rq:self.rq + rw] = o_rq + rot
        perm[self.rk:self.rk + rw] = o_rk + rot
        perm[self.rv:self.rv + rw] = o_rv + np.arange(rw)
        perm[self.rg:self.rg + rw] = o_rg + np.arange(rw)
        perm[self.gu:self.gu + gw] = o_gu + np.arange(gw)
        perm[self.gv:self.gv + gw] = o_gv + np.arange(gw)
        perm[self.gate:self.gate + M_HEADS] = o_i + np.arange(M_HEADS)
        perm[self.gate + M_HEADS:self.gate + 2 * M_HEADS] = o_f + np.arange(M_HEADS)
        return perm

    def head_pad_index(self):
        idx = np.full((self.mpad,), -1, np.int64)
        for h in range(M_HEADS):
            idx[h * LANES:h * LANES + self.md] = h * self.md + np.arange(self.md)
        return idx


def _gather_cols(a, idx):
    out = jnp.take(a, jnp.asarray(np.maximum(idx, 0)), axis=-1)
    return jnp.where(jnp.asarray(idx >= 0), out, jnp.zeros((), a.dtype))


def _mixer_kernel(proj_ref, cos_ref, sin_ref, convw_ref, convb_ref, bdq_ref, bdk_ref, bdv_ref,
                  gbias_ref, mnorm_ref, mskip_ref, rnorm_ref, gnorm_ref, wcat_ref, gbmat_ref,
                  decay_ref, xi_ref, zeta_ref, cd_ref, bmask_ref,
                  o_ref,
                  xbuf, q_s, k_s, v_s, xc_s, c_st, r_st, m_st, *, lay):
    ts = proj_ref.shape[0]
    md, mw, rw, gw = lay.md, lay.mw, lay.rw, lay.gw
    j = pl.program_id(1)

    @pl.when(j == 0)
    def _():
        c_st[...] = jnp.zeros_like(c_st)
        r_st[...] = jnp.zeros_like(r_st)
        m_st[...] = jnp.zeros_like(m_st)
        xbuf[0:8, :] = jnp.zeros((8, mw), F32)

    @pl.when(j > 0)
    def _():
        xbuf[0:8, :] = xbuf[ts:ts + 8, :]

    mx = proj_ref[:, lay.mx:lay.mx + mw]
    xbuf[8:ts + 8, :] = mx
    conv = convb_ref[...] + convw_ref[M_CONV - 1:M_CONV, :] * mx
    for t in range(M_CONV - 1):
        conv = conv + convw_ref[t:t + 1, :] * xbuf[8 - (M_CONV - 1) + t:8 - (M_CONV - 1) + t + ts, :]
    xc = conv * jax.nn.sigmoid(conv)
    xc_s[...] = xc
    xcb = xc.astype(BF16)
    lane_p = lax.broadcasted_iota(jnp.int32, (1, lay.mpad), 1) % LANES
    q_s[...] = _dot(xcb, bdq_ref[...]).astype(BF16)
    k_s[...] = _dot(xcb, bdk_ref[...]) * (md ** -0.5)
    v_s[...] = (_dot(mx.astype(BF16), bdv_ref[...]) + (lane_p == md).astype(F32)).astype(BF16)

    li = lax.broadcasted_iota(jnp.int32, (CHUNK, LANES), 1)
    ri = lax.broadcasted_iota(jnp.int32, (CHUNK, LANES), 0)
    causal = li <= ri
    tril = causal.astype(BF16)
    head_lane = li < M_HEADS
    lo_half = li < (LANES // 2)
    rot_a = (li % (LANES // 2)) < (LANES // 4)

    def chunk(c, carry):
        r0 = pl.multiple_of(c * CHUNK, CHUNK)
        rows = pl.ds(r0, CHUNK)

        g = proj_ref[rows, lay.gate:lay.gate + LANES] + gbias_ref[...]
        ig = jnp.where(head_lane, g, 0.0)
        logf = jnp.where(head_lane, jax.nn.log_sigmoid(pltpu.roll(g, LANES - M_HEADS, axis=1)), 0.0)
        a1 = logf.astype(BF16)
        r1 = logf - a1.astype(F32)
        a2 = r1.astype(BF16)
        a3 = (r1 - a2.astype(F32)).astype(BF16)
        bcum = (_dot(tril, a3) + _dot(tril, a2)) + _dot(tril, a1)
        b_end = bcum[CHUNK - 1:CHUNK, :]
        w = ig - bcum
        w_t = w.T
        m_prev = m_st[0:1, :]
        inter_log = bcum + m_prev
        w_end = b_end + w
        a_c = jnp.max(w_end, axis=0, keepdims=True)
        e_end = jnp.exp(w_end - a_c)
        m_new = jnp.maximum(b_end + m_prev, a_c)
        s_old = jnp.exp(b_end + m_prev - m_new)
        s_new = jnp.exp(a_c - m_new)
        m_st[0:1, :] = m_new

        hms = []
        for h in range(M_HEADS):
            hl = slice(h * LANES, (h + 1) * LANES)
            qh = q_s[rows, hl]
            kh = k_s[rows, hl]
            vh = v_s[rows, hl]
            s = _dot_nt(qh, kh.astype(BF16))
            dm = jnp.where(causal, bcum[:, h:h + 1] + w_t[h:h + 1, :], -jnp.inf)
            il = inter_log[:, h:h + 1]
            m_t = jnp.maximum(il, jnp.max(dm, axis=-1, keepdims=True))
            s_inter = jnp.exp(il - m_t)
            p = s * jnp.exp(dm - m_t)
            c_prev = c_st[h]
            inter = s_inter * _dot(qh, c_prev.astype(BF16))
            tot = _dot(p.astype(BF16), vh) + inter
            den = jnp.sum(p + jnp.where(li == md, inter, 0.0), axis=-1, keepdims=True)
            rden = 1.0 / jnp.maximum(jnp.abs(den), jnp.exp(-m_t))
            hout = jnp.where(li < md, tot * rden, 0.0)
            ms = jnp.sum(hout * hout, axis=-1, keepdims=True) * (1.0 / md)
            hms.append(hout * lax.rsqrt(ms + EPS) * mnorm_ref[:, hl])
            ke = (kh * e_end[:, h:h + 1]).astype(BF16)
            c_st[h] = s_old[:, h:h + 1] * c_prev + s_new[:, h:h + 1] * _dot_tn(ke, vh)

        outs = []
        for t in range(mw // LANES):
            lo_h, lo_off = divmod(t * LANES, md)
            pieces = None
            hh, start = lo_h, lo_off
            pos = 0
            while pos < LANES:
                take = min(md - start, LANES - pos)
                src = hms[hh] if (pos - start) % LANES == 0 else pltpu.roll(hms[hh], (pos - start) % LANES, axis=1)
                sel = (li >= pos) & (li < pos + take)
                pieces = jnp.where(sel, src, 0.0) if pieces is None else jnp.where(sel, src, pieces)
                pos += take
                hh, start = hh + 1, 0
            outs.append(pieces)
        hm = jnp.concatenate(outs, axis=1)
        mz = proj_ref[rows, lay.mz:lay.mz + mw]
        out_m = (hm + mskip_ref[...] * xc_s[rows, :]) * (mz * jax.nn.sigmoid(mz))
        o_ref[rows, 0:mw] = out_m.astype(o_ref.dtype)

        for p_i in range(rw // LANES):
            pl_ = slice(p_i * LANES, (p_i + 1) * LANES)
            cs = cos_ref[rows, pl_]
            sn = sin_ref[rows, pl_]
            rq = proj_ref[rows, lay.rq + p_i * LANES:lay.rq + (p_i + 1) * LANES]
            rk = proj_ref[rows, lay.rk + p_i * LANES:lay.rk + (p_i + 1) * LANES]
            rv = proj_ref[rows, lay.rv + p_i * LANES:lay.rv + (p_i + 1) * LANES]
            rg = proj_ref[rows, lay.rg + p_i * LANES:lay.rg + (p_i + 1) * LANES]
            qb = (rq * cs + pltpu.roll(rq, LANES // 2, axis=1) * sn).astype(BF16)
            kr = (rk * cs + pltpu.roll(rk, LANES // 2, axis=1) * sn) * (lay.rd ** -0.5)
            kcat = jnp.concatenate([jnp.where(rot_a, kr, 0.0).astype(BF16),
                                    jnp.where(rot_a, 0.0, kr).astype(BF16)], axis=0)
            s_ab = _dot_nt(qb, kcat)
            p_ab = (s_ab * decay_ref[p_i]).astype(BF16)
            vst = jnp.concatenate([jnp.where(lo_half, rv, 0.0).astype(BF16),
                                   jnp.where(lo_half, 0.0, rv).astype(BF16)], axis=0)
            r_prev = r_st[p_i]
            hr = _dot(p_ab, vst) + _dot(qb, r_prev.astype(BF16)) * xi_ref[p_i]
            sq = hr * hr
            ms_a = jnp.sum(jnp.where(lo_half, sq, 0.0), axis=-1, keepdims=True)
            ms_b = jnp.sum(jnp.where(lo_half, 0.0, sq), axis=-1, keepdims=True)
            ms = jnp.where(lo_half, ms_a, ms_b) * (1.0 / lay.rd)
            hn = hr * lax.rsqrt(ms + EPS) * rnorm_ref[:, pl_]
            o_ref[rows, mw + p_i * LANES:mw + (p_i + 1) * LANES] = (
                hn * (rg * jax.nn.sigmoid(rg))).astype(o_ref.dtype)
            kv = _dot_tn((kr * zeta_ref[p_i]).astype(BF16), rv.astype(BF16))
            r_st[p_i] = cd_ref[p_i] * r_prev + bmask_ref[...] * kv

        gv = jax.nn.gelu(proj_ref[rows, lay.gv:lay.gv + gw])
        gvn = gv * lax.rsqrt(jnp.mean(gv * gv, axis=-1, keepdims=True) + EPS) * gnorm_ref[...]
        for q_i in range(gw // LANES):
            gl = slice(q_i * LANES, (q_i + 1) * LANES)
            gu = jax.nn.gelu(proj_ref[rows, lay.gu + q_i * LANES:lay.gu + (q_i + 1) * LANES])
            gq = gvn[:, gl]
            vst = jnp.concatenate([jnp.where(lo_half, gq, 0.0).astype(BF16),
                                   jnp.where(lo_half, 0.0, gq).astype(BF16)], axis=0)
            mixed = _dot(wcat_ref[q_i], vst) + gbmat_ref[:, gl]
            o_ref[rows, mw + rw + q_i * LANES:mw + rw + (q_i + 1) * LANES] = (
                gu * mixed).astype(o_ref.dtype)
        return carry

    lax.fori_loop(0, ts // CHUNK, chunk, 0)


def _mixer(proj, tabs, lw, lay, batch, seq):
    t = proj.shape[0]
    ts = MIX_TILE
    nj = seq // ts
    d_mix = lay.mw + lay.rw + lay.gw
    row = lambda b, j: (b * nj + j, 0)
    pos = lambda b, j: (j, 0)
    c2 = lambda b, j: (0, 0)
    c3 = lambda b, j: (0, 0, 0)

    def full(a):
        return pl.BlockSpec(a.shape, c3 if a.ndim == 3 else c2)

    consts = [lw["convw"], lw["convb"], lw["bdq"], lw["bdk"], lw["bdv"], lw["gbias"], lw["mnorm"],
              lw["mskip"], lw["rnorm"], lw["gnorm"], lw["wcat"], lw["gbmat"],
              tabs["decay"], tabs["xi"], tabs["zeta"], tabs["cd"], tabs["bmask"]]
    est = (2 * ts * lay.n * 4 + 4 * ts * lay.rw * 4 + 2 * ts * d_mix * 2
           + (ts + 8) * lay.mw * 4 + ts * lay.mpad * 8 + ts * lay.mw * 4
           + 2 * sum(int(np.prod(a.shape)) * a.dtype.itemsize for a in consts)
           + (M_HEADS + lay.rw // LANES) * LANES * LANES * 4)
    return pl.pallas_call(
        functools.partial(_mixer_kernel, lay=lay),
        grid=(batch, nj),
        in_specs=[pl.BlockSpec((ts, lay.n), row),
                  pl.BlockSpec((ts, lay.rw), pos),
                  pl.BlockSpec((ts, lay.rw), pos)] + [full(a) for a in consts],
        out_specs=pl.BlockSpec((ts, d_mix), row),
        out_shape=jax.ShapeDtypeStruct((t, d_mix), BF16),
        scratch_shapes=[pltpu.VMEM((ts + 8, lay.mw), F32),
                        pltpu.VMEM((ts, lay.mpad), BF16),
                        pltpu.VMEM((ts, lay.mpad), F32),
                        pltpu.VMEM((ts, lay.mpad), BF16),
                        pltpu.VMEM((ts, lay.mw), F32),
                        pltpu.VMEM((M_HEADS, LANES, LANES), F32),
                        pltpu.VMEM((lay.rw // LANES, LANES, LANES), F32),
                        pltpu.VMEM((8, LANES), F32)],
        compiler_params=pltpu.CompilerParams(
            dimension_semantics=("arbitrary", "arbitrary"), vmem_limit_bytes=_vmem_limit(est)),
        name="mixer",
    )(proj, tabs["cos"], tabs["sin"], *consts)


def _tables(lay, seq):
    rd, half = lay.rd, lay.rd // 2
    l = CHUNK
    pos = jnp.arange(seq, dtype=F32)
    freqs = ROPE_THETA ** (-jnp.arange(half, dtype=F32) / half)
    ang = pos[:, None] * freqs[None, :]
    lane = np.arange(lay.rw) % LANES
    fi = jnp.asarray(lane % half)
    sign = jnp.asarray(np.where(lane < LANES // 2, -1.0, 1.0), F32)
    cos = jnp.take(jnp.cos(ang), fi, axis=1)
    sin = jnp.take(jnp.sin(ang), fi, axis=1) * sign[None, :]
    log_gamma = jnp.log(1.0 - 2.0 ** (-5.0 - jnp.arange(R_HEADS, dtype=F32)))
    p = jnp.arange(l, dtype=F32)
    causal = jnp.tril(jnp.ones((l, l), dtype=bool))
    rel = jnp.where(causal, p[:, None] - p[None, :], 0.0)
    decay = jnp.where(causal, jnp.exp(rel[None] * log_gamma[:, None, None]), 0.0)
    zeta = jnp.exp((l - 1.0 - p)[None, :] * log_gamma[:, None])
    xi = jnp.exp((p + 1.0)[None, :] * log_gamma[:, None])
    cdec = jnp.exp(l * log_gamma)
    npair = R_HEADS // 2
    decay_ab = decay.reshape(npair, 2, l, l).transpose(0, 2, 1, 3).reshape(npair, l, 2 * l)
    lane128 = np.arange(LANES)
    head_v = lane128 // rd
    head_k = (lane128 % (2 * half)) // half
    pair_heads = jnp.arange(R_HEADS).reshape(npair, 2)
    xi_ab = jnp.stack([xi[pair_heads[i]][jnp.asarray(head_v)].T for i in range(npair)])
    zeta_k = jnp.stack([zeta[pair_heads[i]][jnp.asarray(head_k)].T for i in range(npair)])
    cd = jnp.stack([jnp.broadcast_to(cdec[pair_heads[i]][jnp.asarray(head_k)][:, None], (LANES, LANES))
                    for i in range(npair)])
    bmask = jnp.asarray((head_k[:, None] == head_v[None, :]).astype(np.float32))
    return {"cos": cos, "sin": sin, "decay": decay_ab, "xi": xi_ab, "zeta": zeta_k, "cd": cd,
            "bmask": bmask}


def _block_diag_padded(w, lay):
    nb = w.shape[0]
    dense = jnp.einsum("nij,nm->nimj", w, jnp.eye(nb, dtype=w.dtype)).reshape(nb * M_BLOCK, nb * M_BLOCK)
    return _gather_cols(dense, lay.head_pad_index()).astype(BF16)


def _layer_weights(lay, layer, p):
    pad = lay.head_pad_index()
    gate_bias = jnp.zeros((1, LANES), F32)
    gate_bias = gate_bias.at[0, 0:M_HEADS].set(p["m_i_bias"][layer])
    gate_bias = gate_bias.at[0, M_HEADS:2 * M_HEADS].set(p["m_f_bias"][layer])
    ws = p["g_ws"][layer]
    causal = jnp.tril(jnp.ones((CHUNK, CHUNK), dtype=bool))
    wm = jnp.where(causal[None], ws, 0.0)
    ng = G_GROUPS // 2
    wcat = wm.reshape(ng, 2, CHUNK, CHUNK).transpose(0, 2, 1, 3).reshape(ng, CHUNK, 2 * CHUNK)
    gbmat = jnp.repeat(p["g_bs"][layer].T, lay.gw // G_GROUPS, axis=1)
    return {
        "convw": p["m_conv_w"][layer], "convb": p["m_conv_b"][layer][None, :],
        "bdq": _block_diag_padded(p["m_wq"][layer], lay),
        "bdk": _block_diag_padded(p["m_wk"][layer], lay),
        "bdv": _block_diag_padded(p["m_wv"][layer], lay),
        "gbias": gate_bias,
        "mnorm": _gather_cols(p["m_norm_g"][layer][None, :], pad),
        "mskip": p["m_skip"][layer][None, :],
        "rnorm": p["r_norm_g"][layer][None, :],
        "gnorm": p["g_norm_g"][layer][None, :],
        "wcat": wcat.astype(BF16), "gbmat": gbmat,
    }


def kernel(x, norm_mix_g, w_in, m_conv_w, m_conv_b, m_wq, m_wk, m_wv, m_i_bias, m_f_bias, m_norm_g,
           m_skip, r_norm_g, g_norm_g, g_ws, g_bs, w_out, norm_ff_g, w_ff1, w_ff2, final_norm_g):
    batch, seq, d = x.shape
    depth = w_in.shape[0]
    lay = _Layout(d)
    assert seq % MIX_TILE == 0 and (batch * seq) % TOKEN_TILE == 0 and MIX_TILE % CHUNK == 0
    assert w_ff1.shape[2] % FF_CHUNK == 0
    params = dict(m_conv_w=m_conv_w, m_conv_b=m_conv_b, m_wq=m_wq, m_wk=m_wk, m_wv=m_wv,
                  m_i_bias=m_i_bias, m_f_bias=m_f_bias, m_norm_g=m_norm_g, m_skip=m_skip,
                  r_norm_g=r_norm_g, g_norm_g=g_norm_g, g_ws=g_ws, g_bs=g_bs)
    tabs = _tables(lay, seq)
    w_in_p = _gather_cols(w_in, lay.in_perm()).astype(BF16)
    w_out_b = w_out.astype(BF16)
    w1_b = w_ff1.astype(BF16)
    w2_b = w_ff2.astype(BF16)

    x2 = x.reshape(batch * seq, d)
    for layer in range(depth):
        proj = _inproj(x2, norm_mix_g[layer][None, :], w_in_p[layer])
        mix = _mixer(proj, tabs, _layer_weights(lay, layer, params), lay, batch, seq)
        x2 = _ffn(mix, x2, w_out_b[layer], norm_ff_g[layer][None, :], w1_b[layer], w2_b[layer],
                  final_norm_g[None, :], final=(layer == depth - 1))
    return x2.reshape(batch, seq, d)
```

```python
import functools
import math

import numpy as np
import jax
import jax.numpy as jnp
from jax import lax
from jax.experimental import pallas as pl
from jax.experimental.pallas import tpu as pltpu

M_HEADS = 4
M_CONV = 4
M_BLOCK = 4
R_HEADS = 6
G_GROUPS = 4
CHUNK = 128
ROPE_THETA = 10000.0
EPS = 1e-6

LANES = 128
V7X_VMEM_BYTES = 64 * 1024 * 1024

TOKEN_TILE = 512
MIX_TILE = 512
FF_CHUNK = 1024

F32 = jnp.float32
BF16 = jnp.bfloat16


def _vmem_limit(nbytes):
    return int(min(nbytes + (16 << 20), V7X_VMEM_BYTES - (6 << 20)))


def _rms(x, g):
    return x * lax.rsqrt(jnp.mean(jnp.square(x), axis=-1, keepdims=True) + EPS) * g


def _dot(a, b):
    return jnp.dot(a, b, preferred_element_type=F32)


def _dot_nt(a, b):
    return lax.dot_general(a, b, (((1,), (1,)), ((), ())), preferred_element_type=F32)


def _dot_tn(a, b):
    return lax.dot_general(a, b, (((0,), (0,)), ((), ())), preferred_element_type=F32)


def _inproj_kernel(x_ref, g_ref, w_ref, o_ref):
    h = _rms(x_ref[...], g_ref[...]).astype(BF16)
    o_ref[...] = _dot(h, w_ref[...])


def _inproj(x2, g, w, layer):
    t, d = x2.shape
    n = w.shape[2]
    tm = TOKEN_TILE
    sel = lambda i: (layer, 0, 0)
    est = 2 * tm * d * 4 + 2 * d * n * 2 + 2 * tm * n * 4
    return pl.pallas_call(
        _inproj_kernel,
        grid=(t // tm,),
        in_specs=[pl.BlockSpec((tm, d), lambda i: (i, 0)),
                  pl.BlockSpec((None, 1, d), sel),
                  pl.BlockSpec((None, d, n), sel)],
        out_specs=pl.BlockSpec((tm, n), lambda i: (i, 0)),
        out_shape=jax.ShapeDtypeStruct((t, n), F32),
        compiler_params=pltpu.CompilerParams(
            dimension_semantics=("arbitrary",), vmem_limit_bytes=_vmem_limit(est)),
        name="inproj",
    )(x2, g, w)


def _ffn_kernel(mix_ref, x_ref, wo_ref, g_ref, w1_ref, w2_ref, gf_ref, o_ref, *, final):
    x1 = x_ref[...] + _dot(mix_ref[...], wo_ref[...])
    h2 = _rms(x1, g_ref[...]).astype(BF16)
    acc = x1
    for j in range(w1_ref.shape[1] // FF_CHUNK):
        cols = slice(j * FF_CHUNK, (j + 1) * FF_CHUNK)
        a = jnp.square(jnp.maximum(_dot(h2, w1_ref[:, cols]), 0.0)).astype(BF16)
        acc = acc + _dot(a, w2_ref[cols, :])
    if final:
        acc = _rms(acc, gf_ref[...])
    o_ref[...] = acc


def _ffn(mix, x2, wo, g, w1, w2, gf, layer, final):
    t, d = x2.shape
    dff = w1.shape[2]
    tm = TOKEN_TILE
    sel = lambda i: (layer, 0, 0)
    est = (2 * tm * d * 2 + 4 * tm * d * 4 + 2 * (d * d + 2 * d * dff) * 2
           + tm * FF_CHUNK * 6 + 3 * tm * d * 4)
    return pl.pallas_call(
        functools.partial(_ffn_kernel, final=final),
        grid=(t // tm,),
        in_specs=[pl.BlockSpec((tm, d), lambda i: (i, 0)),
                  pl.BlockSpec((tm, d), lambda i: (i, 0)),
                  pl.BlockSpec((None, d, d), sel),
                  pl.BlockSpec((None, 1, d), sel),
                  pl.BlockSpec((None, d, dff), sel),
                  pl.BlockSpec((None, dff, d), sel),
                  pl.BlockSpec((1, d), lambda i: (0, 0))],
        out_specs=pl.BlockSpec((tm, d), lambda i: (i, 0)),
        out_shape=jax.ShapeDtypeStruct((t, d), F32),
        compiler_params=pltpu.CompilerParams(
            dimension_semantics=("arbitrary",), vmem_limit_bytes=_vmem_limit(est)),
        name="ffn_final" if final else "ffn",
    )(mix, x2, wo, g, w1, w2, gf)


class _Layout:
    def __init__(self, d_model):
        self.mw = 3 * d_model // 8
        self.rw = 3 * d_model // 8
        self.gw = d_model - self.mw - self.rw
        self.md = self.mw // M_HEADS
        self.rd = self.rw // R_HEADS
        assert self.md <= LANES and 2 * self.rd == LANES and self.gw // G_GROUPS * 2 == LANES
        assert self.mw % LANES == 0 and self.rw % LANES == 0 and self.gw % LANES == 0
        o = 0
        self.mx = o; o += self.mw
        self.mz = o; o += self.mw
        self.rq = o; o += self.rw
        self.rk = o; o += self.rw
        self.rv = o; o += self.rw
        self.rg = o; o += self.rw
        self.gu = o; o += self.gw
        self.gv = o; o += self.gw
        self.gate = o; o += LANES
        self.n = o
        self.mpad = M_HEADS * LANES

    def relayout_in_proj(self, w_in):
        mw, rw, gw, rd = self.mw, self.rw, self.gw, self.rd
        depth, d, _ = w_in.shape
        o_gate = 2 * mw
        o_rq = o_gate + 2 * M_HEADS
        o_rv = o_rq + 2 * rw

        def rotary_layout(seg):
            s = seg.reshape(depth, d, R_HEADS // 2, 2, 2, rd // 2)
            return s.transpose(0, 1, 2, 4, 3, 5).reshape(depth, d, rw)

        parts = [w_in[:, :, 0:2 * mw],
                 rotary_layout(w_in[:, :, o_rq:o_rq + rw]),
                 rotary_layout(w_in[:, :, o_rq + rw:o_rq + 2 * rw]),
                 w_in[:, :, o_rv:o_rv + 2 * rw + 2 * gw],
                 w_in[:, :, o_gate:o_gate + 2 * M_HEADS],
                 jnp.zeros((depth, d, LANES - 2 * M_HEADS), w_in.dtype)]
        return jnp.concatenate(parts, axis=-1).astype(BF16)

    def pad_heads(self, a):
        a4 = a.reshape(a.shape[:-1] + (M_HEADS, self.md))
        a4 = jnp.pad(a4, [(0, 0)] * (a4.ndim - 1) + [(0, LANES - self.md)])
        return a4.reshape(a.shape[:-1] + (self.mpad,))


def _mixer_kernel(proj_ref, cos_ref, sin_ref, convw_ref, convb_ref, bdq_ref, bdk_ref, bdv_ref,
                  gbias_ref, mnorm_ref, mskip_ref, rnorm_ref, gnorm_ref, wcat_ref, gbmat_ref,
                  decay_ref, xi_ref, zeta_ref, cd_ref, bmask_ref,
                  o_ref,
                  xbuf, q_s, k_s, v_s, xc_s, c_st, r_st, m_st, *, lay):
    ts = proj_ref.shape[0]
    md, mw, rw, gw = lay.md, lay.mw, lay.rw, lay.gw
    j = pl.program_id(1)

    @pl.when(j == 0)
    def _():
        c_st[...] = jnp.zeros_like(c_st)
        r_st[...] = jnp.zeros_like(r_st)
        m_st[...] = jnp.zeros_like(m_st)
        xbuf[0:8, :] = jnp.zeros((8, mw), F32)

    @pl.when(j > 0)
    def _():
        xbuf[0:8, :] = xbuf[ts:ts + 8, :]

    mx = proj_ref[:, lay.mx:lay.mx + mw]
    xbuf[8:ts + 8, :] = mx
    conv = convb_ref[...] + convw_ref[M_CONV - 1:M_CONV, :] * mx
    for t in range(M_CONV - 1):
        conv = conv + convw_ref[t:t + 1, :] * xbuf[8 - (M_CONV - 1) + t:8 - (M_CONV - 1) + t + ts, :]
    xc = conv * jax.nn.sigmoid(conv)
    xc_s[...] = xc
    xcb = xc.astype(BF16)
    lane_p = lax.broadcasted_iota(jnp.int32, (1, lay.mpad), 1) % LANES
    q_s[...] = _dot(xcb, bdq_ref[...]).astype(BF16)
    k_s[...] = _dot(xcb, bdk_ref[...]) * (md ** -0.5)
    v_s[...] = (_dot(mx.astype(BF16), bdv_ref[...]) + (lane_p == md).astype(F32)).astype(BF16)

    li = lax.broadcasted_iota(jnp.int32, (CHUNK, LANES), 1)
    ri = lax.broadcasted_iota(jnp.int32, (CHUNK, LANES), 0)
    causal = li <= ri
    tril = causal.astype(BF16)
    head_lane = li < M_HEADS
    lo_half = li < (LANES // 2)
    rot_a = (li % (LANES // 2)) < (LANES // 4)

    def chunk(c, carry):
        r0 = pl.multiple_of(c * CHUNK, CHUNK)
        rows = pl.ds(r0, CHUNK)

        g = proj_ref[rows, lay.gate:lay.gate + LANES] + gbias_ref[...]
        ig = jnp.where(head_lane, g, 0.0)
        logf = jnp.where(head_lane, jax.nn.log_sigmoid(pltpu.roll(g, LANES - M_HEADS, axis=1)), 0.0)
        a1 = logf.astype(BF16)
        r1 = logf - a1.astype(F32)
        a2 = r1.astype(BF16)
        a3 = (r1 - a2.astype(F32)).astype(BF16)
        bcum = (_dot(tril, a3) + _dot(tril, a2)) + _dot(tril, a1)
        b_end = bcum[CHUNK - 1:CHUNK, :]
        w = ig - bcum
        w_t = w.T
        m_prev = m_st[0:1, :]
        inter_log = bcum + m_prev
        w_end = b_end + w
        a_c = jnp.max(w_end, axis=0, keepdims=True)
        e_end = jnp.exp(w_end - a_c)
        m_new = jnp.maximum(b_end + m_prev, a_c)
        s_old = jnp.exp(b_end + m_prev - m_new)
        s_new = jnp.exp(a_c - m_new)
        m_st[0:1, :] = m_new

        hms = []
        for h in range(M_HEADS):
            hl = slice(h * LANES, (h + 1) * LANES)
            qh = q_s[rows, hl]
            kh = k_s[rows, hl]
            vh = v_s[rows, hl]
            s = _dot_nt(qh, kh.astype(BF16))
            dm = jnp.where(causal, bcum[:, h:h + 1] + w_t[h:h + 1, :], -jnp.inf)
            il = inter_log[:, h:h + 1]
            m_t = jnp.maximum(il, jnp.max(dm, axis=-1, keepdims=True))
            s_inter = jnp.exp(il - m_t)
            p = s * jnp.exp(dm - m_t)
            c_prev = c_st[h]
            inter = s_inter * _dot(qh, c_prev.astype(BF16))
            tot = _dot(p.astype(BF16), vh) + inter
            den = jnp.sum(p + jnp.where(li == md, inter, 0.0), axis=-1, keepdims=True)
            rden = 1.0 / jnp.maximum(jnp.abs(den), jnp.exp(-m_t))
            hout = jnp.where(li < md, tot * rden, 0.0)
            ms = jnp.sum(hout * hout, axis=-1, keepdims=True) * (1.0 / md)
            hms.append(hout * lax.rsqrt(ms + EPS) * mnorm_ref[:, hl])
            ke = (kh * e_end[:, h:h + 1]).astype(BF16)
            c_st[h] = s_old[:, h:h + 1] * c_prev + s_new[:, h:h + 1] * _dot_tn(ke, vh)

        outs = []
        for t in range(mw // LANES):
            lo_h, lo_off = divmod(t * LANES, md)
            pieces = None
            hh, start = lo_h, lo_off
            pos = 0
            while pos < LANES:
                take = min(md - start, LANES - pos)
                src = hms[hh] if (pos - start) % LANES == 0 else pltpu.roll(hms[hh], (pos - start) % LANES, axis=1)
                sel = (li >= pos) & (li < pos + take)
                pieces = jnp.where(sel, src, 0.0) if pieces is None else jnp.where(sel, src, pieces)
                pos += take
                hh, start = hh + 1, 0
            outs.append(pieces)
        hm = jnp.concatenate(outs, axis=1)
        mz = proj_ref[rows, lay.mz:lay.mz + mw]
        out_m = (hm + mskip_ref[...] * xc_s[rows, :]) * (mz * jax.nn.sigmoid(mz))
        o_ref[rows, 0:mw] = out_m.astype(o_ref.dtype)

        for p_i in range(rw // LANES):
            pl_ = slice(p_i * LANES, (p_i + 1) * LANES)
            cs = cos_ref[rows, pl_]
            sn = sin_ref[rows, pl_]
            rq = proj_ref[rows, lay.rq + p_i * LANES:lay.rq + (p_i + 1) * LANES]
            rk = proj_ref[rows, lay.rk + p_i * LANES:lay.rk + (p_i + 1) * LANES]
            rv = proj_ref[rows, lay.rv + p_i * LANES:lay.rv + (p_i + 1) * LANES]
            rg = proj_ref[rows, lay.rg + p_i * LANES:lay.rg + (p_i + 1) * LANES]
            qb = (rq * cs + pltpu.roll(rq, LANES // 2, axis=1) * sn).astype(BF16)
            kr = (rk * cs + pltpu.roll(rk, LANES // 2, axis=1) * sn) * (lay.rd ** -0.5)
            kcat = jnp.concatenate([jnp.where(rot_a, kr, 0.0).astype(BF16),
                                    jnp.where(rot_a, 0.0, kr).astype(BF16)], axis=0)
            s_ab = _dot_nt(qb, kcat)
            p_ab = (s_ab * decay_ref[p_i]).astype(BF16)
            vst = jnp.concatenate([jnp.where(lo_half, rv, 0.0).astype(BF16),
                                   jnp.where(lo_half, 0.0, rv).astype(BF16)], axis=0)
            r_prev = r_st[p_i]
            hr = _dot(p_ab, vst) + _dot(qb, r_prev.astype(BF16)) * xi_ref[p_i]
            sq = hr * hr
            ms_a = jnp.sum(jnp.where(lo_half, sq, 0.0), axis=-1, keepdims=True)
            ms_b = jnp.sum(jnp.where(lo_half, 0.0, sq), axis=-1, keepdims=True)
            ms = jnp.where(lo_half, ms_a, ms_b) * (1.0 / lay.rd)
            hn = hr * lax.rsqrt(ms + EPS) * rnorm_ref[:, pl_]
            o_ref[rows, mw + p_i * LANES:mw + (p_i + 1) * LANES] = (
                hn * (rg * jax.nn.sigmoid(rg))).astype(o_ref.dtype)
            kv = _dot_tn((kr * zeta_ref[p_i]).astype(BF16), rv.astype(BF16))
            r_st[p_i] = cd_ref[p_i] * r_prev + bmask_ref[...] * kv

        gv = jax.nn.gelu(proj_ref[rows, lay.gv:lay.gv + gw])
        gvn = gv * lax.rsqrt(jnp.mean(gv * gv, axis=-1, keepdims=True) + EPS) * gnorm_ref[...]
        for q_i in range(gw // LANES):
            gl = slice(q_i * LANES, (q_i + 1) * LANES)
            gu = jax.nn.gelu(proj_ref[rows, lay.gu + q_i * LANES:lay.gu + (q_i + 1) * LANES])
            gq = gvn[:, gl]
            vst = jnp.concatenate([jnp.where(lo_half, gq, 0.0).astype(BF16),
                                   jnp.where(lo_half, 0.0, gq).astype(BF16)], axis=0)
            mixed = _dot(wcat_ref[q_i], vst) + gbmat_ref[:, gl]
            o_ref[rows, mw + rw + q_i * LANES:mw + rw + (q_i + 1) * LANES] = (
                gu * mixed).astype(o_ref.dtype)
        return carry

    lax.fori_loop(0, ts // CHUNK, chunk, 0)


def _mixer(proj, tabs, lw, layer, lay, batch, seq):
    t = proj.shape[0]
    ts = MIX_TILE
    nj = seq // ts
    d_mix = lay.mw + lay.rw + lay.gw
    row = lambda b, j: (b * nj + j, 0)
    pos = lambda b, j: (j, 0)

    def layer_slab(a):
        return pl.BlockSpec((None,) + a.shape[1:], lambda b, j: (layer,) + (0,) * (a.ndim - 1))

    def whole(a):
        return pl.BlockSpec(a.shape, lambda b, j: (0,) * a.ndim)

    lws = [lw[k] for k in ("convw", "convb", "bdq", "bdk", "bdv", "gbias", "mnorm", "mskip", "rnorm",
                           "gnorm", "wcat", "gbmat")]
    tbs = [tabs[k] for k in ("decay", "xi", "zeta", "cd", "bmask")]
    est = (2 * ts * lay.n * 4 + 4 * ts * lay.rw * 4 + 2 * ts * d_mix * 2
           + (ts + 8) * lay.mw * 4 + ts * lay.mpad * 8 + ts * lay.mw * 4
           + 2 * sum(int(np.prod(a.shape[1:])) * a.dtype.itemsize for a in lws)
           + 2 * sum(int(np.prod(a.shape)) * a.dtype.itemsize for a in tbs)
           + (M_HEADS + lay.rw // LANES) * LANES * LANES * 4)
    return pl.pallas_call(
        functools.partial(_mixer_kernel, lay=lay),
        grid=(batch, nj),
        in_specs=[pl.BlockSpec((ts, lay.n), row),
                  pl.BlockSpec((ts, lay.rw), pos),
                  pl.BlockSpec((ts, lay.rw), pos)] + [layer_slab(a) for a in lws] + [whole(a) for a in tbs],
        out_specs=pl.BlockSpec((ts, d_mix), row),
        out_shape=jax.ShapeDtypeStruct((t, d_mix), BF16),
        scratch_shapes=[pltpu.VMEM((ts + 8, lay.mw), F32),
                        pltpu.VMEM((ts, lay.mpad), BF16),
                        pltpu.VMEM((ts, lay.mpad), F32),
                        pltpu.VMEM((ts, lay.mpad), BF16),
                        pltpu.VMEM((ts, lay.mw), F32),
                        pltpu.VMEM((M_HEADS, LANES, LANES), F32),
                        pltpu.VMEM((lay.rw // LANES, LANES, LANES), F32),
                        pltpu.VMEM((8, LANES), F32)],
        compiler_params=pltpu.CompilerParams(
            dimension_semantics=("arbitrary", "arbitrary"), vmem_limit_bytes=_vmem_limit(est)),
        name="mixer",
    )(proj, tabs["cos"], tabs["sin"], *lws, *tbs)


def _tables(lay, seq):
    rd, half = lay.rd, lay.rd // 2
    l = CHUNK
    pos = jnp.arange(seq, dtype=F32)
    freqs = ROPE_THETA ** (-jnp.arange(half, dtype=F32) / half)
    ang = pos[:, None] * freqs[None, :]
    lane = np.arange(lay.rw) % LANES
    sign = jnp.asarray(np.where(lane < LANES // 2, -1.0, 1.0), F32)
    cos = jnp.tile(jnp.cos(ang), (1, lay.rw // half))
    sin = jnp.tile(jnp.sin(ang), (1, lay.rw // half)) * sign[None, :]
    log_gamma = jnp.log(1.0 - 2.0 ** (-5.0 - jnp.arange(R_HEADS, dtype=F32)))
    p = jnp.arange(l, dtype=F32)
    causal = jnp.tril(jnp.ones((l, l), dtype=bool))
    rel = jnp.where(causal, p[:, None] - p[None, :], 0.0)
    decay = jnp.where(causal, jnp.exp(rel[None] * log_gamma[:, None, None]), 0.0)
    zeta = jnp.exp((l - 1.0 - p)[None, :] * log_gamma[:, None])
    xi = jnp.exp((p + 1.0)[None, :] * log_gamma[:, None])
    cdec = jnp.exp(l * log_gamma)
    npair = R_HEADS // 2
    decay_ab = decay.reshape(npair, 2, l, l).transpose(0, 2, 1, 3).reshape(npair, l, 2 * l)
    by_v = lambda a: jnp.repeat(a.reshape(npair, 2, -1), rd, axis=1)
    by_k = lambda a: jnp.tile(jnp.repeat(a.reshape(npair, 2, -1), half, axis=1), (1, 2, 1))
    xi_ab = by_v(xi).transpose(0, 2, 1)
    zeta_k = by_k(zeta).transpose(0, 2, 1)
    cd = jnp.broadcast_to(by_k(cdec), (npair, LANES, LANES))
    lane128 = np.arange(LANES)
    head_v = lane128 // rd
    head_k = (lane128 % (2 * half)) // half
    bmask = jnp.asarray((head_k[:, None] == head_v[None, :]).astype(np.float32))
    return {"cos": cos, "sin": sin, "decay": decay_ab, "xi": xi_ab, "zeta": zeta_k, "cd": cd,
            "bmask": bmask}


def _block_diag_padded(w, lay):
    depth, nb = w.shape[0], w.shape[1]
    rows = w.reshape(depth, nb * M_BLOCK, M_BLOCK)
    tiled = jnp.tile(rows, (1, 1, lay.mpad // M_BLOCK))
    r = np.arange(nb * M_BLOCK)[:, None]
    c = np.arange(lay.mpad)[None, :]
    packed_c = (c // LANES) * lay.md + c % LANES
    keep = (c % LANES < lay.md) & (r // M_BLOCK == packed_c // M_BLOCK)
    return jnp.where(jnp.asarray(keep), tiled, 0.0).astype(BF16)


def _mixer_weights(lay, p):
    depth = p["m_wq"].shape[0]
    gate_bias = jnp.concatenate(
        [p["m_i_bias"], p["m_f_bias"], jnp.zeros((depth, LANES - 2 * M_HEADS), F32)], axis=-1)
    causal = jnp.tril(jnp.ones((CHUNK, CHUNK), dtype=bool))
    wm = jnp.where(causal, p["g_ws"], 0.0)
    ng = G_GROUPS // 2
    wcat = (wm.reshape(depth, ng, 2, CHUNK, CHUNK).transpose(0, 1, 3, 2, 4)
            .reshape(depth, ng, CHUNK, 2 * CHUNK))
    gbmat = jnp.repeat(p["g_bs"].transpose(0, 2, 1), lay.gw // G_GROUPS, axis=2)
    row = lambda a: a[:, None, :]
    return {
        "convw": p["m_conv_w"], "convb": row(p["m_conv_b"]),
        "bdq": _block_diag_padded(p["m_wq"], lay),
        "bdk": _block_diag_padded(p["m_wk"], lay),
        "bdv": _block_diag_padded(p["m_wv"], lay),
        "gbias": row(gate_bias),
        "mnorm": row(lay.pad_heads(p["m_norm_g"])),
        "mskip": row(p["m_skip"]),
        "rnorm": row(p["r_norm_g"]),
        "gnorm": row(p["g_norm_g"]),
        "wcat": wcat.astype(BF16), "gbmat": gbmat,
    }


def kernel(x, norm_mix_g, w_in, m_conv_w, m_conv_b, m_wq, m_wk, m_wv, m_i_bias, m_f_bias, m_norm_g,
           m_skip, r_norm_g, g_norm_g, g_ws, g_bs, w_out, norm_ff_g, w_ff1, w_ff2, final_norm_g):
    batch, seq, d = x.shape
    depth = w_in.shape[0]
    lay = _Layout(d)
    assert seq % MIX_TILE == 0 and (batch * seq) % TOKEN_TILE == 0 and MIX_TILE % CHUNK == 0
    assert w_ff1.shape[2] % FF_CHUNK == 0
    params = dict(m_conv_w=m_conv_w, m_conv_b=m_conv_b, m_wq=m_wq, m_wk=m_wk, m_wv=m_wv,
                  m_i_bias=m_i_bias, m_f_bias=m_f_bias, m_norm_g=m_norm_g, m_skip=m_skip,
                  r_norm_g=r_norm_g, g_norm_g=g_norm_g, g_ws=g_ws, g_bs=g_bs)
    tabs = _tables(lay, seq)
    lw = _mixer_weights(lay, params)
    w_in_p = lay.relayout_in_proj(w_in)
    w_out_b = w_out.astype(BF16)
    w1_b = w_ff1.astype(BF16)
    w2_b = w_ff2.astype(BF16)
    g_mix = norm_mix_g[:, None, :]
    g_ff = norm_ff_g[:, None, :]

    x2 = x.reshape(batch * seq, d)
    for layer in range(depth):
        proj = _inproj(x2, g_mix, w_in_p, layer)
        mix = _mixer(proj, tabs, lw, layer, lay, batch, seq)
        x2 = _ffn(mix, x2, w_out_b, g_ff, w1_b, w2_b, final_norm_g[None, :], layer,
                  final=(layer == depth - 1))
    return x2.reshape(batch, seq, d)
```

```python
import functools
import math

import numpy as np
import jax
import jax.numpy as jnp
from jax import lax
from jax.experimental import pallas as pl
from jax.experimental.pallas import tpu as pltpu

M_HEADS = 4
M_CONV = 4
M_BLOCK = 4
R_HEADS = 6
G_GROUPS = 4
CHUNK = 128
ROPE_THETA = 10000.0
EPS = 1e-6

LANES = 128
V7X_VMEM_BYTES = 64 * 1024 * 1024

TOKEN_TILE = 512
MIX_TILE = 512
FF_CHUNK = 1024

F32 = jnp.float32
BF16 = jnp.bfloat16


def _vmem_limit(nbytes):
    return int(min(nbytes + (16 << 20), V7X_VMEM_BYTES - (6 << 20)))


def _rms(x, g):
    return x * lax.rsqrt(jnp.mean(jnp.square(x), axis=-1, keepdims=True) + EPS) * g


def _dot(a, b):
    return jnp.dot(a, b, preferred_element_type=F32)


def _dot_nt(a, b):
    return lax.dot_general(a, b, (((1,), (1,)), ((), ())), preferred_element_type=F32)


def _dot_tn(a, b):
    return lax.dot_general(a, b, (((0,), (0,)), ((), ())), preferred_element_type=F32)


def _inproj_kernel(x_ref, g_ref, w_ref, o_ref):
    h = _rms(x_ref[...], g_ref[...]).astype(BF16)
    o_ref[...] = _dot(h, w_ref[...])


def _inproj(x2, g, w, layer):
    t, d = x2.shape
    n = w.shape[2]
    tm = TOKEN_TILE
    sel = lambda i: (layer, 0, 0)
    est = 2 * tm * d * 4 + 2 * d * n * 2 + 2 * tm * n * 4
    return pl.pallas_call(
        _inproj_kernel,
        grid=(t // tm,),
        in_specs=[pl.BlockSpec((tm, d), lambda i: (i, 0)),
                  pl.BlockSpec((None, 1, d), sel),
                  pl.BlockSpec((None, d, n), sel)],
        out_specs=pl.BlockSpec((tm, n), lambda i: (i, 0)),
        out_shape=jax.ShapeDtypeStruct((t, n), F32),
        compiler_params=pltpu.CompilerParams(
            dimension_semantics=("arbitrary",), vmem_limit_bytes=_vmem_limit(est)),
        name="inproj",
    )(x2, g, w)


def _ffn_kernel(mix_ref, x_ref, wo_ref, g_ref, w1_ref, w2_ref, gf_ref, o_ref, *, final):
    x1 = x_ref[...] + _dot(mix_ref[...], wo_ref[...])
    h2 = _rms(x1, g_ref[...]).astype(BF16)
    acc = x1
    for j in range(w1_ref.shape[1] // FF_CHUNK):
        cols = slice(j * FF_CHUNK, (j + 1) * FF_CHUNK)
        a = jnp.square(jnp.maximum(_dot(h2, w1_ref[:, cols]), 0.0)).astype(BF16)
        acc = acc + _dot(a, w2_ref[cols, :])
    if final:
        acc = _rms(acc, gf_ref[...])
    o_ref[...] = acc


def _ffn(mix, x2, wo, g, w1, w2, gf, layer, final):
    t, d = x2.shape
    dff = w1.shape[2]
    tm = TOKEN_TILE
    sel = lambda i: (layer, 0, 0)
    est = (2 * tm * d * 2 + 4 * tm * d * 4 + 2 * (d * d + 2 * d * dff) * 2
           + tm * FF_CHUNK * 6 + 3 * tm * d * 4)
    return pl.pallas_call(
        functools.partial(_ffn_kernel, final=final),
        grid=(t // tm,),
        in_specs=[pl.BlockSpec((tm, d), lambda i: (i, 0)),
                  pl.BlockSpec((tm, d), lambda i: (i, 0)),
                  pl.BlockSpec((None, d, d), sel),
                  pl.BlockSpec((None, 1, d), sel),
                  pl.BlockSpec((None, d, dff), sel),
                  pl.BlockSpec((None, dff, d), sel),
                  pl.BlockSpec((1, d), lambda i: (0, 0))],
        out_specs=pl.BlockSpec((tm, d), lambda i: (i, 0)),
        out_shape=jax.ShapeDtypeStruct((t, d), F32),
        compiler_params=pltpu.CompilerParams(
            dimension_semantics=("arbitrary",), vmem_limit_bytes=_vmem_limit(est)),
        name="ffn_final" if final else "ffn",
    )(mix, x2, wo, g, w1, w2, gf)


class _Layout:
    def __init__(self, d_model):
        self.mw = 3 * d_model // 8
        self.rw = 3 * d_model // 8
        self.gw = d_model - self.mw - self.rw
        self.md = self.mw // M_HEADS
        self.rd = self.rw // R_HEADS
        assert self.md <= LANES and 2 * self.rd == LANES and self.gw // G_GROUPS * 2 == LANES
        assert self.mw % LANES == 0 and self.rw % LANES == 0 and self.gw % LANES == 0
        o = 0
        self.mx = o; o += self.mw
        self.mz = o; o += self.mw
        self.rq = o; o += self.rw
        self.rk = o; o += self.rw
        self.rv = o; o += self.rw
        self.rg = o; o += self.rw
        self.gu = o; o += self.gw
        self.gv = o; o += self.gw
        self.gi = o; o += LANES
        self.gf = o; o += LANES
        self.n = o
        self.mpad = M_HEADS * LANES

    def relayout_in_proj(self, w_in):
        mw, rw, gw, rd = self.mw, self.rw, self.gw, self.rd
        depth, d, _ = w_in.shape
        o_gate = 2 * mw
        o_rq = o_gate + 2 * M_HEADS
        o_rv = o_rq + 2 * rw

        def rotary_layout(seg):
            s = seg.reshape(depth, d, R_HEADS // 2, 2, 2, rd // 2)
            return s.transpose(0, 1, 2, 4, 3, 5).reshape(depth, d, rw)

        parts = [w_in[:, :, 0:2 * mw],
                 rotary_layout(w_in[:, :, o_rq:o_rq + rw]),
                 rotary_layout(w_in[:, :, o_rq + rw:o_rq + 2 * rw]),
                 w_in[:, :, o_rv:o_rv + 2 * rw + 2 * gw],
                 w_in[:, :, o_gate:o_gate + M_HEADS],
                 jnp.zeros((depth, d, LANES - M_HEADS), w_in.dtype),
                 w_in[:, :, o_gate + M_HEADS:o_gate + 2 * M_HEADS],
                 w_in[:, :, o_gate + M_HEADS:o_gate + 2 * M_HEADS],
                 jnp.zeros((depth, d, LANES - 2 * M_HEADS), w_in.dtype)]
        return jnp.concatenate(parts, axis=-1).astype(BF16)


def _mixer_kernel(proj_ref, cos_ref, sin_ref, convw_ref, convb_ref, bdq_ref, bdk_ref, bdvt_ref,
                  gbi_ref, gbf_ref, mnormt_ref, mskip_ref, rnorm_ref, gnorm_ref, wcat_ref, gbmat_ref,
                  decay_ref, xi_ref, zeta_ref, cd_ref, bmask_ref,
                  o_ref,
                  xbuf, q_s, k_s, vt_s, xc_s, c_st, r_st, m_st, *, lay):
    ts = proj_ref.shape[0]
    md, mw, rw, gw = lay.md, lay.mw, lay.rw, lay.gw
    j = pl.program_id(1)

    @pl.when(j == 0)
    def _():
        c_st[...] = jnp.zeros_like(c_st)
        r_st[...] = jnp.zeros_like(r_st)
        m_st[...] = jnp.zeros_like(m_st)
        xbuf[0:8, :] = jnp.zeros((8, mw), F32)

    @pl.when(j > 0)
    def _():
        xbuf[0:8, :] = xbuf[ts:ts + 8, :]

    mx = proj_ref[:, lay.mx:lay.mx + mw]
    xbuf[8:ts + 8, :] = mx
    conv = convb_ref[...] + convw_ref[M_CONV - 1:M_CONV, :] * mx
    for t in range(M_CONV - 1):
        conv = conv + convw_ref[t:t + 1, :] * xbuf[8 - (M_CONV - 1) + t:8 - (M_CONV - 1) + t + ts, :]
    xc = conv * jax.nn.sigmoid(conv)
    xc_s[...] = xc
    xcb = xc.astype(BF16)
    q_s[...] = _dot(xcb, bdq_ref[...]).astype(BF16)
    k_s[...] = (_dot(xcb, bdk_ref[...]) * (md ** -0.5)).astype(BF16)
    row_p = lax.broadcasted_iota(jnp.int32, (lay.mpad, 1), 0) % LANES
    vt = _dot_nt(bdvt_ref[...], mx.astype(BF16)) + (row_p == md).astype(F32)
    for c in range(ts // CHUNK):
        vt_s[c] = vt[:, c * CHUNK:(c + 1) * CHUNK]

    li = lax.broadcasted_iota(jnp.int32, (CHUNK, LANES), 1)
    ri = lax.broadcasted_iota(jnp.int32, (CHUNK, LANES), 0)
    causal = li <= ri
    causal_t = ri <= li
    tril = causal.astype(BF16)
    lo_half = li < (LANES // 2)
    rot_a = (li % (LANES // 2)) < (LANES // 4)

    def chunk(c, carry):
        r0 = pl.multiple_of(c * CHUNK, CHUNK)
        rows = pl.ds(r0, CHUNK)

        gi = proj_ref[rows, lay.gi:lay.gi + LANES] + gbi_ref[...]
        gf = proj_ref[rows, lay.gf:lay.gf + LANES] + gbf_ref[...]
        logf = jnp.where(li < 2 * M_HEADS, jax.nn.log_sigmoid(gf), 0.0)
        a1 = logf.astype(BF16)
        r1 = logf - a1.astype(F32)
        a2 = r1.astype(BF16)
        a3 = (r1 - a2.astype(F32)).astype(BF16)
        bcum = (_dot(tril, a3) + _dot(tril, a2)) + _dot(tril, a1)
        wb = jnp.where(li < M_HEADS, gi - bcum, bcum)
        wb_t = wb.T

        hn_t = []
        for h in range(M_HEADS):
            hl = slice(h * LANES, (h + 1) * LANES)
            qh = q_s[rows, hl]
            kh = k_s[rows, hl]
            vth = vt_s[c, hl, :]
            w_row = wb_t[h:h + 1, :]
            b_row = wb_t[M_HEADS + h:M_HEADS + h + 1, :]
            b_end = b_row[:, CHUNK - 1:CHUNK]
            w_end = b_end + w_row
            a_c = jnp.max(w_end, axis=-1, keepdims=True)
            e_row = jnp.exp(w_end - a_c)
            m_prev = m_st[h:h + 1, :]
            m_new = jnp.maximum(b_end + m_prev, a_c)
            s_old = jnp.exp(b_end + m_prev - m_new)
            s_new = jnp.exp(a_c - m_new)
            m_st[h:h + 1, :] = m_new

            il = b_row + m_prev
            dm = jnp.where(causal_t, wb[:, h:h + 1] + b_row, -jnp.inf)
            m_t = jnp.maximum(il, jnp.max(dm, axis=0, keepdims=True))
            s_inter = jnp.exp(il - m_t)
            c_prev = c_st[h]
            kq = _dot_nt(jnp.concatenate([kh, c_prev.astype(BF16)], axis=0), qh)
            p_t = kq[0:CHUNK] * jnp.exp(dm - m_t)
            inter = s_inter * kq[CHUNK:2 * CHUNK]
            tot = _dot(vth.astype(BF16), p_t.astype(BF16)) + inter
            den = jnp.sum(p_t, axis=0, keepdims=True) + inter[md:md + 1, :]
            rden = 1.0 / jnp.maximum(jnp.abs(den), jnp.exp(-m_t))
            hout = tot[0:md, :] * rden
            ms = jnp.sum(hout * hout, axis=0, keepdims=True) * (1.0 / md)
            hn_t.append(hout * lax.rsqrt(ms + EPS) * mnormt_ref[h * md:(h + 1) * md, :])
            c_st[h] = s_old * c_prev + s_new * _dot((vth * e_row).astype(BF16), kh)

        hm_t = jnp.concatenate(hn_t, axis=0)
        hm = jnp.concatenate([hm_t[t * LANES:(t + 1) * LANES, :].T for t in range(mw // LANES)], axis=1)
        mz = proj_ref[rows, lay.mz:lay.mz + mw]
        out_m = (hm + mskip_ref[...] * xc_s[rows, :]) * (mz * jax.nn.sigmoid(mz))
        o_ref[rows, 0:mw] = out_m.astype(o_ref.dtype)

        for p_i in range(rw // LANES):
            pl_ = slice(p_i * LANES, (p_i + 1) * LANES)
            cs = cos_ref[rows, pl_]
            sn = sin_ref[rows, pl_]
            rq = proj_ref[rows, lay.rq + p_i * LANES:lay.rq + (p_i + 1) * LANES]
            rk = proj_ref[rows, lay.rk + p_i * LANES:lay.rk + (p_i + 1) * LANES]
            rv = proj_ref[rows, lay.rv + p_i * LANES:lay.rv + (p_i + 1) * LANES]
            rg = proj_ref[rows, lay.rg + p_i * LANES:lay.rg + (p_i + 1) * LANES]
            qb = (rq * cs + pltpu.roll(rq, LANES // 2, axis=1) * sn).astype(BF16)
            kr = (rk * cs + pltpu.roll(rk, LANES // 2, axis=1) * sn) * (lay.rd ** -0.5)
            kcat = jnp.concatenate([jnp.where(rot_a, kr, 0.0).astype(BF16),
                                    jnp.where(rot_a, 0.0, kr).astype(BF16)], axis=0)
            s_ab = _dot_nt(qb, kcat)
            p_ab = (s_ab * decay_ref[p_i]).astype(BF16)
            vst = jnp.concatenate([jnp.where(lo_half, rv, 0.0).astype(BF16),
                                   jnp.where(lo_half, 0.0, rv).astype(BF16)], axis=0)
            r_prev = r_st[p_i]
            hr = _dot(p_ab, vst) + _dot(qb, r_prev.astype(BF16)) * xi_ref[p_i]
            sq = hr * hr
            ms_a = jnp.sum(jnp.where(lo_half, sq, 0.0), axis=-1, keepdims=True)
            ms_b = jnp.sum(jnp.where(lo_half, 0.0, sq), axis=-1, keepdims=True)
            ms = jnp.where(lo_half, ms_a, ms_b) * (1.0 / lay.rd)
            hn = hr * lax.rsqrt(ms + EPS) * rnorm_ref[:, pl_]
            o_ref[rows, mw + p_i * LANES:mw + (p_i + 1) * LANES] = (
                hn * (rg * jax.nn.sigmoid(rg))).astype(o_ref.dtype)
            kv = _dot_tn((kr * zeta_ref[p_i]).astype(BF16), rv.astype(BF16))
            r_st[p_i] = cd_ref[p_i] * r_prev + bmask_ref[...] * kv

        gv = jax.nn.gelu(proj_ref[rows, lay.gv:lay.gv + gw])
        gvn = gv * lax.rsqrt(jnp.mean(gv * gv, axis=-1, keepdims=True) + EPS) * gnorm_ref[...]
        for q_i in range(gw // LANES):
            gl = slice(q_i * LANES, (q_i + 1) * LANES)
            gu = jax.nn.gelu(proj_ref[rows, lay.gu + q_i * LANES:lay.gu + (q_i + 1) * LANES])
            gq = gvn[:, gl]
            vst = jnp.concatenate([jnp.where(lo_half, gq, 0.0).astype(BF16),
                                   jnp.where(lo_half, 0.0, gq).astype(BF16)], axis=0)
            mixed = _dot(wcat_ref[q_i], vst) + gbmat_ref[:, gl]
            o_ref[rows, mw + rw + q_i * LANES:mw + rw + (q_i + 1) * LANES] = (
                gu * mixed).astype(o_ref.dtype)
        return carry

    lax.fori_loop(0, ts // CHUNK, chunk, 0)


def _mixer(proj, tabs, lw, layer, lay, batch, seq):
    t = proj.shape[0]
    ts = MIX_TILE
    nj = seq // ts
    d_mix = lay.mw + lay.rw + lay.gw
    row = lambda b, j: (b * nj + j, 0)
    pos = lambda b, j: (j, 0)

    def layer_slab(a):
        return pl.BlockSpec((None,) + a.shape[1:], lambda b, j: (layer,) + (0,) * (a.ndim - 1))

    def whole(a):
        return pl.BlockSpec(a.shape, lambda b, j: (0,) * a.ndim)

    lws = [lw[k] for k in ("convw", "convb", "bdq", "bdk", "bdvt", "gbi", "gbf", "mnormt", "mskip",
                           "rnorm", "gnorm", "wcat", "gbmat")]
    tbs = [tabs[k] for k in ("decay", "xi", "zeta", "cd", "bmask")]
    est = (2 * ts * lay.n * 4 + 4 * ts * lay.rw * 4 + 2 * ts * d_mix * 2
           + (ts + 8) * lay.mw * 4 + ts * lay.mpad * 8 + ts * lay.mw * 4
           + 2 * sum(int(np.prod(a.shape[1:])) * a.dtype.itemsize for a in lws)
           + 2 * sum(int(np.prod(a.shape)) * a.dtype.itemsize for a in tbs)
           + (M_HEADS + lay.rw // LANES) * LANES * LANES * 4)
    return pl.pallas_call(
        functools.partial(_mixer_kernel, lay=lay),
        grid=(batch, nj),
        in_specs=[pl.BlockSpec((ts, lay.n), row),
                  pl.BlockSpec((ts, lay.rw), pos),
                  pl.BlockSpec((ts, lay.rw), pos)] + [layer_slab(a) for a in lws] + [whole(a) for a in tbs],
        out_specs=pl.BlockSpec((ts, d_mix), row),
        out_shape=jax.ShapeDtypeStruct((t, d_mix), BF16),
        scratch_shapes=[pltpu.VMEM((ts + 8, lay.mw), F32),
                        pltpu.VMEM((ts, lay.mpad), BF16),
                        pltpu.VMEM((ts, lay.mpad), BF16),
                        pltpu.VMEM((ts // CHUNK, lay.mpad, CHUNK), F32),
                        pltpu.VMEM((ts, lay.mw), F32),
                        pltpu.VMEM((M_HEADS, LANES, LANES), F32),
                        pltpu.VMEM((lay.rw // LANES, LANES, LANES), F32),
                        pltpu.VMEM((8, LANES), F32)],
        compiler_params=pltpu.CompilerParams(
            dimension_semantics=("arbitrary", "arbitrary"), vmem_limit_bytes=_vmem_limit(est)),
        name="mixer",
    )(proj, tabs["cos"], tabs["sin"], *lws, *tbs)


def _tables(lay, seq):
    rd, half = lay.rd, lay.rd // 2
    l = CHUNK
    pos = jnp.arange(seq, dtype=F32)
    freqs = ROPE_THETA ** (-jnp.arange(half, dtype=F32) / half)
    ang = pos[:, None] * freqs[None, :]
    lane = np.arange(lay.rw) % LANES
    sign = jnp.asarray(np.where(lane < LANES // 2, -1.0, 1.0), F32)
    cos = jnp.tile(jnp.cos(ang), (1, lay.rw // half))
    sin = jnp.tile(jnp.sin(ang), (1, lay.rw // half)) * sign[None, :]
    log_gamma = jnp.log(1.0 - 2.0 ** (-5.0 - jnp.arange(R_HEADS, dtype=F32)))
    p = jnp.arange(l, dtype=F32)
    causal = jnp.tril(jnp.ones((l, l), dtype=bool))
    rel = jnp.where(causal, p[:, None] - p[None, :], 0.0)
    decay = jnp.where(causal, jnp.exp(rel[None] * log_gamma[:, None, None]), 0.0)
    zeta = jnp.exp((l - 1.0 - p)[None, :] * log_gamma[:, None])
    xi = jnp.exp((p + 1.0)[None, :] * log_gamma[:, None])
    cdec = jnp.exp(l * log_gamma)
    npair = R_HEADS // 2
    decay_ab = decay.reshape(npair, 2, l, l).transpose(0, 2, 1, 3).reshape(npair, l, 2 * l)
    by_v = lambda a: jnp.repeat(a.reshape(npair, 2, -1), rd, axis=1)
    by_k = lambda a: jnp.tile(jnp.repeat(a.reshape(npair, 2, -1), half, axis=1), (1, 2, 1))
    xi_ab = by_v(xi).transpose(0, 2, 1)
    zeta_k = by_k(zeta).transpose(0, 2, 1)
    cd = jnp.broadcast_to(by_k(cdec), (npair, LANES, LANES))
    lane128 = np.arange(LANES)
    head_v = lane128 // rd
    head_k = (lane128 % (2 * half)) // half
    bmask = jnp.asarray((head_k[:, None] == head_v[None, :]).astype(np.float32))
    return {"cos": cos, "sin": sin, "decay": decay_ab, "xi": xi_ab, "zeta": zeta_k, "cd": cd,
            "bmask": bmask}


def _block_diag_padded(w, lay):
    depth, nb = w.shape[0], w.shape[1]
    rows = w.reshape(depth, nb * M_BLOCK, M_BLOCK)
    tiled = jnp.tile(rows, (1, 1, lay.mpad // M_BLOCK))
    r = np.arange(nb * M_BLOCK)[:, None]
    c = np.arange(lay.mpad)[None, :]
    packed_c = (c // LANES) * lay.md + c % LANES
    keep = (c % LANES < lay.md) & (r // M_BLOCK == packed_c // M_BLOCK)
    return jnp.where(jnp.asarray(keep), tiled, 0.0).astype(BF16)


def _mixer_weights(lay, p):
    depth = p["m_wq"].shape[0]
    gbi = jnp.concatenate([p["m_i_bias"], jnp.zeros((depth, LANES - M_HEADS), F32)], axis=-1)
    gbf = jnp.concatenate(
        [p["m_f_bias"], p["m_f_bias"], jnp.zeros((depth, LANES - 2 * M_HEADS), F32)], axis=-1)
    causal = jnp.tril(jnp.ones((CHUNK, CHUNK), dtype=bool))
    wm = jnp.where(causal, p["g_ws"], 0.0)
    ng = G_GROUPS // 2
    wcat = (wm.reshape(depth, ng, 2, CHUNK, CHUNK).transpose(0, 1, 3, 2, 4)
            .reshape(depth, ng, CHUNK, 2 * CHUNK))
    gbmat = jnp.repeat(p["g_bs"].transpose(0, 2, 1), lay.gw // G_GROUPS, axis=2)
    row = lambda a: a[:, None, :]
    return {
        "convw": p["m_conv_w"], "convb": row(p["m_conv_b"]),
        "bdq": _block_diag_padded(p["m_wq"], lay),
        "bdk": _block_diag_padded(p["m_wk"], lay),
        "bdvt": _block_diag_padded(p["m_wv"], lay).transpose(0, 2, 1),
        "gbi": row(gbi), "gbf": row(gbf),
        "mnormt": jnp.broadcast_to(p["m_norm_g"][:, :, None], (depth, lay.mw, LANES)),
        "mskip": row(p["m_skip"]),
        "rnorm": row(p["r_norm_g"]),
        "gnorm": row(p["g_norm_g"]),
        "wcat": wcat.astype(BF16), "gbmat": gbmat,
    }


def kernel(x, norm_mix_g, w_in, m_conv_w, m_conv_b, m_wq, m_wk, m_wv, m_i_bias, m_f_bias, m_norm_g,
           m_skip, r_norm_g, g_norm_g, g_ws, g_bs, w_out, norm_ff_g, w_ff1, w_ff2, final_norm_g):
    batch, seq, d = x.shape
    depth = w_in.shape[0]
    lay = _Layout(d)
    assert seq % MIX_TILE == 0 and (batch * seq) % TOKEN_TILE == 0 and MIX_TILE % CHUNK == 0
    assert w_ff1.shape[2] % FF_CHUNK == 0
    params = dict(m_conv_w=m_conv_w, m_conv_b=m_conv_b, m_wq=m_wq, m_wk=m_wk, m_wv=m_wv,
                  m_i_bias=m_i_bias, m_f_bias=m_f_bias, m_norm_g=m_norm_g, m_skip=m_skip,
                  r_norm_g=r_norm_g, g_norm_g=g_norm_g, g_ws=g_ws, g_bs=g_bs)
    tabs = _tables(lay, seq)
    lw = _mixer_weights(lay, params)
    w_in_p = lay.relayout_in_proj(w_in)
    w_out_b = w_out.astype(BF16)
    w1_b = w_ff1.astype(BF16)
    w2_b = w_ff2.astype(BF16)
    g_mix = norm_mix_g[:, None, :]
    g_ff = norm_ff_g[:, None, :]

    x2 = x.reshape(batch * seq, d)
    for layer in range(depth):
        proj = _inproj(x2, g_mix, w_in_p, layer)
        mix = _mixer(proj, tabs, lw, layer, lay, batch, seq)
        x2 = _ffn(mix, x2, w_out_b, g_ff, w1_b, w2_b, final_norm_g[None, :], layer,
                  final=(layer == depth - 1))
    return x2.reshape(batch, seq, d)
```

```python
import functools

import numpy as np
import jax
import jax.numpy as jnp
from jax import lax
from jax.experimental import pallas as pl
from jax.experimental.pallas import tpu as pltpu

M_HEADS = 4
M_CONV = 4
M_BLOCK = 4
R_HEADS = 6
G_GROUPS = 4
CHUNK = 128
ROPE_THETA = 10000.0
EPS = 1e-6

LANES = 128
V7X_VMEM_BYTES = 64 * 1024 * 1024

TOKEN_TILE = 512
N_CHUNKS = TOKEN_TILE // CHUNK

F32 = jnp.float32
BF16 = jnp.bfloat16


def _rms(x, g):
    return x * lax.rsqrt(jnp.mean(jnp.square(x), axis=-1, keepdims=True) + EPS) * g


def _dot(a, b):
    return jnp.dot(a, b, preferred_element_type=F32)


def _dot_nt(a, b):
    return lax.dot_general(a, b, (((1,), (1,)), ((), ())), preferred_element_type=F32)


def _dot_tn(a, b):
    return lax.dot_general(a, b, (((0,), (0,)), ((), ())), preferred_element_type=F32)


class _Layout:
    def __init__(self, d_model):
        self.mw = 3 * d_model // 8
        self.rw = 3 * d_model // 8
        self.gw = d_model - self.mw - self.rw
        self.md = self.mw // M_HEADS
        self.rd = self.rw // R_HEADS
        assert self.md <= LANES and 2 * self.rd == LANES and self.gw // G_GROUPS * 2 == LANES
        assert self.mw % LANES == 0 and self.rw % LANES == 0 and self.gw % LANES == 0
        o = 0
        self.mx = o; o += self.mw
        self.mz = o; o += self.mw
        self.rq = o; o += self.rw
        self.rk = o; o += self.rw
        self.rv = o; o += self.rw
        self.rg = o; o += self.rw
        self.gu = o; o += self.gw
        self.gv = o; o += self.gw
        self.gi = o; o += LANES
        self.gf = o; o += LANES
        self.n = o
        self.mpad = M_HEADS * LANES
        self.dmix = self.mw + self.rw + self.gw

    def relayout_in_proj(self, w_in):
        mw, rw, gw, rd = self.mw, self.rw, self.gw, self.rd
        depth, d, _ = w_in.shape
        o_gate = 2 * mw
        o_rq = o_gate + 2 * M_HEADS
        o_rv = o_rq + 2 * rw

        def rotary_layout(seg):
            s = seg.reshape(depth, d, R_HEADS // 2, 2, 2, rd // 2)
            return s.transpose(0, 1, 2, 4, 3, 5).reshape(depth, d, rw)

        parts = [w_in[:, :, 0:2 * mw],
                 rotary_layout(w_in[:, :, o_rq:o_rq + rw]),
                 rotary_layout(w_in[:, :, o_rq + rw:o_rq + 2 * rw]),
                 w_in[:, :, o_rv:o_rv + 2 * rw + 2 * gw],
                 w_in[:, :, o_gate:o_gate + M_HEADS],
                 jnp.zeros((depth, d, LANES - M_HEADS), w_in.dtype),
                 w_in[:, :, o_gate + M_HEADS:o_gate + 2 * M_HEADS],
                 w_in[:, :, o_gate + M_HEADS:o_gate + 2 * M_HEADS],
                 jnp.zeros((depth, d, LANES - 2 * M_HEADS), w_in.dtype)]
        return jnp.concatenate(parts, axis=-1).astype(BF16)


_MIXER_PARAMS = ("convw", "convb", "bdq", "bdk", "bdvt", "gbi", "gbf", "mnormt", "mskip", "rnorm", "gnorm",
                 "wcat", "gbmat")
_TABLES = ("decay", "xi", "zeta", "cd", "bmask")


def _mixer_tile(proj_s, mp, xbuf, q_s, k_s, vt_s, xc_s, lay):
    ts = proj_s.shape[0]
    md, mw = lay.md, lay.mw
    mx = proj_s[:, lay.mx:lay.mx + mw]
    xbuf[8:ts + 8, :] = mx
    conv = mp["convb"][...] + mp["convw"][M_CONV - 1:M_CONV, :] * mx
    for t in range(M_CONV - 1):
        conv = conv + mp["convw"][t:t + 1, :] * xbuf[8 - (M_CONV - 1) + t:8 - (M_CONV - 1) + t + ts, :]
    xc = conv * jax.nn.sigmoid(conv)
    xc_s[...] = xc
    xcb = xc.astype(BF16)
    q_s[...] = _dot(xcb, mp["bdq"][...]).astype(BF16)
    k_s[...] = (_dot(xcb, mp["bdk"][...]) * (md ** -0.5)).astype(BF16)
    row_p = lax.broadcasted_iota(jnp.int32, (lay.mpad, 1), 0) % LANES
    vt = _dot_nt(mp["bdvt"][...], mx.astype(BF16)) + (row_p == md).astype(F32)
    for c in range(ts // CHUNK):
        vt_s[c] = vt[:, c * CHUNK:(c + 1) * CHUNK]


def _mixer_chunk(c, proj_s, cos_ref, sin_ref, mp, tb, mix_s, q_s, k_s, vt_s, xc_s, c_st, r_st, m_st, lay):
    md, mw, rw, gw = lay.md, lay.mw, lay.rw, lay.gw
    li = lax.broadcasted_iota(jnp.int32, (CHUNK, LANES), 1)
    ri = lax.broadcasted_iota(jnp.int32, (CHUNK, LANES), 0)
    causal_t = ri <= li
    tril = (li <= ri).astype(BF16)
    lo_half = li < (LANES // 2)
    rot_a = (li % (LANES // 2)) < (LANES // 4)

    r0 = pl.multiple_of(c * CHUNK, CHUNK)
    rows = pl.ds(r0, CHUNK)

    gi = proj_s[rows, lay.gi:lay.gi + LANES] + mp["gbi"][...]
    gf = proj_s[rows, lay.gf:lay.gf + LANES] + mp["gbf"][...]
    logf = jnp.where(li < 2 * M_HEADS, jax.nn.log_sigmoid(gf), 0.0)
    a1 = logf.astype(BF16)
    r1 = logf - a1.astype(F32)
    a2 = r1.astype(BF16)
    a3 = (r1 - a2.astype(F32)).astype(BF16)
    bcum = (_dot(tril, a3) + _dot(tril, a2)) + _dot(tril, a1)
    wb = jnp.where(li < M_HEADS, gi - bcum, bcum)
    wb_t = wb.T

    hn_t = []
    for h in range(M_HEADS):
        hl = slice(h * LANES, (h + 1) * LANES)
        qh = q_s[rows, hl]
        kh = k_s[rows, hl]
        vth = vt_s[c, hl, :]
        w_row = wb_t[h:h + 1, :]
        b_row = wb_t[M_HEADS + h:M_HEADS + h + 1, :]
        b_end = b_row[:, CHUNK - 1:CHUNK]
        w_end = b_end + w_row
        a_c = jnp.max(w_end, axis=-1, keepdims=True)
        e_row = jnp.exp(w_end - a_c)
        m_prev = m_st[h:h + 1, :]
        m_new = jnp.maximum(b_end + m_prev, a_c)
        s_old = jnp.exp(b_end + m_prev - m_new)
        s_new = jnp.exp(a_c - m_new)
        m_st[h:h + 1, :] = m_new

        il = b_row + m_prev
        dm = jnp.where(causal_t, wb[:, h:h + 1] + b_row, -jnp.inf)
        m_t = jnp.maximum(il, jnp.max(dm, axis=0, keepdims=True))
        s_inter = jnp.exp(il - m_t)
        c_prev = c_st[h]
        kq = _dot_nt(jnp.concatenate([kh, c_prev.astype(BF16)], axis=0), qh)
        p_t = kq[0:CHUNK] * jnp.exp(dm - m_t)
        inter = s_inter * kq[CHUNK:2 * CHUNK]
        tot = _dot(vth.astype(BF16), p_t.astype(BF16)) + inter
        den = jnp.sum(p_t, axis=0, keepdims=True) + inter[md:md + 1, :]
        rden = 1.0 / jnp.maximum(jnp.abs(den), jnp.exp(-m_t))
        hout = tot[0:md, :] * rden
        ms = jnp.sum(hout * hout, axis=0, keepdims=True) * (1.0 / md)
        hn_t.append(hout * lax.rsqrt(ms + EPS) * mp["mnormt"][h * md:(h + 1) * md, :])
        c_st[h] = s_old * c_prev + s_new * _dot((vth * e_row).astype(BF16), kh)

    hm_t = jnp.concatenate(hn_t, axis=0)
    hm = jnp.concatenate([hm_t[t * LANES:(t + 1) * LANES, :].T for t in range(mw // LANES)], axis=1)
    mz = proj_s[rows, lay.mz:lay.mz + mw]
    out_m = (hm + mp["mskip"][...] * xc_s[rows, :]) * (mz * jax.nn.sigmoid(mz))
    mix_s[rows, 0:mw] = out_m.astype(mix_s.dtype)

    cs = cos_ref[rows, :]
    sn = sin_ref[rows, :]
    for p_i in range(rw // LANES):
        pl_ = slice(p_i * LANES, (p_i + 1) * LANES)
        rq = proj_s[rows, lay.rq + p_i * LANES:lay.rq + (p_i + 1) * LANES]
        rk = proj_s[rows, lay.rk + p_i * LANES:lay.rk + (p_i + 1) * LANES]
        rv = proj_s[rows, lay.rv + p_i * LANES:lay.rv + (p_i + 1) * LANES]
        rg = proj_s[rows, lay.rg + p_i * LANES:lay.rg + (p_i + 1) * LANES]
        qb = (rq * cs + pltpu.roll(rq, LANES // 2, axis=1) * sn).astype(BF16)
        kr = (rk * cs + pltpu.roll(rk, LANES // 2, axis=1) * sn) * (lay.rd ** -0.5)
        kcat = jnp.concatenate([jnp.where(rot_a, kr, 0.0).astype(BF16),
                                jnp.where(rot_a, 0.0, kr).astype(BF16)], axis=0)
        s_ab = _dot_nt(qb, kcat)
        p_ab = (s_ab * tb["decay"][p_i]).astype(BF16)
        vst = jnp.concatenate([jnp.where(lo_half, rv, 0.0).astype(BF16),
                               jnp.where(lo_half, 0.0, rv).astype(BF16)], axis=0)
        r_prev = r_st[p_i]
        hr = _dot(p_ab, vst) + _dot(qb, r_prev.astype(BF16)) * tb["xi"][p_i]
        sq = hr * hr
        ms_a = jnp.sum(jnp.where(lo_half, sq, 0.0), axis=-1, keepdims=True)
        ms_b = jnp.sum(jnp.where(lo_half, 0.0, sq), axis=-1, keepdims=True)
        ms = jnp.where(lo_half, ms_a, ms_b) * (1.0 / lay.rd)
        hn = hr * lax.rsqrt(ms + EPS) * mp["rnorm"][:, pl_]
        mix_s[rows, mw + p_i * LANES:mw + (p_i + 1) * LANES] = (
            hn * (rg * jax.nn.sigmoid(rg))).astype(mix_s.dtype)
        kv = _dot_tn((kr * tb["zeta"][p_i]).astype(BF16), rv.astype(BF16))
        r_st[p_i] = tb["cd"][p_i] * r_prev + tb["bmask"][...] * kv

    gv = jax.nn.gelu(proj_s[rows, lay.gv:lay.gv + gw])
    gvn = gv * lax.rsqrt(jnp.mean(gv * gv, axis=-1, keepdims=True) + EPS) * mp["gnorm"][...]
    for q_i in range(gw // LANES):
        gl = slice(q_i * LANES, (q_i + 1) * LANES)
        gu = jax.nn.gelu(proj_s[rows, lay.gu + q_i * LANES:lay.gu + (q_i + 1) * LANES])
        gq = gvn[:, gl]
        vst = jnp.concatenate([jnp.where(lo_half, gq, 0.0).astype(BF16),
                               jnp.where(lo_half, 0.0, gq).astype(BF16)], axis=0)
        mixed = _dot(mp["wcat"][q_i], vst) + mp["gbmat"][:, gl]
        mix_s[rows, mw + rw + q_i * LANES:mw + rw + (q_i + 1) * LANES] = (gu * mixed).astype(mix_s.dtype)


def _layer_kernel(*refs, lay, nj, final):
    n_mp, n_tb = len(_MIXER_PARAMS), len(_TABLES)
    x_ref, cos_ref, sin_ref, gmix_ref, win_ref = refs[0:5]
    mp = dict(zip(_MIXER_PARAMS, refs[5:5 + n_mp]))
    tb = dict(zip(_TABLES, refs[5 + n_mp:5 + n_mp + n_tb]))
    wo_ref, gff_ref, w1_ref, w2_ref, gfin_ref = refs[5 + n_mp + n_tb:10 + n_mp + n_tb]
    o_ref = refs[10 + n_mp + n_tb]
    (proj_s, mix_s, h2_s, xprev_s, xbuf, q_s, k_s, vt_s, xc_s, c_st, r_st, m_st) = refs[11 + n_mp + n_tb:]
    ts = x_ref.shape[0]
    g = pl.program_id(0)
    first_of_seq = lax.rem(g, nj) == 0

    @pl.when(g == 0)
    def _():
        mix_s[...] = jnp.zeros_like(mix_s)
        xprev_s[...] = jnp.zeros_like(xprev_s)

    @pl.when(first_of_seq)
    def _():
        c_st[...] = jnp.zeros_like(c_st)
        r_st[...] = jnp.zeros_like(r_st)
        m_st[...] = jnp.zeros_like(m_st)
        xbuf[0:8, :] = jnp.zeros((8, lay.mw), F32)

    @pl.when(jnp.logical_not(first_of_seq))
    def _():
        xbuf[0:8, :] = xbuf[ts:ts + 8, :]

    x1 = xprev_s[...] + _dot(mix_s[...], wo_ref[...])
    o_ref[...] = x1
    h2_s[...] = _rms(x1, gff_ref[...]).astype(BF16)

    x = x_ref[...]
    proj_s[...] = _dot(_rms(x, gmix_ref[...]).astype(BF16), win_ref[...])
    xprev_s[...] = x
    _mixer_tile(proj_s, mp, xbuf, q_s, k_s, vt_s, xc_s, lay)

    def step(c, carry):
        _mixer_chunk(c, proj_s, cos_ref, sin_ref, mp, tb, mix_s, q_s, k_s, vt_s, xc_s, c_st, r_st, m_st, lay)
        a = jnp.square(jnp.maximum(_dot(h2_s[...], w1_ref[c]), 0.0)).astype(BF16)
        o_ref[...] += _dot(a, w2_ref[c])
        return carry

    lax.fori_loop(0, ts // CHUNK, step, 0)
    if final:
        o_ref[...] = _rms(o_ref[...], gfin_ref[...])


def _layer(x2, tabs, lw, dense, layer, lay, seq, final):
    t, d = x2.shape
    ts = TOKEN_TILE
    nj = seq // ts
    n_tiles = t // ts
    cur = lambda g: (jnp.minimum(g, n_tiles - 1), 0)
    prev = lambda g: (jnp.maximum(g - 1, 0), 0)
    pos = lambda g: (lax.rem(jnp.minimum(g, n_tiles - 1), nj), 0)

    def layer_slab(a, **kw):
        return pl.BlockSpec((None,) + a.shape[1:], lambda g: (layer,) + (0,) * (a.ndim - 1), **kw)

    def whole(a):
        return pl.BlockSpec(a.shape, lambda g: (0,) * a.ndim)

    once = dict(pipeline_mode=pl.Buffered(1))
    mps = [lw[k] for k in _MIXER_PARAMS]
    tbs = [tabs[k] for k in _TABLES]
    big = [dense["w_in"], dense["w_out"], dense["w1"], dense["w2"]]
    scratch = [pltpu.VMEM((ts, lay.n), F32),
               pltpu.VMEM((ts, lay.dmix), BF16),
               pltpu.VMEM((ts, d), BF16),
               pltpu.VMEM((ts, d), F32),
               pltpu.VMEM((ts + 8, lay.mw), F32),
               pltpu.VMEM((ts, lay.mpad), BF16),
               pltpu.VMEM((ts, lay.mpad), BF16),
               pltpu.VMEM((ts // CHUNK, lay.mpad, CHUNK), F32),
               pltpu.VMEM((ts, lay.mw), F32),
               pltpu.VMEM((M_HEADS, LANES, LANES), F32),
               pltpu.VMEM((lay.rw // LANES, LANES, LANES), F32),
               pltpu.VMEM((8, LANES), F32)]
    nbytes = lambda shape, dt: int(np.prod(shape)) * jnp.dtype(dt).itemsize
    est = (sum(nbytes(a.shape[1:], a.dtype) for a in big)
           + 2 * sum(nbytes(a.shape[1:], a.dtype) for a in mps) + 2 * sum(nbytes(a.shape, a.dtype) for a in tbs)
           + 4 * ts * d * 4 + 4 * ts * LANES * 4
           + sum(nbytes(s.shape, s.dtype) for s in scratch))
    return pl.pallas_call(
        functools.partial(_layer_kernel, lay=lay, nj=nj, final=final),
        grid=(n_tiles + 1,),
        in_specs=([pl.BlockSpec((ts, d), cur),
                   pl.BlockSpec((ts, LANES), pos),
                   pl.BlockSpec((ts, LANES), pos),
                   layer_slab(dense["g_mix"]),
                   layer_slab(dense["w_in"], **once)]
                  + [layer_slab(a) for a in mps] + [whole(a) for a in tbs]
                  + [layer_slab(dense["w_out"], **once),
                     layer_slab(dense["g_ff"]),
                     layer_slab(dense["w1"], **once),
                     layer_slab(dense["w2"], **once),
                     whole(dense["g_final"])]),
        out_specs=pl.BlockSpec((ts, d), prev),
        out_shape=jax.ShapeDtypeStruct((t, d), F32),
        scratch_shapes=scratch,
        compiler_params=pltpu.CompilerParams(
            dimension_semantics=("arbitrary",),
            vmem_limit_bytes=int(min(est + (8 << 20), V7X_VMEM_BYTES - (2 << 20)))),
        name="layer_final" if final else "layer",
    )(x2, tabs["cos"], tabs["sin"], dense["g_mix"], dense["w_in"], *mps, *tbs,
      dense["w_out"], dense["g_ff"], dense["w1"], dense["w2"], dense["g_final"])


def _tables(lay, seq):
    rd, half = lay.rd, lay.rd // 2
    l = CHUNK
    pos = jnp.arange(seq, dtype=F32)
    freqs = ROPE_THETA ** (-jnp.arange(half, dtype=F32) / half)
    ang = pos[:, None] * freqs[None, :]
    sign = jnp.asarray(np.where(np.arange(LANES) < LANES // 2, -1.0, 1.0), F32)
    cos = jnp.tile(jnp.cos(ang), (1, LANES // half))
    sin = jnp.tile(jnp.sin(ang), (1, LANES // half)) * sign[None, :]
    log_gamma = jnp.log(1.0 - 2.0 ** (-5.0 - jnp.arange(R_HEADS, dtype=F32)))
    p = jnp.arange(l, dtype=F32)
    causal = jnp.tril(jnp.ones((l, l), dtype=bool))
    rel = jnp.where(causal, p[:, None] - p[None, :], 0.0)
    decay = jnp.where(causal, jnp.exp(rel[None] * log_gamma[:, None, None]), 0.0)
    zeta = jnp.exp((l - 1.0 - p)[None, :] * log_gamma[:, None])
    xi = jnp.exp((p + 1.0)[None, :] * log_gamma[:, None])
    cdec = jnp.exp(l * log_gamma)
    npair = R_HEADS // 2
    decay_ab = decay.reshape(npair, 2, l, l).transpose(0, 2, 1, 3).reshape(npair, l, 2 * l)
    by_v = lambda a: jnp.repeat(a.reshape(npair, 2, -1), rd, axis=1)
    by_k = lambda a: jnp.tile(jnp.repeat(a.reshape(npair, 2, -1), half, axis=1), (1, 2, 1))
    xi_ab = by_v(xi).transpose(0, 2, 1)
    zeta_k = by_k(zeta).transpose(0, 2, 1)
    cd = jnp.broadcast_to(by_k(cdec), (npair, LANES, LANES))
    lane128 = np.arange(LANES)
    head_v = lane128 // rd
    head_k = (lane128 % (2 * half)) // half
    bmask = jnp.asarray((head_k[:, None] == head_v[None, :]).astype(np.float32))
    return {"cos": cos, "sin": sin, "decay": decay_ab, "xi": xi_ab, "zeta": zeta_k, "cd": cd,
            "bmask": bmask}


def _block_diag_padded(w, lay):
    depth, nb = w.shape[0], w.shape[1]
    rows = w.reshape(depth, nb * M_BLOCK, M_BLOCK)
    tiled = jnp.tile(rows, (1, 1, lay.mpad // M_BLOCK))
    r = np.arange(nb * M_BLOCK)[:, None]
    c = np.arange(lay.mpad)[None, :]
    packed_c = (c // LANES) * lay.md + c % LANES
    keep = (c % LANES < lay.md) & (r // M_BLOCK == packed_c // M_BLOCK)
    return jnp.where(jnp.asarray(keep), tiled, 0.0).astype(BF16)


def _mixer_weights(lay, p):
    depth = p["m_wq"].shape[0]
    gbi = jnp.concatenate([p["m_i_bias"], jnp.zeros((depth, LANES - M_HEADS), F32)], axis=-1)
    gbf = jnp.concatenate(
        [p["m_f_bias"], p["m_f_bias"], jnp.zeros((depth, LANES - 2 * M_HEADS), F32)], axis=-1)
    causal = jnp.tril(jnp.ones((CHUNK, CHUNK), dtype=bool))
    wm = jnp.where(causal, p["g_ws"], 0.0)
    ng = G_GROUPS // 2
    wcat = (wm.reshape(depth, ng, 2, CHUNK, CHUNK).transpose(0, 1, 3, 2, 4)
            .reshape(depth, ng, CHUNK, 2 * CHUNK))
    gbmat = jnp.repeat(p["g_bs"].transpose(0, 2, 1), lay.gw // G_GROUPS, axis=2)
    row = lambda a: a[:, None, :]
    return {
        "convw": p["m_conv_w"], "convb": row(p["m_conv_b"]),
        "bdq": _block_diag_padded(p["m_wq"], lay),
        "bdk": _block_diag_padded(p["m_wk"], lay),
        "bdvt": _block_diag_padded(p["m_wv"], lay).transpose(0, 2, 1),
        "gbi": row(gbi), "gbf": row(gbf),
        "mnormt": jnp.broadcast_to(p["m_norm_g"][:, :, None], (depth, lay.mw, LANES)),
        "mskip": row(p["m_skip"]),
        "rnorm": row(p["r_norm_g"]),
        "gnorm": row(p["g_norm_g"]),
        "wcat": wcat.astype(BF16), "gbmat": gbmat,
    }


def kernel(x, norm_mix_g, w_in, m_conv_w, m_conv_b, m_wq, m_wk, m_wv, m_i_bias, m_f_bias, m_norm_g,
           m_skip, r_norm_g, g_norm_g, g_ws, g_bs, w_out, norm_ff_g, w_ff1, w_ff2, final_norm_g):
    batch, seq, d = x.shape
    depth = w_in.shape[0]
    dff = w_ff1.shape[2]
    lay = _Layout(d)
    assert seq % TOKEN_TILE == 0 and dff % N_CHUNKS == 0
    params = dict(m_conv_w=m_conv_w, m_conv_b=m_conv_b, m_wq=m_wq, m_wk=m_wk, m_wv=m_wv,
                  m_i_bias=m_i_bias, m_f_bias=m_f_bias, m_norm_g=m_norm_g, m_skip=m_skip,
                  r_norm_g=r_norm_g, g_norm_g=g_norm_g, g_ws=g_ws, g_bs=g_bs)
    tabs = _tables(lay, seq)
    lw = _mixer_weights(lay, params)
    slab = dff // N_CHUNKS
    dense = {
        "w_in": lay.relayout_in_proj(w_in),
        "w_out": w_out.astype(BF16),
        "w1": w_ff1.reshape(depth, d, N_CHUNKS, slab).transpose(0, 2, 1, 3).astype(BF16),
        "w2": w_ff2.reshape(depth, N_CHUNKS, slab, d).astype(BF16),
        "g_mix": norm_mix_g[:, None, :], "g_ff": norm_ff_g[:, None, :], "g_final": final_norm_g[None, :],
    }
    x2 = x.reshape(batch * seq, d)
    for layer in range(depth):
        x2 = _layer(x2, tabs, lw, dense, layer, lay, seq, final=(layer == depth - 1))
    return x2.reshape(batch, seq, d)
```

```python
import functools

import numpy as np
import jax
import jax.numpy as jnp
from jax import lax
from jax.experimental import pallas as pl
from jax.experimental.pallas import tpu as pltpu

M_HEADS = 4
M_CONV = 4
M_BLOCK = 4
R_HEADS = 6
G_GROUPS = 4
CHUNK = 128
ROPE_THETA = 10000.0
EPS = 1e-6

LANES = 128
V7X_VMEM_BYTES = 64 * 1024 * 1024

TOKEN_TILE = 512
MIX_TILE = 512
MIX_ROWS = 2
FF_CHUNK = 1024

F32 = jnp.float32
BF16 = jnp.bfloat16


def _rms(x, g):
    return x * lax.rsqrt(jnp.mean(jnp.square(x), axis=-1, keepdims=True) + EPS) * g


def _silu(x):
    return x * jax.nn.sigmoid(x)


def _dot(a, b):
    return jnp.dot(a, b, preferred_element_type=F32)


def _dot_nt(a, b):
    return lax.dot_general(a, b, (((1,), (1,)), ((), ())), preferred_element_type=F32)


def _dot_tn(a, b):
    return lax.dot_general(a, b, (((0,), (0,)), ((), ())), preferred_element_type=F32)


def _nbytes(shape, dtype):
    return int(np.prod(shape)) * jnp.dtype(dtype).itemsize


def _vmem_limit(nbytes):
    return int(min(nbytes + (12 << 20), V7X_VMEM_BYTES - (4 << 20)))


def _layer_slab(a, layer, **kw):
    return pl.BlockSpec((None,) + a.shape[1:], lambda *g: (layer,) + (0,) * (a.ndim - 1), **kw)


def _whole(a):
    return pl.BlockSpec(a.shape, lambda *g: (0,) * a.ndim)


_ONCE = dict(pipeline_mode=pl.Buffered(1))


class _Layout:
    def __init__(self, d_model):
        self.mw = 3 * d_model // 8
        self.rw = 3 * d_model // 8
        self.gw = d_model - self.mw - self.rw
        self.md = self.mw // M_HEADS
        self.rd = self.rw // R_HEADS
        assert self.md <= LANES and 2 * self.rd == LANES and self.gw // G_GROUPS * 2 == LANES
        assert self.mw % LANES == 0 and self.rw % LANES == 0 and self.gw % LANES == 0
        self.mpad = M_HEADS * LANES
        self.dmix = self.mw + self.rw + self.gw
        o = 0
        self.mx = o; o += self.mw
        self.mz = o; o += self.mw
        self.rq = o; o += self.rw
        self.rk = o; o += self.rw
        self.rv = o; o += self.rw
        self.rg = o; o += self.rw
        self.gu = o; o += self.gw
        self.gv = o; o += self.gw
        self.gi = o; o += LANES
        self.gf = o; o += LANES
        self.n = o
        o = 0
        self.a_q = o; o += self.mpad
        self.a_k = o; o += self.mpad
        self.a_rq = o; o += self.rw
        self.a_rka = o; o += self.rw
        self.a_rkb = o; o += self.rw
        self.a_kz = o; o += self.rw
        self.a_rva = o; o += self.rw
        self.a_rvb = o; o += self.rw
        self.a_gva = o; o += self.gw
        self.a_gvb = o; o += self.gw
        self.a_zs = o; o += self.mw
        self.a_sxz = o; o += self.mw
        self.a_rgs = o; o += self.rw
        self.a_gug = o; o += self.gw
        self.na = o

    def relayout_in_proj(self, w_in):
        mw, rw, gw, rd = self.mw, self.rw, self.gw, self.rd
        depth, d, _ = w_in.shape
        o_gate = 2 * mw
        o_rq = o_gate + 2 * M_HEADS
        o_rv = o_rq + 2 * rw

        def rotary_layout(seg):
            s = seg.reshape(depth, d, R_HEADS // 2, 2, 2, rd // 2)
            return s.transpose(0, 1, 2, 4, 3, 5).reshape(depth, d, rw)

        parts = [w_in[:, :, 0:2 * mw],
                 rotary_layout(w_in[:, :, o_rq:o_rq + rw]),
                 rotary_layout(w_in[:, :, o_rq + rw:o_rq + 2 * rw]),
                 w_in[:, :, o_rv:o_rv + 2 * rw + 2 * gw],
                 w_in[:, :, o_gate:o_gate + M_HEADS],
                 jnp.zeros((depth, d, LANES - M_HEADS), w_in.dtype),
                 w_in[:, :, o_gate + M_HEADS:o_gate + 2 * M_HEADS],
                 w_in[:, :, o_gate + M_HEADS:o_gate + 2 * M_HEADS],
                 jnp.zeros((depth, d, LANES - 2 * M_HEADS), w_in.dtype)]
        return jnp.concatenate(parts, axis=-1).astype(BF16)


_INPROJ_PARAMS = ("convw", "convb", "bdq", "bdk", "bdvt", "gbi", "gbf", "mskip", "gnorm")


def _inproj_kernel(*refs, lay, nj):
    n_p = len(_INPROJ_PARAMS)
    x_ref, g_ref, w_ref, cos_ref, sin_ref, zeta_ref = refs[0:6]
    mp = dict(zip(_INPROJ_PARAMS, refs[6:6 + n_p]))
    act_ref, gates_ref, vt_ref = refs[6 + n_p:9 + n_p]
    xbuf = refs[9 + n_p]
    tm = x_ref.shape[0]
    md, mw, rw, gw = lay.md, lay.mw, lay.rw, lay.gw
    first_of_seq = lax.rem(pl.program_id(0), nj) == 0

    @pl.when(first_of_seq)
    def _():
        xbuf[0:8, :] = jnp.zeros((8, mw), F32)

    @pl.when(jnp.logical_not(first_of_seq))
    def _():
        xbuf[0:8, :] = xbuf[tm:tm + 8, :]

    proj = _dot(_rms(x_ref[...], g_ref[...]).astype(BF16), w_ref[...])

    def put(col, val):
        act_ref[:, col:col + val.shape[1]] = val.astype(act_ref.dtype)

    mx = proj[:, lay.mx:lay.mx + mw]
    xbuf[8:tm + 8, :] = mx
    conv = mp["convb"][...] + mp["convw"][M_CONV - 1:M_CONV, :] * mx
    for t in range(M_CONV - 1):
        conv = conv + mp["convw"][t:t + 1, :] * xbuf[8 - (M_CONV - 1) + t:8 - (M_CONV - 1) + t + tm, :]
    xc = _silu(conv)
    xcb = xc.astype(BF16)
    put(lay.a_q, _dot(xcb, mp["bdq"][...]))
    put(lay.a_k, _dot(xcb, mp["bdk"][...]) * (md ** -0.5))
    row_p = lax.broadcasted_iota(jnp.int32, (lay.mpad, 1), 0) % LANES
    vt = _dot_nt(mp["bdvt"][...], mx.astype(BF16)) + (row_p == md).astype(F32)
    for c in range(tm // CHUNK):
        vt_ref[c] = vt[:, c * CHUNK:(c + 1) * CHUNK]
    zs = _silu(proj[:, lay.mz:lay.mz + mw])
    put(lay.a_zs, zs)
    put(lay.a_sxz, mp["mskip"][...] * xc * zs)

    li = lax.broadcasted_iota(jnp.int32, (tm, LANES), 1)
    gates_ref[:, 0:LANES] = proj[:, lay.gi:lay.gi + LANES] + mp["gbi"][...]
    gf = proj[:, lay.gf:lay.gf + LANES] + mp["gbf"][...]
    gates_ref[:, LANES:2 * LANES] = jnp.where(li < 2 * M_HEADS, jax.nn.log_sigmoid(gf), 0.0)

    lo_half = li < (LANES // 2)
    rot_a = (li % (LANES // 2)) < (LANES // 4)
    cs = cos_ref[...]
    sn = sin_ref[...]
    for p_i in range(rw // LANES):
        off = p_i * LANES
        rq = proj[:, lay.rq + off:lay.rq + off + LANES]
        rk = proj[:, lay.rk + off:lay.rk + off + LANES]
        rv = proj[:, lay.rv + off:lay.rv + off + LANES]
        put(lay.a_rq + off, rq * cs + pltpu.roll(rq, LANES // 2, axis=1) * sn)
        kr = (rk * cs + pltpu.roll(rk, LANES // 2, axis=1) * sn) * (lay.rd ** -0.5)
        put(lay.a_rka + off, jnp.where(rot_a, kr, 0.0))
        put(lay.a_rkb + off, jnp.where(rot_a, 0.0, kr))
        put(lay.a_kz + off, kr * zeta_ref[:, off:off + LANES])
        put(lay.a_rva + off, jnp.where(lo_half, rv, 0.0))
        put(lay.a_rvb + off, jnp.where(lo_half, 0.0, rv))
    put(lay.a_rgs, _silu(proj[:, lay.rg:lay.rg + rw]))

    put(lay.a_gug, jax.nn.gelu(proj[:, lay.gu:lay.gu + gw]))
    gv = jax.nn.gelu(proj[:, lay.gv:lay.gv + gw])
    gvn = gv * lax.rsqrt(jnp.mean(gv * gv, axis=-1, keepdims=True) + EPS) * mp["gnorm"][...]
    for q_i in range(gw // LANES):
        off = q_i * LANES
        put(lay.a_gva + off, jnp.where(lo_half, gvn[:, off:off + LANES], 0.0))
        put(lay.a_gvb + off, jnp.where(lo_half, 0.0, gvn[:, off:off + LANES]))


def _inproj(x2, tabs, lw, dense, layer, lay, seq):
    t, d = x2.shape
    tm = TOKEN_TILE
    nj = seq // tm
    w = dense["w_in"]
    mps = [lw[k] for k in _INPROJ_PARAMS]
    pos = lambda i: (lax.rem(i, nj), 0)
    est = (2 * tm * d * 4 + d * lay.n * 2 + 2 * tm * lay.na * 2 + 2 * tm * 2 * LANES * 4
           + 2 * tm * lay.mpad * 4 + 6 * tm * LANES * 4 + (tm + 8) * lay.mw * 4 + tm * lay.n * 4
           + 2 * sum(_nbytes(a.shape[1:], a.dtype) for a in mps))
    return pl.pallas_call(
        functools.partial(_inproj_kernel, lay=lay, nj=nj),
        grid=(t // tm,),
        in_specs=([pl.BlockSpec((tm, d), lambda i: (i, 0)),
                   _layer_slab(dense["g_mix"], layer),
                   _layer_slab(w, layer, **_ONCE),
                   pl.BlockSpec((tm, LANES), pos),
                   pl.BlockSpec((tm, LANES), pos),
                   _whole(tabs["zeta_tile"])]
                  + [_layer_slab(a, layer) for a in mps]),
        out_specs=[pl.BlockSpec((tm, lay.na), lambda i: (i, 0)),
                   pl.BlockSpec((tm, 2 * LANES), lambda i: (i, 0)),
                   pl.BlockSpec((tm // CHUNK, lay.mpad, CHUNK), lambda i: (i, 0, 0))],
        out_shape=[jax.ShapeDtypeStruct((t, lay.na), BF16),
                   jax.ShapeDtypeStruct((t, 2 * LANES), F32),
                   jax.ShapeDtypeStruct((t // CHUNK, lay.mpad, CHUNK), F32)],
        scratch_shapes=[pltpu.VMEM((tm + 8, lay.mw), F32)],
        compiler_params=pltpu.CompilerParams(
            dimension_semantics=("arbitrary",), vmem_limit_bytes=_vmem_limit(est)),
        name="inproj",
    )(x2, dense["g_mix"], w, tabs["cos"], tabs["sin"], tabs["zeta_tile"], *mps)


_MIXER_PARAMS = ("mnormt", "rnorm", "wcat", "gbmat")
_TABLES = ("decay", "xi", "cdt", "bmaskt")


def _block_diag2(x, y):
    z = jnp.zeros_like(x)
    return jnp.concatenate([jnp.concatenate([x, z], axis=1), jnp.concatenate([z, y], axis=1)], axis=0)


def _mixer_chunk(c, act, gates, vt, mp, tb, out, c_st, r_st, m_st, lay):
    md, mw, rw, gw = lay.md, lay.mw, lay.rw, lay.gw
    li = lax.broadcasted_iota(jnp.int32, (CHUNK, LANES), 1)
    ri = lax.broadcasted_iota(jnp.int32, (CHUNK, LANES), 0)
    causal_t = ri <= li
    tril = (li <= ri).astype(BF16)
    lo_half = li < (LANES // 2)
    rows = pl.ds(pl.multiple_of(c * CHUNK, CHUNK), CHUNK)

    def a(col, width=LANES):
        return act[rows, col:col + width]

    gi = gates[rows, 0:LANES]
    logf = gates[rows, LANES:2 * LANES]
    a1 = logf.astype(BF16)
    r1 = logf - a1.astype(F32)
    a2 = r1.astype(BF16)
    a3 = (r1 - a2.astype(F32)).astype(BF16)
    b3 = _dot(tril, jnp.concatenate([a1, a2, a3], axis=1))
    bcum = (b3[:, 2 * LANES:3 * LANES] + b3[:, LANES:2 * LANES]) + b3[:, 0:LANES]
    wb = jnp.where(li < M_HEADS, gi - bcum, bcum)
    wb_t = wb.T

    causal_t2 = jnp.concatenate([causal_t, causal_t], axis=1)
    hn_t = []
    for hp in range(M_HEADS // 2):
        heads = (2 * hp, 2 * hp + 1)
        pair = lambda f: jnp.concatenate([f(h) for h in heads], axis=1)
        rep = lambda v: jnp.broadcast_to(v, (1, LANES))
        q2 = a(lay.a_q + 2 * hp * LANES, 2 * LANES)
        k2 = a(lay.a_k + 2 * hp * LANES, 2 * LANES)
        vt2 = pair(lambda h: vt[c, h * LANES:(h + 1) * LANES, :])
        w_row = pair(lambda h: wb_t[h:h + 1, :])
        b_row = pair(lambda h: wb_t[M_HEADS + h:M_HEADS + h + 1, :])
        b_end = pair(lambda h: rep(wb_t[M_HEADS + h:M_HEADS + h + 1, CHUNK - 1:CHUNK]))
        w_end = b_end + w_row
        a_c = pair(lambda h: rep(jnp.max(w_end[:, (h % 2) * LANES:(h % 2 + 1) * LANES], axis=-1, keepdims=True)))
        e_row = jnp.exp(w_end - a_c)
        m_prev = m_st[hp:hp + 1, :]
        m_new = jnp.maximum(b_end + m_prev, a_c)
        s_old = jnp.exp(b_end + m_prev - m_new)
        s_new = jnp.exp(a_c - m_new)
        m_st[hp:hp + 1, :] = m_new

        il = b_row + m_prev
        dm = jnp.where(causal_t2, pair(lambda h: jnp.broadcast_to(wb[:, h:h + 1], (CHUNK, LANES))) + b_row,
                       -jnp.inf)
        m_t = jnp.maximum(il, jnp.max(dm, axis=0, keepdims=True))
        s_inter = jnp.exp(il - m_t)
        c_prev = c_st[hp]
        kq = _dot_nt(jnp.concatenate([k2, c_prev.astype(BF16)], axis=0),
                     _block_diag2(q2[:, 0:LANES], q2[:, LANES:2 * LANES]))
        p_t = kq[0:CHUNK] * jnp.exp(dm - m_t)
        inter = s_inter * kq[CHUNK:2 * CHUNK]
        p_b = p_t.astype(BF16)
        tot = _dot(vt2.astype(BF16), _block_diag2(p_b[:, 0:LANES], p_b[:, LANES:2 * LANES])) + inter
        den = jnp.sum(p_t, axis=0, keepdims=True) + inter[md:md + 1, :]
        rden = 1.0 / jnp.maximum(jnp.abs(den), jnp.exp(-m_t))
        hout = tot[0:md, :] * rden
        ms = jnp.sum(hout * hout, axis=0, keepdims=True) * (1.0 / md)
        hn = hout * lax.rsqrt(ms + EPS)
        for i, h in enumerate(heads):
            hn_t.append(hn[:, i * LANES:(i + 1) * LANES] * mp["mnormt"][h * md:(h + 1) * md, :])
        c_st[hp] = s_old * c_prev + s_new * _dot((vt2 * e_row).astype(BF16),
                                                 _block_diag2(k2[:, 0:LANES], k2[:, LANES:2 * LANES]))

    hm_t = jnp.concatenate(hn_t, axis=0)
    hm = jnp.concatenate([hm_t[t * LANES:(t + 1) * LANES, :].T for t in range(mw // LANES)], axis=1)
    out[rows, 0:mw] = (hm * a(lay.a_zs, mw).astype(F32) + a(lay.a_sxz, mw).astype(F32)).astype(out.dtype)

    for p_i in range(rw // LANES):
        off = p_i * LANES
        qb = a(lay.a_rq + off)
        rva, rvb = a(lay.a_rva + off), a(lay.a_rvb + off)
        r_prev = r_st[p_i]
        sqr = _dot_nt(qb, jnp.concatenate([a(lay.a_rka + off), a(lay.a_rkb + off), r_prev.astype(BF16)],
                                          axis=0))
        p_ab = (sqr[:, 0:2 * LANES] * tb["decay"][p_i]).astype(BF16)
        hr = _dot(p_ab, jnp.concatenate([rva, rvb], axis=0)) + sqr[:, 2 * LANES:3 * LANES] * tb["xi"][p_i]
        sq = hr * hr
        ms_a = jnp.sum(jnp.where(lo_half, sq, 0.0), axis=-1, keepdims=True)
        ms_b = jnp.sum(jnp.where(lo_half, 0.0, sq), axis=-1, keepdims=True)
        ms = jnp.where(lo_half, ms_a, ms_b) * (1.0 / lay.rd)
        hn = hr * lax.rsqrt(ms + EPS) * mp["rnorm"][:, off:off + LANES]
        out[rows, mw + off:mw + off + LANES] = (hn * a(lay.a_rgs + off).astype(F32)).astype(out.dtype)
        r_st[p_i] = tb["cdt"][p_i] * r_prev + tb["bmaskt"][...] * _dot_tn(rva + rvb, a(lay.a_kz + off))

    zero = jnp.zeros((2 * CHUNK, LANES), BF16)
    vst = [jnp.concatenate([a(lay.a_gva + q_i * LANES), a(lay.a_gvb + q_i * LANES)], axis=0)
           for q_i in range(gw // LANES)]
    assert len(vst) == 2
    mixed = _dot(mp["wcat"][...], jnp.concatenate([jnp.concatenate([vst[0], zero], axis=1),
                                                   jnp.concatenate([zero, vst[1]], axis=1)], axis=0))
    out[rows, mw + rw:mw + rw + gw] = (a(lay.a_gug, gw).astype(F32) * (mixed + mp["gbmat"][...])).astype(out.dtype)


def _mixer_kernel(*refs, lay):
    n_mp, n_tb = len(_MIXER_PARAMS), len(_TABLES)
    act_ref, gates_ref, vt_ref = refs[0:3]
    mp = dict(zip(_MIXER_PARAMS, refs[3:3 + n_mp]))
    tb = dict(zip(_TABLES, refs[3 + n_mp:3 + n_mp + n_tb]))
    o_ref = refs[3 + n_mp + n_tb]
    c_st, r_st, m_st = refs[4 + n_mp + n_tb:]
    nrow, ts = act_ref.shape[0], act_ref.shape[1]

    @pl.when(pl.program_id(1) == 0)
    def _():
        c_st[...] = jnp.zeros_like(c_st)
        r_st[...] = jnp.zeros_like(r_st)
        m_st[...] = jnp.zeros_like(m_st)

    def step(c, carry):
        for r in range(nrow):
            _mixer_chunk(c, act_ref.at[r], gates_ref.at[r], vt_ref.at[r], mp, tb, o_ref.at[r],
                         c_st.at[r], r_st.at[r], m_st.at[r], lay)
        return carry

    lax.fori_loop(0, ts // CHUNK, step, 0)


def _mixer(act, gates, vt, tabs, lw, layer, lay, batch, seq):
    ts, nrow = MIX_TILE, MIX_ROWS
    mps = [lw[k] for k in _MIXER_PARAMS]
    tbs = [tabs[k] for k in _TABLES]
    scratch = [pltpu.VMEM((nrow, M_HEADS // 2, LANES, 2 * LANES), F32),
               pltpu.VMEM((nrow, lay.rw // LANES, LANES, LANES), F32),
               pltpu.VMEM((nrow, 8, 2 * LANES), F32)]
    est = (2 * nrow * ts * (lay.na * 2 + 2 * LANES * 4 + lay.mpad * 4 + lay.dmix * 2)
           + 2 * sum(_nbytes(a.shape[1:], a.dtype) for a in mps) + 2 * sum(_nbytes(a.shape, a.dtype) for a in tbs)
           + sum(_nbytes(s.shape, s.dtype) for s in scratch))
    return pl.pallas_call(
        functools.partial(_mixer_kernel, lay=lay),
        grid=(batch // nrow, seq // ts),
        in_specs=([pl.BlockSpec((nrow, ts, lay.na), lambda b, j: (b, j, 0)),
                   pl.BlockSpec((nrow, ts, 2 * LANES), lambda b, j: (b, j, 0)),
                   pl.BlockSpec((nrow, ts // CHUNK, lay.mpad, CHUNK), lambda b, j: (b, j, 0, 0))]
                  + [_layer_slab(a, layer) for a in mps] + [_whole(a) for a in tbs]),
        out_specs=pl.BlockSpec((nrow, ts, lay.dmix), lambda b, j: (b, j, 0)),
        out_shape=jax.ShapeDtypeStruct((batch, seq, lay.dmix), BF16),
        scratch_shapes=scratch,
        compiler_params=pltpu.CompilerParams(
            dimension_semantics=("arbitrary", "arbitrary"), vmem_limit_bytes=_vmem_limit(est)),
        name="mixer",
    )(act, gates, vt, *mps, *tbs)


def _ffn_kernel(mix_ref, x_ref, wo_ref, g_ref, w1_ref, w2_ref, gf_ref, o_ref, *, final):
    x1 = x_ref[...] + _dot(mix_ref[...], wo_ref[...])
    h2 = _rms(x1, g_ref[...]).astype(BF16)
    acc = x1
    for j in range(w1_ref.shape[1] // FF_CHUNK):
        cols = slice(j * FF_CHUNK, (j + 1) * FF_CHUNK)
        a = jnp.square(jnp.maximum(_dot(h2, w1_ref[:, cols]), 0.0)).astype(BF16)
        acc = acc + _dot(a, w2_ref[cols, :])
    if final:
        acc = _rms(acc, gf_ref[...])
    o_ref[...] = acc


def _ffn(mix, x2, dense, layer, final):
    t, d = x2.shape
    dff = dense["w1"].shape[2]
    tm = TOKEN_TILE
    est = (2 * tm * d * 2 + 4 * tm * d * 4 + (d * d + 2 * d * dff) * 2
           + tm * FF_CHUNK * 6 + 3 * tm * d * 4)
    return pl.pallas_call(
        functools.partial(_ffn_kernel, final=final),
        grid=(t // tm,),
        in_specs=[pl.BlockSpec((tm, d), lambda i: (i, 0)),
                  pl.BlockSpec((tm, d), lambda i: (i, 0)),
                  _layer_slab(dense["w_out"], layer, **_ONCE),
                  _layer_slab(dense["g_ff"], layer),
                  _layer_slab(dense["w1"], layer, **_ONCE),
                  _layer_slab(dense["w2"], layer, **_ONCE),
                  pl.BlockSpec((1, d), lambda i: (0, 0))],
        out_specs=pl.BlockSpec((tm, d), lambda i: (i, 0)),
        out_shape=jax.ShapeDtypeStruct((t, d), F32),
        compiler_params=pltpu.CompilerParams(
            dimension_semantics=("arbitrary",), vmem_limit_bytes=_vmem_limit(est)),
        name="ffn_final" if final else "ffn",
    )(mix, x2, dense["w_out"], dense["g_ff"], dense["w1"], dense["w2"], dense["g_final"])


def _tables(lay, seq):
    rd, half = lay.rd, lay.rd // 2
    l = CHUNK
    pos = jnp.arange(seq, dtype=F32)
    freqs = ROPE_THETA ** (-jnp.arange(half, dtype=F32) / half)
    ang = pos[:, None] * freqs[None, :]
    sign = jnp.asarray(np.where(np.arange(LANES) < LANES // 2, -1.0, 1.0), F32)
    cos = jnp.tile(jnp.cos(ang), (1, LANES // half))
    sin = jnp.tile(jnp.sin(ang), (1, LANES // half)) * sign[None, :]
    log_gamma = jnp.log(1.0 - 2.0 ** (-5.0 - jnp.arange(R_HEADS, dtype=F32)))
    p = jnp.arange(l, dtype=F32)
    causal = jnp.tril(jnp.ones((l, l), dtype=bool))
    rel = jnp.where(causal, p[:, None] - p[None, :], 0.0)
    decay = jnp.where(causal, jnp.exp(rel[None] * log_gamma[:, None, None]), 0.0)
    zeta = jnp.exp((l - 1.0 - p)[None, :] * log_gamma[:, None])
    xi = jnp.exp((p + 1.0)[None, :] * log_gamma[:, None])
    cdec = jnp.exp(l * log_gamma)
    npair = R_HEADS // 2
    decay_ab = decay.reshape(npair, 2, l, l).transpose(0, 2, 1, 3).reshape(npair, l, 2 * l)
    by_v = lambda a: jnp.repeat(a.reshape(npair, 2, -1), rd, axis=1)
    by_k = lambda a: jnp.tile(jnp.repeat(a.reshape(npair, 2, -1), half, axis=1), (1, 2, 1))
    xi_ab = by_v(xi).transpose(0, 2, 1)
    zeta_k = by_k(zeta).transpose(0, 2, 1)
    zeta_tile = jnp.tile(zeta_k.transpose(1, 0, 2).reshape(l, npair * LANES), (TOKEN_TILE // l, 1))
    cdt = jnp.broadcast_to(by_k(cdec).transpose(0, 2, 1), (npair, LANES, LANES))
    lane128 = np.arange(LANES)
    head_v = lane128 // rd
    head_k = (lane128 % (2 * half)) // half
    bmaskt = jnp.asarray((head_v[:, None] == head_k[None, :]).astype(np.float32))
    return {"cos": cos, "sin": sin, "decay": decay_ab, "xi": xi_ab, "zeta_tile": zeta_tile, "cdt": cdt,
            "bmaskt": bmaskt}


def _block_diag_padded(w, lay):
    depth, nb = w.shape[0], w.shape[1]
    rows = w.reshape(depth, nb * M_BLOCK, M_BLOCK)
    tiled = jnp.tile(rows, (1, 1, lay.mpad // M_BLOCK))
    r = np.arange(nb * M_BLOCK)[:, None]
    c = np.arange(lay.mpad)[None, :]
    packed_c = (c // LANES) * lay.md + c % LANES
    keep = (c % LANES < lay.md) & (r // M_BLOCK == packed_c // M_BLOCK)
    return jnp.where(jnp.asarray(keep), tiled, 0.0).astype(BF16)


def _mixer_weights(lay, p):
    depth = p["m_wq"].shape[0]
    gbi = jnp.concatenate([p["m_i_bias"], jnp.zeros((depth, LANES - M_HEADS), F32)], axis=-1)
    gbf = jnp.concatenate(
        [p["m_f_bias"], p["m_f_bias"], jnp.zeros((depth, LANES - 2 * M_HEADS), F32)], axis=-1)
    causal = jnp.tril(jnp.ones((CHUNK, CHUNK), dtype=bool))
    wm = jnp.where(causal, p["g_ws"], 0.0)
    wcat = wm.transpose(0, 2, 1, 3).reshape(depth, CHUNK, G_GROUPS * CHUNK)
    gbmat = jnp.repeat(p["g_bs"].transpose(0, 2, 1), lay.gw // G_GROUPS, axis=2)
    row = lambda a: a[:, None, :]
    return {
        "convw": p["m_conv_w"], "convb": row(p["m_conv_b"]),
        "bdq": _block_diag_padded(p["m_wq"], lay),
        "bdk": _block_diag_padded(p["m_wk"], lay),
        "bdvt": _block_diag_padded(p["m_wv"], lay).transpose(0, 2, 1),
        "gbi": row(gbi), "gbf": row(gbf),
        "mnormt": jnp.broadcast_to(p["m_norm_g"][:, :, None], (depth, lay.mw, LANES)),
        "mskip": row(p["m_skip"]),
        "rnorm": row(p["r_norm_g"]),
        "gnorm": row(p["g_norm_g"]),
        "wcat": wcat.astype(BF16), "gbmat": gbmat,
    }


def kernel(x, norm_mix_g, w_in, m_conv_w, m_conv_b, m_wq, m_wk, m_wv, m_i_bias, m_f_bias, m_norm_g,
           m_skip, r_norm_g, g_norm_g, g_ws, g_bs, w_out, norm_ff_g, w_ff1, w_ff2, final_norm_g):
    batch, seq, d = x.shape
    depth = w_in.shape[0]
    lay = _Layout(d)
    assert seq % MIX_TILE == 0 and MIX_TILE % CHUNK == 0 and batch % MIX_ROWS == 0
    assert seq % TOKEN_TILE == 0 and TOKEN_TILE % CHUNK == 0 and w_ff1.shape[2] % FF_CHUNK == 0
    params = dict(m_conv_w=m_conv_w, m_conv_b=m_conv_b, m_wq=m_wq, m_wk=m_wk, m_wv=m_wv,
                  m_i_bias=m_i_bias, m_f_bias=m_f_bias, m_norm_g=m_norm_g, m_skip=m_skip,
                  r_norm_g=r_norm_g, g_norm_g=g_norm_g, g_ws=g_ws, g_bs=g_bs)
    tabs = _tables(lay, seq)
    lw = _mixer_weights(lay, params)
    dense = {
        "w_in": lay.relayout_in_proj(w_in),
        "w_out": w_out.astype(BF16), "w1": w_ff1.astype(BF16), "w2": w_ff2.astype(BF16),
        "g_mix": norm_mix_g[:, None, :], "g_ff": norm_ff_g[:, None, :], "g_final": final_norm_g[None, :],
    }
    x2 = x.reshape(batch * seq, d)
    for layer in range(depth):
        act, gates, vt = _inproj(x2, tabs, lw, dense, layer, lay, seq)
        mix = _mixer(act.reshape(batch, seq, lay.na), gates.reshape(batch, seq, 2 * LANES),
                     vt.reshape(batch, seq // CHUNK, lay.mpad, CHUNK), tabs, lw, layer, lay, batch, seq)
        x2 = _ffn(mix.reshape(batch * seq, lay.dmix), x2, dense, layer, final=(layer == depth - 1))
    return x2.reshape(batch, seq, d)
```

```python
import functools

import numpy as np
import jax
import jax.numpy as jnp
from jax import lax
from jax.experimental import pallas as pl
from jax.experimental.pallas import tpu as pltpu

M_HEADS = 4
M_CONV = 4
M_BLOCK = 4
R_HEADS = 6
G_GROUPS = 4
CHUNK = 128
ROPE_THETA = 10000.0
EPS = 1e-6

LANES = 128
V7X_VMEM_BYTES = 64 * 1024 * 1024

TOKEN_TILE = 512
FFN_TILE = 1024
MIX_TILE = 512
MIX_ROWS = 2
FF_CHUNK = 1024

F32 = jnp.float32
BF16 = jnp.bfloat16


def _rms(x, g):
    return x * lax.rsqrt(jnp.mean(jnp.square(x), axis=-1, keepdims=True) + EPS) * g


def _silu(x):
    return x * jax.nn.sigmoid(x)


def _dot(a, b):
    return jnp.dot(a, b, preferred_element_type=F32)


def _dot_nt(a, b):
    return lax.dot_general(a, b, (((1,), (1,)), ((), ())), preferred_element_type=F32)


def _dot_tn(a, b):
    return lax.dot_general(a, b, (((0,), (0,)), ((), ())), preferred_element_type=F32)


def _nbytes(shape, dtype):
    return int(np.prod(shape)) * jnp.dtype(dtype).itemsize


def _vmem_limit(nbytes):
    return int(min(nbytes + (12 << 20), V7X_VMEM_BYTES - (4 << 20)))


def _layer_slab(a, layer, **kw):
    return pl.BlockSpec((None,) + a.shape[1:], lambda *g: (layer,) + (0,) * (a.ndim - 1), **kw)


def _whole(a):
    return pl.BlockSpec(a.shape, lambda *g: (0,) * a.ndim)


_ONCE = dict(pipeline_mode=pl.Buffered(1))


class _Layout:
    def __init__(self, d_model):
        self.mw = 3 * d_model // 8
        self.rw = 3 * d_model // 8
        self.gw = d_model - self.mw - self.rw
        self.md = self.mw // M_HEADS
        self.rd = self.rw // R_HEADS
        assert self.md <= LANES and 2 * self.rd == LANES and self.gw // G_GROUPS * 2 == LANES
        assert self.mw % LANES == 0 and self.rw % LANES == 0 and self.gw % LANES == 0
        self.mpad = M_HEADS * LANES
        self.dmix = self.mw + self.rw + self.gw
        o = 0
        self.mx = o; o += self.mw
        self.mz = o; o += self.mw
        self.rq = o; o += self.rw
        self.rk = o; o += self.rw
        self.rv = o; o += self.rw
        self.rg = o; o += self.rw
        self.gu = o; o += self.gw
        self.gv = o; o += self.gw
        self.gi = o; o += LANES
        self.gf = o; o += LANES
        self.n = o
        o = 0
        self.a_q = o; o += self.mpad
        self.a_k = o; o += self.mpad
        self.a_rq = o; o += self.rw
        self.a_rka = o; o += self.rw
        self.a_rkb = o; o += self.rw
        self.a_kz = o; o += self.rw
        self.a_rva = o; o += self.rw
        self.a_rvb = o; o += self.rw
        self.a_gva = o; o += self.gw
        self.a_gvb = o; o += self.gw
        self.a_zs = o; o += self.mw
        self.a_sxz = o; o += self.mw
        self.a_rgs = o; o += self.rw
        self.a_gug = o; o += self.gw
        self.na = o

    def relayout_in_proj(self, w_in):
        mw, rw, gw = self.mw, self.rw, self.gw
        depth, d, _ = w_in.shape
        o_gate = 2 * mw
        o_rq = o_gate + 2 * M_HEADS
        parts = [w_in[:, :, 0:2 * mw],
                 w_in[:, :, o_rq:o_rq + 4 * rw + 2 * gw],
                 w_in[:, :, o_gate:o_gate + M_HEADS],
                 jnp.zeros((depth, d, LANES - M_HEADS), w_in.dtype),
                 w_in[:, :, o_gate + M_HEADS:o_gate + 2 * M_HEADS],
                 w_in[:, :, o_gate + M_HEADS:o_gate + 2 * M_HEADS],
                 jnp.zeros((depth, d, LANES - 2 * M_HEADS), w_in.dtype)]
        return jnp.concatenate(parts, axis=-1).astype(BF16)


_INPROJ_PARAMS = ("convw", "convb", "bdq", "bdk", "bdvt", "gbi", "gbf", "mskip", "gnorm")


def _inproj_kernel(*refs, lay, nj):
    n_p = len(_INPROJ_PARAMS)
    x_ref, g_ref, w_ref, cos_ref, sin_ref, zeta_ref = refs[0:6]
    mp = dict(zip(_INPROJ_PARAMS, refs[6:6 + n_p]))
    act_ref, gates_ref, vt_ref = refs[6 + n_p:9 + n_p]
    xbuf = refs[9 + n_p]
    tm = x_ref.shape[0]
    md, mw, rw, gw = lay.md, lay.mw, lay.rw, lay.gw
    first_of_seq = lax.rem(pl.program_id(0), nj) == 0

    @pl.when(first_of_seq)
    def _():
        xbuf[0:8, :] = jnp.zeros((8, mw), F32)

    @pl.when(jnp.logical_not(first_of_seq))
    def _():
        xbuf[0:8, :] = xbuf[tm:tm + 8, :]

    proj = _dot(_rms(x_ref[...], g_ref[...]).astype(BF16), w_ref[...])

    def put(col, val):
        act_ref[:, col:col + val.shape[1]] = val.astype(act_ref.dtype)

    mx = proj[:, lay.mx:lay.mx + mw]
    xbuf[8:tm + 8, :] = mx
    conv = mp["convb"][...] + mp["convw"][M_CONV - 1:M_CONV, :] * mx
    for t in range(M_CONV - 1):
        conv = conv + mp["convw"][t:t + 1, :] * xbuf[8 - (M_CONV - 1) + t:8 - (M_CONV - 1) + t + tm, :]
    xc = _silu(conv)
    xcb = xc.astype(BF16)
    put(lay.a_q, _dot(xcb, mp["bdq"][...]))
    put(lay.a_k, _dot(xcb, mp["bdk"][...]) * (md ** -0.5))
    row_p = lax.broadcasted_iota(jnp.int32, (lay.mpad, 1), 0) % LANES
    vt = _dot_nt(mp["bdvt"][...], mx.astype(BF16)) + (row_p == md).astype(F32)
    for c in range(tm // CHUNK):
        vt_ref[c] = vt[:, c * CHUNK:(c + 1) * CHUNK]
    zs = _silu(proj[:, lay.mz:lay.mz + mw])
    put(lay.a_zs, zs)
    put(lay.a_sxz, mp["mskip"][...] * xc * zs)

    li = lax.broadcasted_iota(jnp.int32, (tm, LANES), 1)
    gates_ref[:, 0:LANES] = proj[:, lay.gi:lay.gi + LANES] + mp["gbi"][...]
    gf = proj[:, lay.gf:lay.gf + LANES] + mp["gbf"][...]
    gates_ref[:, LANES:2 * LANES] = jnp.where(li < 2 * M_HEADS, jax.nn.log_sigmoid(gf), 0.0)

    lo_half = li < (LANES // 2)
    half = lay.rd // 2
    first_half = (li % lay.rd) < half
    cs = cos_ref[...]
    sn = sin_ref[...]

    def rotary(v):
        partner = jnp.where(first_half, pltpu.roll(v, LANES - half, axis=1), pltpu.roll(v, half, axis=1))
        return v * cs + partner * sn

    for p_i in range(rw // LANES):
        off = p_i * LANES
        rq = proj[:, lay.rq + off:lay.rq + off + LANES]
        rk = proj[:, lay.rk + off:lay.rk + off + LANES]
        rv = proj[:, lay.rv + off:lay.rv + off + LANES]
        put(lay.a_rq + off, rotary(rq))
        kr = rotary(rk) * (lay.rd ** -0.5)
        put(lay.a_rka + off, jnp.where(lo_half, kr, 0.0))
        put(lay.a_rkb + off, jnp.where(lo_half, 0.0, kr))
        put(lay.a_kz + off, kr * zeta_ref[:, off:off + LANES])
        put(lay.a_rva + off, jnp.where(lo_half, rv, 0.0))
        put(lay.a_rvb + off, jnp.where(lo_half, 0.0, rv))
    put(lay.a_rgs, _silu(proj[:, lay.rg:lay.rg + rw]))

    put(lay.a_gug, jax.nn.gelu(proj[:, lay.gu:lay.gu + gw]))
    gv = jax.nn.gelu(proj[:, lay.gv:lay.gv + gw])
    gvn = gv * lax.rsqrt(jnp.mean(gv * gv, axis=-1, keepdims=True) + EPS) * mp["gnorm"][...]
    for q_i in range(gw // LANES):
        off = q_i * LANES
        put(lay.a_gva + off, jnp.where(lo_half, gvn[:, off:off + LANES], 0.0))
        put(lay.a_gvb + off, jnp.where(lo_half, 0.0, gvn[:, off:off + LANES]))


def _inproj(x2, tabs, lw, dense, layer, lay, seq):
    t, d = x2.shape
    tm = TOKEN_TILE
    nj = seq // tm
    w = dense["w_in"]
    mps = [lw[k] for k in _INPROJ_PARAMS]
    pos = lambda i: (lax.rem(i, nj), 0)
    est = (2 * tm * d * 4 + d * lay.n * 2 + 2 * tm * lay.na * 2 + 2 * tm * 2 * LANES * 4
           + 2 * tm * lay.mpad * 4 + 6 * tm * LANES * 4 + (tm + 8) * lay.mw * 4 + tm * lay.n * 4
           + 2 * sum(_nbytes(a.shape[1:], a.dtype) for a in mps))
    return pl.pallas_call(
        functools.partial(_inproj_kernel, lay=lay, nj=nj),
        grid=(t // tm,),
        in_specs=([pl.BlockSpec((tm, d), lambda i: (i, 0)),
                   _layer_slab(dense["g_mix"], layer),
                   _layer_slab(w, layer, **_ONCE),
                   pl.BlockSpec((tm, LANES), pos),
                   pl.BlockSpec((tm, LANES), pos),
                   _whole(tabs["zeta_tile"])]
                  + [_layer_slab(a, layer) for a in mps]),
        out_specs=[pl.BlockSpec((tm, lay.na), lambda i: (i, 0)),
                   pl.BlockSpec((tm, 2 * LANES), lambda i: (i, 0)),
                   pl.BlockSpec((tm // CHUNK, lay.mpad, CHUNK), lambda i: (i, 0, 0))],
        out_shape=[jax.ShapeDtypeStruct((t, lay.na), BF16),
                   jax.ShapeDtypeStruct((t, 2 * LANES), F32),
                   jax.ShapeDtypeStruct((t // CHUNK, lay.mpad, CHUNK), F32)],
        scratch_shapes=[pltpu.VMEM((tm + 8, lay.mw), F32)],
        compiler_params=pltpu.CompilerParams(
            dimension_semantics=("arbitrary",), vmem_limit_bytes=_vmem_limit(est)),
        name="inproj",
    )(x2, dense["g_mix"], w, tabs["cos"], tabs["sin"], tabs["zeta_tile"], *mps)


_MIXER_PARAMS = ("mnormt", "rnorm", "wcat", "gbmat")
_TABLES = ("decay", "xi", "cdt", "bmaskt")


def _block_diag2(x, y):
    z = jnp.zeros_like(x)
    return jnp.concatenate([jnp.concatenate([x, z], axis=1), jnp.concatenate([z, y], axis=1)], axis=0)


def _mixer_chunk(c, act, gates, vt, mp, tb, out, c_st, r_st, m_st, lay):
    md, mw, rw, gw = lay.md, lay.mw, lay.rw, lay.gw
    li = lax.broadcasted_iota(jnp.int32, (CHUNK, LANES), 1)
    ri = lax.broadcasted_iota(jnp.int32, (CHUNK, LANES), 0)
    causal_t = ri <= li
    tril = (li <= ri).astype(BF16)
    lo_half = li < (LANES // 2)
    rows = pl.ds(pl.multiple_of(c * CHUNK, CHUNK), CHUNK)

    def a(col, width=LANES):
        return act[rows, col:col + width]

    gi = gates[rows, 0:LANES]
    logf = gates[rows, LANES:2 * LANES]
    a1 = logf.astype(BF16)
    r1 = logf - a1.astype(F32)
    a2 = r1.astype(BF16)
    a3 = (r1 - a2.astype(F32)).astype(BF16)
    b3 = _dot(tril, jnp.concatenate([a1, a2, a3], axis=1))
    bcum = (b3[:, 2 * LANES:3 * LANES] + b3[:, LANES:2 * LANES]) + b3[:, 0:LANES]
    wb = jnp.where(li < M_HEADS, gi - bcum, bcum)
    wb_t = wb.T

    causal_t2 = jnp.concatenate([causal_t, causal_t], axis=1)
    hn_t = []
    for hp in range(M_HEADS // 2):
        heads = (2 * hp, 2 * hp + 1)
        pair = lambda f: jnp.concatenate([f(h) for h in heads], axis=1)
        rep = lambda v: jnp.broadcast_to(v, (1, LANES))
        q2 = a(lay.a_q + 2 * hp * LANES, 2 * LANES)
        k2 = a(lay.a_k + 2 * hp * LANES, 2 * LANES)
        vt2 = pair(lambda h: vt[c, h * LANES:(h + 1) * LANES, :])
        w_row = pair(lambda h: wb_t[h:h + 1, :])
        b_row = pair(lambda h: wb_t[M_HEADS + h:M_HEADS + h + 1, :])
        b_end = pair(lambda h: rep(wb_t[M_HEADS + h:M_HEADS + h + 1, CHUNK - 1:CHUNK]))
        w_end = b_end + w_row
        a_c = pair(lambda h: rep(jnp.max(w_end[:, (h % 2) * LANES:(h % 2 + 1) * LANES], axis=-1, keepdims=True)))
        e_row = jnp.exp(w_end - a_c)
        m_prev = m_st[hp:hp + 1, :]
        m_new = jnp.maximum(b_end + m_prev, a_c)
        s_old = jnp.exp(b_end + m_prev - m_new)
        s_new = jnp.exp(a_c - m_new)
        m_st[hp:hp + 1, :] = m_new

        il = b_row + m_prev
        dm = jnp.where(causal_t2, pair(lambda h: jnp.broadcast_to(wb[:, h:h + 1], (CHUNK, LANES))) + b_row,
                       -jnp.inf)
        m_t = jnp.maximum(il, jnp.max(dm, axis=0, keepdims=True))
        s_inter = jnp.exp(il - m_t)
        c_prev = c_st[hp]
        kq = _dot_nt(jnp.concatenate([k2, c_prev.astype(BF16)], axis=0),
                     _block_diag2(q2[:, 0:LANES], q2[:, LANES:2 * LANES]))
        p_t = kq[0:CHUNK] * jnp.exp(dm - m_t)
        inter = s_inter * kq[CHUNK:2 * CHUNK]
        p_b = p_t.astype(BF16)
        tot = _dot(vt2.astype(BF16), _block_diag2(p_b[:, 0:LANES], p_b[:, LANES:2 * LANES])) + inter
        den = jnp.sum(p_t, axis=0, keepdims=True) + inter[md:md + 1, :]
        rden = 1.0 / jnp.maximum(jnp.abs(den), jnp.exp(-m_t))
        hout = tot[0:md, :] * rden
        ms = jnp.sum(hout * hout, axis=0, keepdims=True) * (1.0 / md)
        hn = hout * lax.rsqrt(ms + EPS)
        for i, h in enumerate(heads):
            hn_t.append(hn[:, i * LANES:(i + 1) * LANES] * mp["mnormt"][h * md:(h + 1) * md, :])
        c_st[hp] = s_old * c_prev + s_new * _dot((vt2 * e_row).astype(BF16),
                                                 _block_diag2(k2[:, 0:LANES], k2[:, LANES:2 * LANES]))

    hm_t = jnp.concatenate(hn_t, axis=0)
    hm = jnp.concatenate([hm_t[t * LANES:(t + 1) * LANES, :].T for t in range(mw // LANES)], axis=1)
    out[rows, 0:mw] = (hm * a(lay.a_zs, mw).astype(F32) + a(lay.a_sxz, mw).astype(F32)).astype(out.dtype)

    for p_i in range(rw // LANES):
        off = p_i * LANES
        qb = a(lay.a_rq + off)
        rva, rvb = a(lay.a_rva + off), a(lay.a_rvb + off)
        r_prev = r_st[p_i]
        sqr = _dot_nt(qb, jnp.concatenate([a(lay.a_rka + off), a(lay.a_rkb + off), r_prev.astype(BF16)],
                                          axis=0))
        p_ab = (sqr[:, 0:2 * LANES] * tb["decay"][p_i]).astype(BF16)
        hr = _dot(p_ab, jnp.concatenate([rva, rvb], axis=0)) + sqr[:, 2 * LANES:3 * LANES] * tb["xi"][p_i]
        sq = hr * hr
        ms_a = jnp.sum(jnp.where(lo_half, sq, 0.0), axis=-1, keepdims=True)
        ms_b = jnp.sum(jnp.where(lo_half, 0.0, sq), axis=-1, keepdims=True)
        ms = jnp.where(lo_half, ms_a, ms_b) * (1.0 / lay.rd)
        hn = hr * lax.rsqrt(ms + EPS) * mp["rnorm"][:, off:off + LANES]
        out[rows, mw + off:mw + off + LANES] = (hn * a(lay.a_rgs + off).astype(F32)).astype(out.dtype)
        r_st[p_i] = tb["cdt"][p_i] * r_prev + tb["bmaskt"][...] * _dot_tn(rva + rvb, a(lay.a_kz + off))

    zero = jnp.zeros((2 * CHUNK, LANES), BF16)
    vst = [jnp.concatenate([a(lay.a_gva + q_i * LANES), a(lay.a_gvb + q_i * LANES)], axis=0)
           for q_i in range(gw // LANES)]
    assert len(vst) == 2
    mixed = _dot(mp["wcat"][...], jnp.concatenate([jnp.concatenate([vst[0], zero], axis=1),
                                                   jnp.concatenate([zero, vst[1]], axis=1)], axis=0))
    out[rows, mw + rw:mw + rw + gw] = (a(lay.a_gug, gw).astype(F32) * (mixed + mp["gbmat"][...])).astype(out.dtype)


def _mixer_kernel(*refs, lay):
    n_mp, n_tb = len(_MIXER_PARAMS), len(_TABLES)
    act_ref, gates_ref, vt_ref = refs[0:3]
    mp = dict(zip(_MIXER_PARAMS, refs[3:3 + n_mp]))
    tb = dict(zip(_TABLES, refs[3 + n_mp:3 + n_mp + n_tb]))
    o_ref = refs[3 + n_mp + n_tb]
    c_st, r_st, m_st = refs[4 + n_mp + n_tb:]
    nrow, ts = act_ref.shape[0], act_ref.shape[1]

    @pl.when(pl.program_id(1) == 0)
    def _():
        c_st[...] = jnp.zeros_like(c_st)
        r_st[...] = jnp.zeros_like(r_st)
        m_st[...] = jnp.zeros_like(m_st)

    def step(c, carry):
        for r in range(nrow):
            _mixer_chunk(c, act_ref.at[r], gates_ref.at[r], vt_ref.at[r], mp, tb, o_ref.at[r],
                         c_st.at[r], r_st.at[r], m_st.at[r], lay)
        return carry

    lax.fori_loop(0, ts // CHUNK, step, 0)


def _mixer(act, gates, vt, tabs, lw, layer, lay, batch, seq):
    ts, nrow = MIX_TILE, MIX_ROWS
    mps = [lw[k] for k in _MIXER_PARAMS]
    tbs = [tabs[k] for k in _TABLES]
    scratch = [pltpu.VMEM((nrow, M_HEADS // 2, LANES, 2 * LANES), F32),
               pltpu.VMEM((nrow, lay.rw // LANES, LANES, LANES), F32),
               pltpu.VMEM((nrow, 8, 2 * LANES), F32)]
    est = (2 * nrow * ts * (lay.na * 2 + 2 * LANES * 4 + lay.mpad * 4 + lay.dmix * 2)
           + 2 * sum(_nbytes(a.shape[1:], a.dtype) for a in mps) + 2 * sum(_nbytes(a.shape, a.dtype) for a in tbs)
           + sum(_nbytes(s.shape, s.dtype) for s in scratch))
    return pl.pallas_call(
        functools.partial(_mixer_kernel, lay=lay),
        grid=(batch // nrow, seq // ts),
        in_specs=([pl.BlockSpec((nrow, ts, lay.na), lambda b, j: (b, j, 0)),
                   pl.BlockSpec((nrow, ts, 2 * LANES), lambda b, j: (b, j, 0)),
                   pl.BlockSpec((nrow, ts // CHUNK, lay.mpad, CHUNK), lambda b, j: (b, j, 0, 0))]
                  + [_layer_slab(a, layer) for a in mps] + [_whole(a) for a in tbs]),
        out_specs=pl.BlockSpec((nrow, ts, lay.dmix), lambda b, j: (b, j, 0)),
        out_shape=jax.ShapeDtypeStruct((batch, seq, lay.dmix), BF16),
        scratch_shapes=scratch,
        compiler_params=pltpu.CompilerParams(
            dimension_semantics=("arbitrary", "arbitrary"), vmem_limit_bytes=_vmem_limit(est)),
        name="mixer",
    )(act, gates, vt, *mps, *tbs)


def _ffn_kernel(mix_ref, x_ref, wo_ref, g_ref, w1_ref, w2_ref, gf_ref, o_ref, *, final):
    x1 = x_ref[...] + _dot(mix_ref[...], wo_ref[...])
    h2 = _rms(x1, g_ref[...]).astype(BF16)
    acc = x1
    for j in range(w1_ref.shape[1] // FF_CHUNK):
        cols = slice(j * FF_CHUNK, (j + 1) * FF_CHUNK)
        a = jnp.square(jnp.maximum(_dot(h2, w1_ref[:, cols]), 0.0)).astype(BF16)
        acc = acc + _dot(a, w2_ref[cols, :])
    if final:
        acc = _rms(acc, gf_ref[...])
    o_ref[...] = acc


def _ffn(mix, x2, dense, layer, final):
    t, d = x2.shape
    dff = dense["w1"].shape[2]
    tm = FFN_TILE
    est = (2 * tm * d * 2 + 4 * tm * d * 4 + (d * d + 2 * d * dff) * 2
           + tm * FF_CHUNK * 6 + 3 * tm * d * 4)
    return pl.pallas_call(
        functools.partial(_ffn_kernel, final=final),
        grid=(t // tm,),
        in_specs=[pl.BlockSpec((tm, d), lambda i: (i, 0)),
                  pl.BlockSpec((tm, d), lambda i: (i, 0)),
                  _layer_slab(dense["w_out"], layer, **_ONCE),
                  _layer_slab(dense["g_ff"], layer),
                  _layer_slab(dense["w1"], layer, **_ONCE),
                  _layer_slab(dense["w2"], layer, **_ONCE),
                  pl.BlockSpec((1, d), lambda i: (0, 0))],
        out_specs=pl.BlockSpec((tm, d), lambda i: (i, 0)),
        out_shape=jax.ShapeDtypeStruct((t, d), F32),
        compiler_params=pltpu.CompilerParams(
            dimension_semantics=("arbitrary",), vmem_limit_bytes=_vmem_limit(est)),
        name="ffn_final" if final else "ffn",
    )(mix, x2, dense["w_out"], dense["g_ff"], dense["w1"], dense["w2"], dense["g_final"])


def _tables(lay, seq):
    rd, half = lay.rd, lay.rd // 2
    l = CHUNK
    pos = jnp.arange(seq, dtype=F32)
    freqs = ROPE_THETA ** (-jnp.arange(half, dtype=F32) / half)
    ang = pos[:, None] * freqs[None, :]
    sign = jnp.asarray(np.where(np.arange(LANES) % rd < half, -1.0, 1.0), F32)
    cos = jnp.tile(jnp.cos(ang), (1, LANES // half))
    sin = jnp.tile(jnp.sin(ang), (1, LANES // half)) * sign[None, :]
    log_gamma = jnp.log(1.0 - 2.0 ** (-5.0 - jnp.arange(R_HEADS, dtype=F32)))
    p = jnp.arange(l, dtype=F32)
    causal = jnp.tril(jnp.ones((l, l), dtype=bool))
    rel = jnp.where(causal, p[:, None] - p[None, :], 0.0)
    decay = jnp.where(causal, jnp.exp(rel[None] * log_gamma[:, None, None]), 0.0)
    zeta = jnp.exp((l - 1.0 - p)[None, :] * log_gamma[:, None])
    xi = jnp.exp((p + 1.0)[None, :] * log_gamma[:, None])
    cdec = jnp.exp(l * log_gamma)
    npair = R_HEADS // 2
    decay_ab = decay.reshape(npair, 2, l, l).transpose(0, 2, 1, 3).reshape(npair, l, 2 * l)
    by_lane = lambda a: jnp.repeat(a.reshape(npair, 2, -1), rd, axis=1)
    xi_ab = by_lane(xi).transpose(0, 2, 1)
    zeta_k = by_lane(zeta).transpose(0, 2, 1)
    zeta_tile = jnp.tile(zeta_k.transpose(1, 0, 2).reshape(l, npair * LANES), (TOKEN_TILE // l, 1))
    cdt = jnp.broadcast_to(by_lane(cdec).transpose(0, 2, 1), (npair, LANES, LANES))
    head = np.arange(LANES) // rd
    bmaskt = jnp.asarray((head[:, None] == head[None, :]).astype(np.float32))
    return {"cos": cos, "sin": sin, "decay": decay_ab, "xi": xi_ab, "zeta_tile": zeta_tile, "cdt": cdt,
            "bmaskt": bmaskt}


def _block_diag_padded(w, lay):
    depth, nb = w.shape[0], w.shape[1]
    rows = w.reshape(depth, nb * M_BLOCK, M_BLOCK)
    tiled = jnp.tile(rows, (1, 1, lay.mpad // M_BLOCK))
    r = np.arange(nb * M_BLOCK)[:, None]
    c = np.arange(lay.mpad)[None, :]
    packed_c = (c // LANES) * lay.md + c % LANES
    keep = (c % LANES < lay.md) & (r // M_BLOCK == packed_c // M_BLOCK)
    return jnp.where(jnp.asarray(keep), tiled, 0.0).astype(BF16)


def _mixer_weights(lay, p):
    depth = p["m_wq"].shape[0]
    gbi = jnp.concatenate([p["m_i_bias"], jnp.zeros((depth, LANES - M_HEADS), F32)], axis=-1)
    gbf = jnp.concatenate(
        [p["m_f_bias"], p["m_f_bias"], jnp.zeros((depth, LANES - 2 * M_HEADS), F32)], axis=-1)
    causal = jnp.tril(jnp.ones((CHUNK, CHUNK), dtype=bool))
    wm = jnp.where(causal, p["g_ws"], 0.0)
    wcat = wm.transpose(0, 2, 1, 3).reshape(depth, CHUNK, G_GROUPS * CHUNK)
    gbmat = jnp.repeat(p["g_bs"].transpose(0, 2, 1), lay.gw // G_GROUPS, axis=2)
    row = lambda a: a[:, None, :]
    return {
        "convw": p["m_conv_w"], "convb": row(p["m_conv_b"]),
        "bdq": _block_diag_padded(p["m_wq"], lay),
        "bdk": _block_diag_padded(p["m_wk"], lay),
        "bdvt": _block_diag_padded(p["m_wv"], lay).transpose(0, 2, 1),
        "gbi": row(gbi), "gbf": row(gbf),
        "mnormt": jnp.broadcast_to(p["m_norm_g"][:, :, None], (depth, lay.mw, LANES)),
        "mskip": row(p["m_skip"]),
        "rnorm": row(p["r_norm_g"]),
        "gnorm": row(p["g_norm_g"]),
        "wcat": wcat.astype(BF16), "gbmat": gbmat,
    }


def kernel(x, norm_mix_g, w_in, m_conv_w, m_conv_b, m_wq, m_wk, m_wv, m_i_bias, m_f_bias, m_norm_g,
           m_skip, r_norm_g, g_norm_g, g_ws, g_bs, w_out, norm_ff_g, w_ff1, w_ff2, final_norm_g):
    batch, seq, d = x.shape
    depth = w_in.shape[0]
    lay = _Layout(d)
    assert seq % MIX_TILE == 0 and MIX_TILE % CHUNK == 0 and batch % MIX_ROWS == 0
    assert seq % TOKEN_TILE == 0 and TOKEN_TILE % CHUNK == 0 and w_ff1.shape[2] % FF_CHUNK == 0
    assert (batch * seq) % FFN_TILE == 0
    params = dict(m_conv_w=m_conv_w, m_conv_b=m_conv_b, m_wq=m_wq, m_wk=m_wk, m_wv=m_wv,
                  m_i_bias=m_i_bias, m_f_bias=m_f_bias, m_norm_g=m_norm_g, m_skip=m_skip,
                  r_norm_g=r_norm_g, g_norm_g=g_norm_g, g_ws=g_ws, g_bs=g_bs)
    tabs = _tables(lay, seq)
    lw = _mixer_weights(lay, params)
    dense = {
        "w_in": lay.relayout_in_proj(w_in),
        "w_out": w_out.astype(BF16), "w1": w_ff1.astype(BF16), "w2": w_ff2.astype(BF16),
        "g_mix": norm_mix_g[:, None, :], "g_ff": norm_ff_g[:, None, :], "g_final": final_norm_g[None, :],
    }
    x2 = x.reshape(batch * seq, d)
    for layer in range(depth):
        act, gates, vt = _inproj(x2, tabs, lw, dense, layer, lay, seq)
        mix = _mixer(act.reshape(batch, seq, lay.na), gates.reshape(batch, seq, 2 * LANES),
                     vt.reshape(batch, seq // CHUNK, lay.mpad, CHUNK), tabs, lw, layer, lay, batch, seq)
        x2 = _ffn(mix.reshape(batch * seq, lay.dmix), x2, dense, layer, final=(layer == depth - 1))
    return x2.reshape(batch, seq, d)
```

```python
import functools

import numpy as np
import jax
import jax.numpy as jnp
from jax import lax
from jax.experimental import pallas as pl
from jax.experimental.pallas import tpu as pltpu

M_HEADS = 4
M_CONV = 4
M_BLOCK = 4
R_HEADS = 6
G_GROUPS = 4
CHUNK = 128
ROPE_THETA = 10000.0
EPS = 1e-6

LANES = 128
V7X_VMEM_BYTES = 64 * 1024 * 1024

TOKEN_TILE = 512
FFN_TILE = 1024
MIX_TILE = 512
MIX_ROWS = 2
FF_CHUNK = 1024

F32 = jnp.float32
BF16 = jnp.bfloat16


def _rms(x, g):
    return x * lax.rsqrt(jnp.mean(jnp.square(x), axis=-1, keepdims=True) + EPS) * g


def _silu(x):
    return x * jax.nn.sigmoid(x)


def _dot(a, b):
    return jnp.dot(a, b, preferred_element_type=F32)


def _dot_nt(a, b):
    return lax.dot_general(a, b, (((1,), (1,)), ((), ())), preferred_element_type=F32)


def _dot_tn(a, b):
    return lax.dot_general(a, b, (((0,), (0,)), ((), ())), preferred_element_type=F32)


def _nbytes(shape, dtype):
    return int(np.prod(shape)) * jnp.dtype(dtype).itemsize


def _vmem_limit(nbytes):
    return int(min(nbytes + (12 << 20), V7X_VMEM_BYTES - (4 << 20)))


def _layer_slab(a, layer, **kw):
    return pl.BlockSpec((None,) + a.shape[1:], lambda *g: (layer,) + (0,) * (a.ndim - 1), **kw)


def _whole(a):
    return pl.BlockSpec(a.shape, lambda *g: (0,) * a.ndim)


_ONCE = dict(pipeline_mode=pl.Buffered(1))


class _Layout:
    def __init__(self, d_model):
        self.mw = 3 * d_model // 8
        self.rw = 3 * d_model // 8
        self.gw = d_model - self.mw - self.rw
        self.md = self.mw // M_HEADS
        self.rd = self.rw // R_HEADS
        assert self.md <= LANES and 2 * self.rd == LANES and self.gw // G_GROUPS * 2 == LANES
        assert self.mw % LANES == 0 and self.rw % LANES == 0 and self.gw % LANES == 0
        self.mpad = M_HEADS * LANES
        self.dmix = self.mw + self.rw + self.gw
        o = 0
        self.mx = o; o += self.mw
        self.mz = o; o += self.mw
        self.rq = o; o += self.rw
        self.rk = o; o += self.rw
        self.rv = o; o += self.rw
        self.rg = o; o += self.rw
        self.gu = o; o += self.gw
        self.gv = o; o += self.gw
        self.gi = o; o += LANES
        self.gf = o; o += LANES
        self.n = o
        o = 0
        self.a_q = o; o += self.mpad
        self.a_k = o; o += self.mpad
        self.a_rq = o; o += self.rw
        self.a_rka = o; o += self.rw
        self.a_rkb = o; o += self.rw
        self.a_kz = o; o += self.rw
        self.a_rva = o; o += self.rw
        self.a_rvb = o; o += self.rw
        self.a_gva = o; o += self.gw
        self.a_gvb = o; o += self.gw
        self.a_zs = o; o += self.mw
        self.a_sxz = o; o += self.mw
        self.a_rgs = o; o += self.rw
        self.a_gug = o; o += self.gw
        self.na = o

    def relayout_in_proj(self, w_in):
        mw, rw, gw = self.mw, self.rw, self.gw
        depth, d, _ = w_in.shape
        o_gate = 2 * mw
        o_rq = o_gate + 2 * M_HEADS
        parts = [w_in[:, :, 0:2 * mw],
                 w_in[:, :, o_rq:o_rq + 4 * rw + 2 * gw],
                 w_in[:, :, o_gate:o_gate + M_HEADS],
                 jnp.zeros((depth, d, LANES - M_HEADS), w_in.dtype),
                 w_in[:, :, o_gate + M_HEADS:o_gate + 2 * M_HEADS],
                 w_in[:, :, o_gate + M_HEADS:o_gate + 2 * M_HEADS],
                 jnp.zeros((depth, d, LANES - 2 * M_HEADS), w_in.dtype)]
        return jnp.concatenate(parts, axis=-1).astype(BF16)


_INPROJ_PARAMS = ("convw", "convb", "bdq", "bdk", "bdvt", "gbi", "gbf", "mskip", "gnorm")


def _relayout_in_proj_rows(v, lay):
    rows = v.shape[0]
    o_gate = 2 * lay.mw
    o_rest = o_gate + 2 * M_HEADS
    gi = v[:, o_gate:o_gate + M_HEADS]
    gf = v[:, o_gate + M_HEADS:o_gate + 2 * M_HEADS]
    parts = [v[:, 0:o_gate], v[:, o_rest:o_rest + 4 * lay.rw + 2 * lay.gw],
             gi, jnp.zeros((rows, LANES - M_HEADS), v.dtype),
             gf, gf, jnp.zeros((rows, LANES - 2 * M_HEADS), v.dtype)]
    return jnp.concatenate(parts, axis=-1).astype(BF16)


def _inproj_kernel(*refs, lay, nj, has_next):
    n_p = len(_INPROJ_PARAMS)
    n_w = 4 if has_next else 3
    x_ref, g_ref, w_ref, cos_ref, sin_ref, zeta_ref = refs[0:6]
    mp = dict(zip(_INPROJ_PARAMS, refs[6:6 + n_p]))
    wf = refs[6 + n_p:6 + n_p + n_w]
    act_ref, gates_ref, vt_ref = refs[6 + n_p + n_w:9 + n_p + n_w]
    wb = refs[9 + n_p + n_w:9 + n_p + 2 * n_w]
    xbuf = refs[9 + n_p + 2 * n_w]
    tm = x_ref.shape[0]
    for src, dst in zip(wf[0:3], wb[0:3]):
        dst[...] = src[...].astype(dst.dtype)
    if has_next:
        wb[3][...] = _relayout_in_proj_rows(wf[3][...], lay)

    md, mw, rw, gw = lay.md, lay.mw, lay.rw, lay.gw
    first_of_seq = lax.rem(pl.program_id(0), nj) == 0

    @pl.when(first_of_seq)
    def _():
        xbuf[0:8, :] = jnp.zeros((8, mw), F32)

    @pl.when(jnp.logical_not(first_of_seq))
    def _():
        xbuf[0:8, :] = xbuf[tm:tm + 8, :]

    proj = _dot(_rms(x_ref[...], g_ref[...]).astype(BF16), w_ref[...])

    def put(col, val):
        act_ref[:, col:col + val.shape[1]] = val.astype(act_ref.dtype)

    mx = proj[:, lay.mx:lay.mx + mw]
    xbuf[8:tm + 8, :] = mx
    conv = mp["convb"][...] + mp["convw"][M_CONV - 1:M_CONV, :] * mx
    for t in range(M_CONV - 1):
        conv = conv + mp["convw"][t:t + 1, :] * xbuf[8 - (M_CONV - 1) + t:8 - (M_CONV - 1) + t + tm, :]
    xc = _silu(conv)
    xcb = xc.astype(BF16)
    put(lay.a_q, _dot(xcb, mp["bdq"][...]))
    put(lay.a_k, _dot(xcb, mp["bdk"][...]) * (md ** -0.5))
    row_p = lax.broadcasted_iota(jnp.int32, (lay.mpad, 1), 0) % LANES
    vt = _dot_nt(mp["bdvt"][...], mx.astype(BF16)) + (row_p == md).astype(F32)
    for c in range(tm // CHUNK):
        vt_ref[c] = vt[:, c * CHUNK:(c + 1) * CHUNK]
    zs = _silu(proj[:, lay.mz:lay.mz + mw])
    put(lay.a_zs, zs)
    put(lay.a_sxz, mp["mskip"][...] * xc * zs)

    li = lax.broadcasted_iota(jnp.int32, (tm, LANES), 1)
    gates_ref[:, 0:LANES] = proj[:, lay.gi:lay.gi + LANES] + mp["gbi"][...]
    gf = proj[:, lay.gf:lay.gf + LANES] + mp["gbf"][...]
    gates_ref[:, LANES:2 * LANES] = jnp.where(li < 2 * M_HEADS, jax.nn.log_sigmoid(gf), 0.0)

    lo_half = li < (LANES // 2)
    half = lay.rd // 2
    first_half = (li % lay.rd) < half
    cs = cos_ref[...]
    sn = sin_ref[...]

    def rotary(v):
        partner = jnp.where(first_half, pltpu.roll(v, LANES - half, axis=1), pltpu.roll(v, half, axis=1))
        return v * cs + partner * sn

    for p_i in range(rw // LANES):
        off = p_i * LANES
        rq = proj[:, lay.rq + off:lay.rq + off + LANES]
        rk = proj[:, lay.rk + off:lay.rk + off + LANES]
        rv = proj[:, lay.rv + off:lay.rv + off + LANES]
        put(lay.a_rq + off, rotary(rq))
        kr = rotary(rk) * (lay.rd ** -0.5)
        put(lay.a_rka + off, jnp.where(lo_half, kr, 0.0))
        put(lay.a_rkb + off, jnp.where(lo_half, 0.0, kr))
        put(lay.a_kz + off, kr * zeta_ref[:, off:off + LANES])
        put(lay.a_rva + off, jnp.where(lo_half, rv, 0.0))
        put(lay.a_rvb + off, jnp.where(lo_half, 0.0, rv))
    put(lay.a_rgs, _silu(proj[:, lay.rg:lay.rg + rw]))

    put(lay.a_gug, jax.nn.gelu(proj[:, lay.gu:lay.gu + gw]))
    gv = jax.nn.gelu(proj[:, lay.gv:lay.gv + gw])
    gvn = gv * lax.rsqrt(jnp.mean(gv * gv, axis=-1, keepdims=True) + EPS) * mp["gnorm"][...]
    for q_i in range(gw // LANES):
        off = q_i * LANES
        put(lay.a_gva + off, jnp.where(lo_half, gvn[:, off:off + LANES], 0.0))
        put(lay.a_gvb + off, jnp.where(lo_half, 0.0, gvn[:, off:off + LANES]))


def _inproj(x2, w_in_b, tabs, lw, dense, layer, lay, seq):
    t, d = x2.shape
    tm = TOKEN_TILE
    nj = seq // tm
    steps = t // tm
    mps = [lw[k] for k in _INPROJ_PARAMS]
    pos = lambda i: (lax.rem(i, nj), 0)
    side = [(dense["w_out"], layer), (dense["w_ff1"], layer), (dense["w_ff2"], layer)]
    has_next = layer + 1 < dense["w_in"].shape[0]
    if has_next:
        side.append((dense["w_in"], layer + 1))
    side_in, side_out, side_shape = [], [], []
    for a, l in side:
        rows, cols = a.shape[1], a.shape[2]
        assert rows % steps == 0 and (rows // steps) % 16 == 0
        side_in.append(pl.BlockSpec((None, rows // steps, cols), lambda i, l=l: (l, i, 0)))
        out_cols = lay.n if a is dense["w_in"] else cols
        side_out.append(pl.BlockSpec((rows // steps, out_cols), lambda i: (i, 0)))
        side_shape.append(jax.ShapeDtypeStruct((rows, out_cols), BF16))
    est = (2 * tm * d * 4 + d * lay.n * 2 + 2 * tm * lay.na * 2 + 2 * tm * 2 * LANES * 4
           + 2 * tm * lay.mpad * 4 + 6 * tm * LANES * 4 + (tm + 8) * lay.mw * 4 + tm * lay.n * 4
           + 2 * sum(_nbytes(a.shape[1:], a.dtype) for a in mps)
           + 3 * sum(_nbytes(a.shape[1:], a.dtype) // steps for a, _ in side))
    outs = pl.pallas_call(
        functools.partial(_inproj_kernel, lay=lay, nj=nj, has_next=has_next),
        grid=(steps,),
        in_specs=([pl.BlockSpec((tm, d), lambda i: (i, 0)),
                   _layer_slab(dense["g_mix"], layer),
                   pl.BlockSpec(w_in_b.shape, lambda i: (0, 0), **_ONCE),
                   pl.BlockSpec((tm, LANES), pos),
                   pl.BlockSpec((tm, LANES), pos),
                   _whole(tabs["zeta_tile"])]
                  + [_layer_slab(a, layer) for a in mps] + side_in),
        out_specs=[pl.BlockSpec((tm, lay.na), lambda i: (i, 0)),
                   pl.BlockSpec((tm, 2 * LANES), lambda i: (i, 0)),
                   pl.BlockSpec((tm // CHUNK, lay.mpad, CHUNK), lambda i: (i, 0, 0))] + side_out,
        out_shape=[jax.ShapeDtypeStruct((t, lay.na), BF16),
                   jax.ShapeDtypeStruct((t, 2 * LANES), F32),
                   jax.ShapeDtypeStruct((t // CHUNK, lay.mpad, CHUNK), F32)] + side_shape,
        scratch_shapes=[pltpu.VMEM((tm + 8, lay.mw), F32)],
        compiler_params=pltpu.CompilerParams(
            dimension_semantics=("arbitrary",), vmem_limit_bytes=_vmem_limit(est)),
        name="inproj",
    )(x2, dense["g_mix"], w_in_b, tabs["cos"], tabs["sin"], tabs["zeta_tile"], *mps, *[a for a, _ in side])
    return outs[0:3], outs[3:6], (outs[6] if has_next else None)


_MIXER_PARAMS = ("mnormt", "rnorm", "wcat", "gbmat")
_TABLES = ("decay", "xi", "cdt", "bmaskt")


def _block_diag2(x, y):
    z = jnp.zeros_like(x)
    return jnp.concatenate([jnp.concatenate([x, z], axis=1), jnp.concatenate([z, y], axis=1)], axis=0)


def _mixer_chunk(c, act, gates, vt, mp, tb, out, c_st, r_st, m_st, lay):
    md, mw, rw, gw = lay.md, lay.mw, lay.rw, lay.gw
    li = lax.broadcasted_iota(jnp.int32, (CHUNK, LANES), 1)
    ri = lax.broadcasted_iota(jnp.int32, (CHUNK, LANES), 0)
    causal_t = ri <= li
    tril = (li <= ri).astype(BF16)
    lo_half = li < (LANES // 2)
    rows = pl.ds(pl.multiple_of(c * CHUNK, CHUNK), CHUNK)

    def a(col, width=LANES):
        return act[rows, col:col + width]

    gi = gates[rows, 0:LANES]
    logf = gates[rows, LANES:2 * LANES]
    a1 = logf.astype(BF16)
    r1 = logf - a1.astype(F32)
    a2 = r1.astype(BF16)
    a3 = (r1 - a2.astype(F32)).astype(BF16)
    b3 = _dot(tril, jnp.concatenate([a1, a2, a3], axis=1))
    bcum = (b3[:, 2 * LANES:3 * LANES] + b3[:, LANES:2 * LANES]) + b3[:, 0:LANES]
    wb = jnp.where(li < M_HEADS, gi - bcum, bcum)
    wb_t = wb.T

    causal_t2 = jnp.concatenate([causal_t, causal_t], axis=1)
    hn_t = []
    for hp in range(M_HEADS // 2):
        heads = (2 * hp, 2 * hp + 1)
        pair = lambda f: jnp.concatenate([f(h) for h in heads], axis=1)
        rep = lambda v: jnp.broadcast_to(v, (1, LANES))
        q2 = a(lay.a_q + 2 * hp * LANES, 2 * LANES)
        k2 = a(lay.a_k + 2 * hp * LANES, 2 * LANES)
        vt2 = pair(lambda h: vt[c, h * LANES:(h + 1) * LANES, :])
        w_row = pair(lambda h: wb_t[h:h + 1, :])
        b_row = pair(lambda h: wb_t[M_HEADS + h:M_HEADS + h + 1, :])
        b_end = pair(lambda h: rep(wb_t[M_HEADS + h:M_HEADS + h + 1, CHUNK - 1:CHUNK]))
        w_end = b_end + w_row
        a_c = pair(lambda h: rep(jnp.max(w_end[:, (h % 2) * LANES:(h % 2 + 1) * LANES], axis=-1, keepdims=True)))
        e_row = jnp.exp(w_end - a_c)
        m_prev = m_st[hp:hp + 1, :]
        m_new = jnp.maximum(b_end + m_prev, a_c)
        s_old = jnp.exp(b_end + m_prev - m_new)
        s_new = jnp.exp(a_c - m_new)
        m_st[hp:hp + 1, :] = m_new

        il = b_row + m_prev
        dm = jnp.where(causal_t2, pair(lambda h: jnp.broadcast_to(wb[:, h:h + 1], (CHUNK, LANES))) + b_row,
                       -jnp.inf)
        m_t = jnp.maximum(il, jnp.max(dm, axis=0, keepdims=True))
        s_inter = jnp.exp(il - m_t)
        c_prev = c_st[hp]
        kq = _dot_nt(jnp.concatenate([k2, c_prev.astype(BF16)], axis=0),
                     _block_diag2(q2[:, 0:LANES], q2[:, LANES:2 * LANES]))
        p_t = kq[0:CHUNK] * jnp.exp(dm - m_t)
        inter = s_inter * kq[CHUNK:2 * CHUNK]
        p_b = p_t.astype(BF16)
        tot = _dot(vt2.astype(BF16), _block_diag2(p_b[:, 0:LANES], p_b[:, LANES:2 * LANES])) + inter
        den = jnp.sum(p_t, axis=0, keepdims=True) + inter[md:md + 1, :]
        rden = 1.0 / jnp.maximum(jnp.abs(den), jnp.exp(-m_t))
        hout = tot[0:md, :] * rden
        ms = jnp.sum(hout * hout, axis=0, keepdims=True) * (1.0 / md)
        hn = hout * lax.rsqrt(ms + EPS)
        for i, h in enumerate(heads):
            hn_t.append(hn[:, i * LANES:(i + 1) * LANES] * mp["mnormt"][h * md:(h + 1) * md, :])
        c_st[hp] = s_old * c_prev + s_new * _dot((vt2 * e_row).astype(BF16),
                                                 _block_diag2(k2[:, 0:LANES], k2[:, LANES:2 * LANES]))

    hm_t = jnp.concatenate(hn_t, axis=0)
    hm = jnp.concatenate([hm_t[t * LANES:(t + 1) * LANES, :].T for t in range(mw // LANES)], axis=1)
    out[rows, 0:mw] = (hm * a(lay.a_zs, mw).astype(F32) + a(lay.a_sxz, mw).astype(F32)).astype(out.dtype)

    for p_i in range(rw // LANES):
        off = p_i * LANES
        qb = a(lay.a_rq + off)
        rva, rvb = a(lay.a_rva + off), a(lay.a_rvb + off)
        r_prev = r_st[p_i]
        sqr = _dot_nt(qb, jnp.concatenate([a(lay.a_rka + off), a(lay.a_rkb + off), r_prev.astype(BF16)],
                                          axis=0))
        p_ab = (sqr[:, 0:2 * LANES] * tb["decay"][p_i]).astype(BF16)
        hr = _dot(p_ab, jnp.concatenate([rva, rvb], axis=0)) + sqr[:, 2 * LANES:3 * LANES] * tb["xi"][p_i]
        sq = hr * hr
        ms_a = jnp.sum(jnp.where(lo_half, sq, 0.0), axis=-1, keepdims=True)
        ms_b = jnp.sum(jnp.where(lo_half, 0.0, sq), axis=-1, keepdims=True)
        ms = jnp.where(lo_half, ms_a, ms_b) * (1.0 / lay.rd)
        hn = hr * lax.rsqrt(ms + EPS) * mp["rnorm"][:, off:off + LANES]
        out[rows, mw + off:mw + off + LANES] = (hn * a(lay.a_rgs + off).astype(F32)).astype(out.dtype)
        r_st[p_i] = tb["cdt"][p_i] * r_prev + tb["bmaskt"][...] * _dot_tn(rva + rvb, a(lay.a_kz + off))

    zero = jnp.zeros((2 * CHUNK, LANES), BF16)
    vst = [jnp.concatenate([a(lay.a_gva + q_i * LANES), a(lay.a_gvb + q_i * LANES)], axis=0)
           for q_i in range(gw // LANES)]
    assert len(vst) == 2
    mixed = _dot(mp["wcat"][...], jnp.concatenate([jnp.concatenate([vst[0], zero], axis=1),
                                                   jnp.concatenate([zero, vst[1]], axis=1)], axis=0))
    out[rows, mw + rw:mw + rw + gw] = (a(lay.a_gug, gw).astype(F32) * (mixed + mp["gbmat"][...])).astype(out.dtype)


def _mixer_kernel(*refs, lay):
    n_mp, n_tb = len(_MIXER_PARAMS), len(_TABLES)
    act_ref, gates_ref, vt_ref = refs[0:3]
    mp = dict(zip(_MIXER_PARAMS, refs[3:3 + n_mp]))
    tb = dict(zip(_TABLES, refs[3 + n_mp:3 + n_mp + n_tb]))
    o_ref = refs[3 + n_mp + n_tb]
    c_st, r_st, m_st = refs[4 + n_mp + n_tb:]
    nrow, ts = act_ref.shape[0], act_ref.shape[1]

    @pl.when(pl.program_id(1) == 0)
    def _():
        c_st[...] = jnp.zeros_like(c_st)
        r_st[...] = jnp.zeros_like(r_st)
        m_st[...] = jnp.zeros_like(m_st)

    def step(c, carry):
        for r in range(nrow):
            _mixer_chunk(c, act_ref.at[r], gates_ref.at[r], vt_ref.at[r], mp, tb, o_ref.at[r],
                         c_st.at[r], r_st.at[r], m_st.at[r], lay)
        return carry

    lax.fori_loop(0, ts // CHUNK, step, 0)


def _mixer(act, gates, vt, tabs, lw, layer, lay, batch, seq):
    ts, nrow = MIX_TILE, MIX_ROWS
    mps = [lw[k] for k in _MIXER_PARAMS]
    tbs = [tabs[k] for k in _TABLES]
    scratch = [pltpu.VMEM((nrow, M_HEADS // 2, LANES, 2 * LANES), F32),
               pltpu.VMEM((nrow, lay.rw // LANES, LANES, LANES), F32),
               pltpu.VMEM((nrow, 8, 2 * LANES), F32)]
    est = (2 * nrow * ts * (lay.na * 2 + 2 * LANES * 4 + lay.mpad * 4 + lay.dmix * 2)
           + 2 * sum(_nbytes(a.shape[1:], a.dtype) for a in mps) + 2 * sum(_nbytes(a.shape, a.dtype) for a in tbs)
           + sum(_nbytes(s.shape, s.dtype) for s in scratch))
    return pl.pallas_call(
        functools.partial(_mixer_kernel, lay=lay),
        grid=(batch // nrow, seq // ts),
        in_specs=([pl.BlockSpec((nrow, ts, lay.na), lambda b, j: (b, j, 0)),
                   pl.BlockSpec((nrow, ts, 2 * LANES), lambda b, j: (b, j, 0)),
                   pl.BlockSpec((nrow, ts // CHUNK, lay.mpad, CHUNK), lambda b, j: (b, j, 0, 0))]
                  + [_layer_slab(a, layer) for a in mps] + [_whole(a) for a in tbs]),
        out_specs=pl.BlockSpec((nrow, ts, lay.dmix), lambda b, j: (b, j, 0)),
        out_shape=jax.ShapeDtypeStruct((batch, seq, lay.dmix), BF16),
        scratch_shapes=scratch,
        compiler_params=pltpu.CompilerParams(
            dimension_semantics=("arbitrary", "arbitrary"), vmem_limit_bytes=_vmem_limit(est)),
        name="mixer",
    )(act, gates, vt, *mps, *tbs)


def _ffn_kernel(mix_ref, x_ref, wo_ref, g_ref, w1_ref, w2_ref, gf_ref, o_ref, *, final):
    x1 = x_ref[...] + _dot(mix_ref[...], wo_ref[...])
    h2 = _rms(x1, g_ref[...]).astype(BF16)
    acc = x1
    for j in range(w1_ref.shape[1] // FF_CHUNK):
        cols = slice(j * FF_CHUNK, (j + 1) * FF_CHUNK)
        a = jnp.square(jnp.maximum(_dot(h2, w1_ref[:, cols]), 0.0)).astype(BF16)
        acc = acc + _dot(a, w2_ref[cols, :])
    if final:
        acc = _rms(acc, gf_ref[...])
    o_ref[...] = acc


def _ffn(mix, x2, wts, dense, layer, final):
    t, d = x2.shape
    wo, w1, w2 = wts
    dff = w1.shape[1]
    tm = FFN_TILE
    est = (2 * tm * d * 2 + 4 * tm * d * 4 + (d * d + 2 * d * dff) * 2
           + tm * FF_CHUNK * 6 + 3 * tm * d * 4)
    resident = lambda a: pl.BlockSpec(a.shape, lambda i: (0, 0), **_ONCE)
    return pl.pallas_call(
        functools.partial(_ffn_kernel, final=final),
        grid=(t // tm,),
        in_specs=[pl.BlockSpec((tm, d), lambda i: (i, 0)),
                  pl.BlockSpec((tm, d), lambda i: (i, 0)),
                  resident(wo),
                  _layer_slab(dense["g_ff"], layer),
                  resident(w1),
                  resident(w2),
                  pl.BlockSpec((1, d), lambda i: (0, 0))],
        out_specs=pl.BlockSpec((tm, d), lambda i: (i, 0)),
        out_shape=jax.ShapeDtypeStruct((t, d), F32),
        compiler_params=pltpu.CompilerParams(
            dimension_semantics=("arbitrary",), vmem_limit_bytes=_vmem_limit(est)),
        name="ffn_final" if final else "ffn",
    )(mix, x2, wo, dense["g_ff"], w1, w2, dense["g_final"])


def _tables(lay, seq):
    rd, half = lay.rd, lay.rd // 2
    l = CHUNK
    pos = jnp.arange(seq, dtype=F32)
    freqs = ROPE_THETA ** (-jnp.arange(half, dtype=F32) / half)
    ang = pos[:, None] * freqs[None, :]
    sign = jnp.asarray(np.where(np.arange(LANES) % rd < half, -1.0, 1.0), F32)
    cos = jnp.tile(jnp.cos(ang), (1, LANES // half))
    sin = jnp.tile(jnp.sin(ang), (1, LANES // half)) * sign[None, :]
    log_gamma = jnp.log(1.0 - 2.0 ** (-5.0 - jnp.arange(R_HEADS, dtype=F32)))
    p = jnp.arange(l, dtype=F32)
    causal = jnp.tril(jnp.ones((l, l), dtype=bool))
    rel = jnp.where(causal, p[:, None] - p[None, :], 0.0)
    decay = jnp.where(causal, jnp.exp(rel[None] * log_gamma[:, None, None]), 0.0)
    zeta = jnp.exp((l - 1.0 - p)[None, :] * log_gamma[:, None])
    xi = jnp.exp((p + 1.0)[None, :] * log_gamma[:, None])
    cdec = jnp.exp(l * log_gamma)
    npair = R_HEADS // 2
    decay_ab = decay.reshape(npair, 2, l, l).transpose(0, 2, 1, 3).reshape(npair, l, 2 * l)
    by_lane = lambda a: jnp.repeat(a.reshape(npair, 2, -1), rd, axis=1)
    xi_ab = by_lane(xi).transpose(0, 2, 1)
    zeta_k = by_lane(zeta).transpose(0, 2, 1)
    zeta_tile = jnp.tile(zeta_k.transpose(1, 0, 2).reshape(l, npair * LANES), (TOKEN_TILE // l, 1))
    cdt = jnp.broadcast_to(by_lane(cdec).transpose(0, 2, 1), (npair, LANES, LANES))
    head = np.arange(LANES) // rd
    bmaskt = jnp.asarray((head[:, None] == head[None, :]).astype(np.float32))
    return {"cos": cos, "sin": sin, "decay": decay_ab, "xi": xi_ab, "zeta_tile": zeta_tile, "cdt": cdt,
            "bmaskt": bmaskt}


def _block_diag_padded(w, lay):
    depth, nb = w.shape[0], w.shape[1]
    rows = w.reshape(depth, nb * M_BLOCK, M_BLOCK)
    tiled = jnp.tile(rows, (1, 1, lay.mpad // M_BLOCK))
    r = np.arange(nb * M_BLOCK)[:, None]
    c = np.arange(lay.mpad)[None, :]
    packed_c = (c // LANES) * lay.md + c % LANES
    keep = (c % LANES < lay.md) & (r // M_BLOCK == packed_c // M_BLOCK)
    return jnp.where(jnp.asarray(keep), tiled, 0.0).astype(BF16)


def _mixer_weights(lay, p):
    depth = p["m_wq"].shape[0]
    gbi = jnp.concatenate([p["m_i_bias"], jnp.zeros((depth, LANES - M_HEADS), F32)], axis=-1)
    gbf = jnp.concatenate(
        [p["m_f_bias"], p["m_f_bias"], jnp.zeros((depth, LANES - 2 * M_HEADS), F32)], axis=-1)
    causal = jnp.tril(jnp.ones((CHUNK, CHUNK), dtype=bool))
    wm = jnp.where(causal, p["g_ws"], 0.0)
    wcat = wm.transpose(0, 2, 1, 3).reshape(depth, CHUNK, G_GROUPS * CHUNK)
    gbmat = jnp.repeat(p["g_bs"].transpose(0, 2, 1), lay.gw // G_GROUPS, axis=2)
    row = lambda a: a[:, None, :]
    return {
        "convw": p["m_conv_w"], "convb": row(p["m_conv_b"]),
        "bdq": _block_diag_padded(p["m_wq"], lay),
        "bdk": _block_diag_padded(p["m_wk"], lay),
        "bdvt": _block_diag_padded(p["m_wv"], lay).transpose(0, 2, 1),
        "gbi": row(gbi), "gbf": row(gbf),
        "mnormt": jnp.broadcast_to(p["m_norm_g"][:, :, None], (depth, lay.mw, LANES)),
        "mskip": row(p["m_skip"]),
        "rnorm": row(p["r_norm_g"]),
        "gnorm": row(p["g_norm_g"]),
        "wcat": wcat.astype(BF16), "gbmat": gbmat,
    }


def kernel(x, norm_mix_g, w_in, m_conv_w, m_conv_b, m_wq, m_wk, m_wv, m_i_bias, m_f_bias, m_norm_g,
           m_skip, r_norm_g, g_norm_g, g_ws, g_bs, w_out, norm_ff_g, w_ff1, w_ff2, final_norm_g):
    batch, seq, d = x.shape
    depth = w_in.shape[0]
    lay = _Layout(d)
    assert seq % MIX_TILE == 0 and MIX_TILE % CHUNK == 0 and batch % MIX_ROWS == 0
    assert seq % TOKEN_TILE == 0 and TOKEN_TILE % CHUNK == 0 and w_ff1.shape[2] % FF_CHUNK == 0
    assert (batch * seq) % FFN_TILE == 0
    params = dict(m_conv_w=m_conv_w, m_conv_b=m_conv_b, m_wq=m_wq, m_wk=m_wk, m_wv=m_wv,
                  m_i_bias=m_i_bias, m_f_bias=m_f_bias, m_norm_g=m_norm_g, m_skip=m_skip,
                  r_norm_g=r_norm_g, g_norm_g=g_norm_g, g_ws=g_ws, g_bs=g_bs)
    tabs = _tables(lay, seq)
    lw = _mixer_weights(lay, params)
    dense = {
        "w_in": w_in, "w_out": w_out, "w_ff1": w_ff1, "w_ff2": w_ff2,
        "g_mix": norm_mix_g[:, None, :], "g_ff": norm_ff_g[:, None, :], "g_final": final_norm_g[None, :],
    }
    w_in_b = lay.relayout_in_proj(w_in[0:1])[0]
    x2 = x.reshape(batch * seq, d)
    for layer in range(depth):
        (act, gates, vt), ffn_w, w_in_b = _inproj(x2, w_in_b, tabs, lw, dense, layer, lay, seq)
        mix = _mixer(act.reshape(batch, seq, lay.na), gates.reshape(batch, seq, 2 * LANES),
                     vt.reshape(batch, seq // CHUNK, lay.mpad, CHUNK), tabs, lw, layer, lay, batch, seq)
        x2 = _ffn(mix.reshape(batch * seq, lay.dmix), x2, ffn_w, dense, layer, final=(layer == depth - 1))
    return x2.reshape(batch, seq, d)
```

```python
import functools

import numpy as np
import jax
import jax.numpy as jnp
from jax import lax
from jax.experimental import pallas as pl
from jax.experimental.pallas import tpu as pltpu

M_HEADS = 4
M_CONV = 4
M_BLOCK = 4
R_HEADS = 6
G_GROUPS = 4
CHUNK = 128
ROPE_THETA = 10000.0
EPS = 1e-6

LANES = 128
V7X_VMEM_BYTES = 64 * 1024 * 1024

TOKEN_TILE = 512
FFN_TILE = 1024
MIX_TILE = 512
MIX_ROWS = 2
FF_CHUNK = 1024

F32 = jnp.float32
BF16 = jnp.bfloat16


def _rms(x, g):
    return x * lax.rsqrt(jnp.mean(jnp.square(x), axis=-1, keepdims=True) + EPS) * g


def _silu(x):
    return x * jax.nn.sigmoid(x)


def _dot(a, b):
    return jnp.dot(a, b, preferred_element_type=F32)


def _dot_nt(a, b):
    return lax.dot_general(a, b, (((1,), (1,)), ((), ())), preferred_element_type=F32)


def _dot_tn(a, b):
    return lax.dot_general(a, b, (((0,), (0,)), ((), ())), preferred_element_type=F32)


def _nbytes(shape, dtype):
    return int(np.prod(shape)) * jnp.dtype(dtype).itemsize


def _vmem_limit(nbytes):
    return int(min(nbytes + (12 << 20), V7X_VMEM_BYTES - (4 << 20)))


def _layer_slab(a, layer, **kw):
    return pl.BlockSpec((None,) + a.shape[1:], lambda *g: (layer,) + (0,) * (a.ndim - 1), **kw)


def _whole(a):
    return pl.BlockSpec(a.shape, lambda *g: (0,) * a.ndim)


_ONCE = dict(pipeline_mode=pl.Buffered(1))


class _Layout:
    def __init__(self, d_model):
        self.mw = 3 * d_model // 8
        self.rw = 3 * d_model // 8
        self.gw = d_model - self.mw - self.rw
        self.md = self.mw // M_HEADS
        self.rd = self.rw // R_HEADS
        assert self.md <= LANES and 2 * self.rd == LANES and self.gw // G_GROUPS * 2 == LANES
        assert self.mw % LANES == 0 and self.rw % LANES == 0 and self.gw % LANES == 0
        self.mpad = M_HEADS * LANES
        self.dmix = self.mw + self.rw + self.gw
        o = 0
        self.mx = o; o += self.mw
        self.mz = o; o += self.mw
        self.rq = o; o += self.rw
        self.rk = o; o += self.rw
        self.rv = o; o += self.rw
        self.rg = o; o += self.rw
        self.gu = o; o += self.gw
        self.gv = o; o += self.gw
        self.gi = o; o += LANES
        self.gf = o; o += LANES
        self.n = o
        o = 0
        self.a_q = o; o += self.mpad
        self.a_k = o; o += self.mpad
        self.a_rq = o; o += self.rw
        self.a_rka = o; o += self.rw
        self.a_rkb = o; o += self.rw
        self.a_kz = o; o += self.rw
        self.a_rva = o; o += self.rw
        self.a_rvb = o; o += self.rw
        self.a_gva = o; o += self.gw
        self.a_gvb = o; o += self.gw
        self.a_zs = o; o += self.mw
        self.a_sxz = o; o += self.mw
        self.a_rgs = o; o += self.rw
        self.a_gug = o; o += self.gw
        self.na = o

    def relayout_in_proj_t(self, w_in):
        mw, rw, gw = self.mw, self.rw, self.gw
        depth, d, _ = w_in.shape
        wt = jnp.swapaxes(w_in, 1, 2)
        o_gate = 2 * mw
        o_rq = o_gate + 2 * M_HEADS
        parts = [wt[:, 0:2 * mw],
                 wt[:, o_rq:o_rq + 4 * rw + 2 * gw],
                 wt[:, o_gate:o_gate + M_HEADS],
                 jnp.zeros((depth, LANES - M_HEADS, d), w_in.dtype),
                 wt[:, o_gate + M_HEADS:o_gate + 2 * M_HEADS],
                 wt[:, o_gate + M_HEADS:o_gate + 2 * M_HEADS],
                 jnp.zeros((depth, LANES - 2 * M_HEADS, d), w_in.dtype)]
        return jnp.concatenate(parts, axis=1).astype(BF16)


_INPROJ_PARAMS = ("convw", "convb", "bdq", "bdk", "bdvt", "gbi", "gbf", "mskip", "gnorm")
_N_SIDE = 3


def _inproj_kernel(*refs, lay, nj):
    n_p = len(_INPROJ_PARAMS)
    x_ref, g_ref, w_ref, cos_ref, sin_ref, zeta_ref = refs[0:6]
    mp = dict(zip(_INPROJ_PARAMS, refs[6:6 + n_p]))
    wf = refs[6 + n_p:6 + n_p + _N_SIDE]
    act_ref, gates_ref, vt_ref = refs[6 + n_p + _N_SIDE:9 + n_p + _N_SIDE]
    wb = refs[9 + n_p + _N_SIDE:9 + n_p + 2 * _N_SIDE]
    xbuf = refs[9 + n_p + 2 * _N_SIDE]
    tm = x_ref.shape[0]
    for src, dst in zip(wf, wb):
        dst[...] = src[...].astype(dst.dtype)

    md, mw, rw, gw = lay.md, lay.mw, lay.rw, lay.gw
    first_of_seq = lax.rem(pl.program_id(0), nj) == 0

    @pl.when(first_of_seq)
    def _():
        xbuf[0:8, :] = jnp.zeros((8, mw), F32)

    @pl.when(jnp.logical_not(first_of_seq))
    def _():
        xbuf[0:8, :] = xbuf[tm:tm + 8, :]

    proj = _dot_nt(_rms(x_ref[...], g_ref[...]).astype(BF16), w_ref[...])

    def put(col, val):
        act_ref[:, col:col + val.shape[1]] = val.astype(act_ref.dtype)

    mx = proj[:, lay.mx:lay.mx + mw]
    xbuf[8:tm + 8, :] = mx
    conv = mp["convb"][...] + mp["convw"][M_CONV - 1:M_CONV, :] * mx
    for t in range(M_CONV - 1):
        conv = conv + mp["convw"][t:t + 1, :] * xbuf[8 - (M_CONV - 1) + t:8 - (M_CONV - 1) + t + tm, :]
    xc = _silu(conv)
    xcb = xc.astype(BF16)
    put(lay.a_q, _dot(xcb, mp["bdq"][...]))
    put(lay.a_k, _dot(xcb, mp["bdk"][...]) * (md ** -0.5))
    row_p = lax.broadcasted_iota(jnp.int32, (lay.mpad, 1), 0) % LANES
    vt = _dot_nt(mp["bdvt"][...], mx.astype(BF16)) + (row_p == md).astype(F32)
    for c in range(tm // CHUNK):
        vt_ref[c] = vt[:, c * CHUNK:(c + 1) * CHUNK]
    zs = _silu(proj[:, lay.mz:lay.mz + mw])
    put(lay.a_zs, zs)
    put(lay.a_sxz, mp["mskip"][...] * xc * zs)

    li = lax.broadcasted_iota(jnp.int32, (tm, LANES), 1)
    gates_ref[:, 0:LANES] = proj[:, lay.gi:lay.gi + LANES] + mp["gbi"][...]
    gf = proj[:, lay.gf:lay.gf + LANES] + mp["gbf"][...]
    gates_ref[:, LANES:2 * LANES] = jnp.where(li < 2 * M_HEADS, jax.nn.log_sigmoid(gf), 0.0)

    lo_half = li < (LANES // 2)
    half = lay.rd // 2
    first_half = (li % lay.rd) < half
    cs = cos_ref[...]
    sn = sin_ref[...]

    def rotary(v):
        partner = jnp.where(first_half, pltpu.roll(v, LANES - half, axis=1), pltpu.roll(v, half, axis=1))
        return v * cs + partner * sn

    for p_i in range(rw // LANES):
        off = p_i * LANES
        rq = proj[:, lay.rq + off:lay.rq + off + LANES]
        rk = proj[:, lay.rk + off:lay.rk + off + LANES]
        rv = proj[:, lay.rv + off:lay.rv + off + LANES]
        put(lay.a_rq + off, rotary(rq))
        kr = rotary(rk) * (lay.rd ** -0.5)
        put(lay.a_rka + off, jnp.where(lo_half, kr, 0.0))
        put(lay.a_rkb + off, jnp.where(lo_half, 0.0, kr))
        put(lay.a_kz + off, kr * zeta_ref[:, off:off + LANES])
        put(lay.a_rva + off, jnp.where(lo_half, rv, 0.0))
        put(lay.a_rvb + off, jnp.where(lo_half, 0.0, rv))
    put(lay.a_rgs, _silu(proj[:, lay.rg:lay.rg + rw]))

    put(lay.a_gug, jax.nn.gelu(proj[:, lay.gu:lay.gu + gw]))
    gv = jax.nn.gelu(proj[:, lay.gv:lay.gv + gw])
    gvn = gv * lax.rsqrt(jnp.mean(gv * gv, axis=-1, keepdims=True) + EPS) * mp["gnorm"][...]
    for q_i in range(gw // LANES):
        off = q_i * LANES
        put(lay.a_gva + off, jnp.where(lo_half, gvn[:, off:off + LANES], 0.0))
        put(lay.a_gvb + off, jnp.where(lo_half, 0.0, gvn[:, off:off + LANES]))


def _inproj(x2, tabs, lw, dense, layer, lay, seq):
    t, d = x2.shape
    tm = TOKEN_TILE
    nj = seq // tm
    steps = t // tm
    mps = [lw[k] for k in _INPROJ_PARAMS]
    pos = lambda i: (lax.rem(i, nj), 0)
    side = [dense["w_out"], dense["w_ff1"], dense["w_ff2"]]
    assert len(side) == _N_SIDE
    side_in, side_out, side_shape = [], [], []
    for a in side:
        rows, cols = a.shape[1], a.shape[2]
        assert rows % steps == 0 and (rows // steps) % 16 == 0
        side_in.append(pl.BlockSpec((None, rows // steps, cols), lambda i: (layer, i, 0)))
        side_out.append(pl.BlockSpec((rows // steps, cols), lambda i: (i, 0)))
        side_shape.append(jax.ShapeDtypeStruct((rows, cols), BF16))
    est = (2 * tm * d * 4 + d * lay.n * 2 + 2 * tm * lay.na * 2 + 2 * tm * 2 * LANES * 4
           + 2 * tm * lay.mpad * 4 + 6 * tm * LANES * 4 + (tm + 8) * lay.mw * 4 + tm * lay.n * 4
           + 2 * sum(_nbytes(a.shape[1:], a.dtype) for a in mps)
           + 3 * sum(_nbytes(a.shape[1:], a.dtype) // steps for a in side))
    outs = pl.pallas_call(
        functools.partial(_inproj_kernel, lay=lay, nj=nj),
        grid=(steps,),
        in_specs=([pl.BlockSpec((tm, d), lambda i: (i, 0)),
                   _layer_slab(dense["g_mix"], layer),
                   _layer_slab(dense["w_in_t"], layer, **_ONCE),
                   pl.BlockSpec((tm, LANES), pos),
                   pl.BlockSpec((tm, LANES), pos),
                   _whole(tabs["zeta_tile"])]
                  + [_layer_slab(a, layer) for a in mps] + side_in),
        out_specs=[pl.BlockSpec((tm, lay.na), lambda i: (i, 0)),
                   pl.BlockSpec((tm, 2 * LANES), lambda i: (i, 0)),
                   pl.BlockSpec((tm // CHUNK, lay.mpad, CHUNK), lambda i: (i, 0, 0))] + side_out,
        out_shape=[jax.ShapeDtypeStruct((t, lay.na), BF16),
                   jax.ShapeDtypeStruct((t, 2 * LANES), F32),
                   jax.ShapeDtypeStruct((t // CHUNK, lay.mpad, CHUNK), F32)] + side_shape,
        scratch_shapes=[pltpu.VMEM((tm + 8, lay.mw), F32)],
        compiler_params=pltpu.CompilerParams(
            dimension_semantics=("arbitrary",), vmem_limit_bytes=_vmem_limit(est)),
        name="inproj",
    )(x2, dense["g_mix"], dense["w_in_t"], tabs["cos"], tabs["sin"], tabs["zeta_tile"], *mps, *side)
    return outs[0:3], outs[3:6]


_MIXER_PARAMS = ("mnormt", "rnorm", "wcat", "gbmat")
_TABLES = ("decay", "xi", "cdt", "bmaskt")


def _block_diag2(x, y):
    z = jnp.zeros_like(x)
    return jnp.concatenate([jnp.concatenate([x, z], axis=1), jnp.concatenate([z, y], axis=1)], axis=0)


def _mixer_chunk(c, act, gates, vt, mp, tb, out, c_st, r_st, m_st, lay):
    md, mw, rw, gw = lay.md, lay.mw, lay.rw, lay.gw
    li = lax.broadcasted_iota(jnp.int32, (CHUNK, LANES), 1)
    ri = lax.broadcasted_iota(jnp.int32, (CHUNK, LANES), 0)
    causal_t = ri <= li
    tril = (li <= ri).astype(BF16)
    lo_half = li < (LANES // 2)
    rows = pl.ds(pl.multiple_of(c * CHUNK, CHUNK), CHUNK)

    def a(col, width=LANES):
        return act[rows, col:col + width]

    gi = gates[rows, 0:LANES]
    logf = gates[rows, LANES:2 * LANES]
    a1 = logf.astype(BF16)
    r1 = logf - a1.astype(F32)
    a2 = r1.astype(BF16)
    a3 = (r1 - a2.astype(F32)).astype(BF16)
    b3 = _dot(tril, jnp.concatenate([a1, a2, a3], axis=1))
    bcum = (b3[:, 2 * LANES:3 * LANES] + b3[:, LANES:2 * LANES]) + b3[:, 0:LANES]
    wb = jnp.where(li < M_HEADS, gi - bcum, bcum)
    wb_t = wb.T

    causal_t2 = jnp.concatenate([causal_t, causal_t], axis=1)
    hn_t = []
    for hp in range(M_HEADS // 2):
        heads = (2 * hp, 2 * hp + 1)
        pair = lambda f: jnp.concatenate([f(h) for h in heads], axis=1)
        rep = lambda v: jnp.broadcast_to(v, (1, LANES))
        q2 = a(lay.a_q + 2 * hp * LANES, 2 * LANES)
        k2 = a(lay.a_k + 2 * hp * LANES, 2 * LANES)
        vt2 = pair(lambda h: vt[c, h * LANES:(h + 1) * LANES, :])
        w_row = pair(lambda h: wb_t[h:h + 1, :])
        b_row = pair(lambda h: wb_t[M_HEADS + h:M_HEADS + h + 1, :])
        b_end = pair(lambda h: rep(wb_t[M_HEADS + h:M_HEADS + h + 1, CHUNK - 1:CHUNK]))
        w_end = b_end + w_row
        a_c = pair(lambda h: rep(jnp.max(w_end[:, (h % 2) * LANES:(h % 2 + 1) * LANES], axis=-1, keepdims=True)))
        e_row = jnp.exp(w_end - a_c)
        m_prev = m_st[hp:hp + 1, :]
        m_new = jnp.maximum(b_end + m_prev, a_c)
        s_old = jnp.exp(b_end + m_prev - m_new)
        s_new = jnp.exp(a_c - m_new)
        m_st[hp:hp + 1, :] = m_new

        il = b_row + m_prev
        dm = jnp.where(causal_t2, pair(lambda h: jnp.broadcast_to(wb[:, h:h + 1], (CHUNK, LANES))) + b_row,
                       -jnp.inf)
        m_t = jnp.maximum(il, jnp.max(dm, axis=0, keepdims=True))
        s_inter = jnp.exp(il - m_t)
        c_prev = c_st[hp]
        kq = _dot_nt(jnp.concatenate([k2, c_prev.astype(BF16)], axis=0),
                     _block_diag2(q2[:, 0:LANES], q2[:, LANES:2 * LANES]))
        p_t = kq[0:CHUNK] * jnp.exp(dm - m_t)
        inter = s_inter * kq[CHUNK:2 * CHUNK]
        p_b = p_t.astype(BF16)
        tot = _dot(vt2.astype(BF16), _block_diag2(p_b[:, 0:LANES], p_b[:, LANES:2 * LANES])) + inter
        den = jnp.sum(p_t, axis=0, keepdims=True) + inter[md:md + 1, :]
        rden = 1.0 / jnp.maximum(jnp.abs(den), jnp.exp(-m_t))
        hout = tot[0:md, :] * rden
        ms = jnp.sum(hout * hout, axis=0, keepdims=True) * (1.0 / md)
        hn = hout * lax.rsqrt(ms + EPS)
        for i, h in enumerate(heads):
            hn_t.append(hn[:, i * LANES:(i + 1) * LANES] * mp["mnormt"][h * md:(h + 1) * md, :])
        c_st[hp] = s_old * c_prev + s_new * _dot((vt2 * e_row).astype(BF16),
                                                 _block_diag2(k2[:, 0:LANES], k2[:, LANES:2 * LANES]))

    hm_t = jnp.concatenate(hn_t, axis=0)
    hm = jnp.concatenate([hm_t[t * LANES:(t + 1) * LANES, :].T for t in range(mw // LANES)], axis=1)
    out[rows, 0:mw] = (hm * a(lay.a_zs, mw).astype(F32) + a(lay.a_sxz, mw).astype(F32)).astype(out.dtype)

    for p_i in range(rw // LANES):
        off = p_i * LANES
        qb = a(lay.a_rq + off)
        rva, rvb = a(lay.a_rva + off), a(lay.a_rvb + off)
        r_prev = r_st[p_i]
        sqr = _dot_nt(qb, jnp.concatenate([a(lay.a_rka + off), a(lay.a_rkb + off), r_prev.astype(BF16)],
                                          axis=0))
        p_ab = (sqr[:, 0:2 * LANES] * tb["decay"][p_i]).astype(BF16)
        hr = _dot(p_ab, jnp.concatenate([rva, rvb], axis=0)) + sqr[:, 2 * LANES:3 * LANES] * tb["xi"][p_i]
        sq = hr * hr
        ms_a = jnp.sum(jnp.where(lo_half, sq, 0.0), axis=-1, keepdims=True)
        ms_b = jnp.sum(jnp.where(lo_half, 0.0, sq), axis=-1, keepdims=True)
        ms = jnp.where(lo_half, ms_a, ms_b) * (1.0 / lay.rd)
        hn = hr * lax.rsqrt(ms + EPS) * mp["rnorm"][:, off:off + LANES]
        out[rows, mw + off:mw + off + LANES] = (hn * a(lay.a_rgs + off).astype(F32)).astype(out.dtype)
        r_st[p_i] = tb["cdt"][p_i] * r_prev + tb["bmaskt"][...] * _dot_tn(rva + rvb, a(lay.a_kz + off))

    zero = jnp.zeros((2 * CHUNK, LANES), BF16)
    vst = [jnp.concatenate([a(lay.a_gva + q_i * LANES), a(lay.a_gvb + q_i * LANES)], axis=0)
           for q_i in range(gw // LANES)]
    assert len(vst) == 2
    mixed = _dot(mp["wcat"][...], jnp.concatenate([jnp.concatenate([vst[0], zero], axis=1),
                                                   jnp.concatenate([zero, vst[1]], axis=1)], axis=0))
    out[rows, mw + rw:mw + rw + gw] = (a(lay.a_gug, gw).astype(F32) * (mixed + mp["gbmat"][...])).astype(out.dtype)


def _mixer_kernel(*refs, lay):
    n_mp, n_tb = len(_MIXER_PARAMS), len(_TABLES)
    act_ref, gates_ref, vt_ref = refs[0:3]
    mp = dict(zip(_MIXER_PARAMS, refs[3:3 + n_mp]))
    tb = dict(zip(_TABLES, refs[3 + n_mp:3 + n_mp + n_tb]))
    o_ref = refs[3 + n_mp + n_tb]
    c_st, r_st, m_st = refs[4 + n_mp + n_tb:]
    nrow, ts = act_ref.shape[0], act_ref.shape[1]

    @pl.when(pl.program_id(1) == 0)
    def _():
        c_st[...] = jnp.zeros_like(c_st)
        r_st[...] = jnp.zeros_like(r_st)
        m_st[...] = jnp.zeros_like(m_st)

    def step(c, carry):
        for r in range(nrow):
            _mixer_chunk(c, act_ref.at[r], gates_ref.at[r], vt_ref.at[r], mp, tb, o_ref.at[r],
                         c_st.at[r], r_st.at[r], m_st.at[r], lay)
        return carry

    lax.fori_loop(0, ts // CHUNK, step, 0)


def _mixer(act, gates, vt, tabs, lw, layer, lay, batch, seq):
    ts, nrow = MIX_TILE, MIX_ROWS
    mps = [lw[k] for k in _MIXER_PARAMS]
    tbs = [tabs[k] for k in _TABLES]
    scratch = [pltpu.VMEM((nrow, M_HEADS // 2, LANES, 2 * LANES), F32),
               pltpu.VMEM((nrow, lay.rw // LANES, LANES, LANES), F32),
               pltpu.VMEM((nrow, 8, 2 * LANES), F32)]
    est = (2 * nrow * ts * (lay.na * 2 + 2 * LANES * 4 + lay.mpad * 4 + lay.dmix * 2)
           + 2 * sum(_nbytes(a.shape[1:], a.dtype) for a in mps) + 2 * sum(_nbytes(a.shape, a.dtype) for a in tbs)
           + sum(_nbytes(s.shape, s.dtype) for s in scratch))
    return pl.pallas_call(
        functools.partial(_mixer_kernel, lay=lay),
        grid=(batch // nrow, seq // ts),
        in_specs=([pl.BlockSpec((nrow, ts, lay.na), lambda b, j: (b, j, 0)),
                   pl.BlockSpec((nrow, ts, 2 * LANES), lambda b, j: (b, j, 0)),
                   pl.BlockSpec((nrow, ts // CHUNK, lay.mpad, CHUNK), lambda b, j: (b, j, 0, 0))]
                  + [_layer_slab(a, layer) for a in mps] + [_whole(a) for a in tbs]),
        out_specs=pl.BlockSpec((nrow, ts, lay.dmix), lambda b, j: (b, j, 0)),
        out_shape=jax.ShapeDtypeStruct((batch, seq, lay.dmix), BF16),
        scratch_shapes=scratch,
        compiler_params=pltpu.CompilerParams(
            dimension_semantics=("arbitrary", "arbitrary"), vmem_limit_bytes=_vmem_limit(est)),
        name="mixer",
    )(act, gates, vt, *mps, *tbs)


def _ffn_kernel(mix_ref, x_ref, wo_ref, g_ref, w1_ref, w2_ref, gf_ref, o_ref, *, final):
    x1 = x_ref[...] + _dot(mix_ref[...], wo_ref[...])
    h2 = _rms(x1, g_ref[...]).astype(BF16)
    acc = x1
    for j in range(w1_ref.shape[1] // FF_CHUNK):
        cols = slice(j * FF_CHUNK, (j + 1) * FF_CHUNK)
        a = jnp.square(jnp.maximum(_dot(h2, w1_ref[:, cols]), 0.0)).astype(BF16)
        acc = acc + _dot(a, w2_ref[cols, :])
    if final:
        acc = _rms(acc, gf_ref[...])
    o_ref[...] = acc


def _ffn(mix, x2, wts, dense, layer, final):
    t, d = x2.shape
    wo, w1, w2 = wts
    dff = w1.shape[1]
    tm = FFN_TILE
    est = (2 * tm * d * 2 + 4 * tm * d * 4 + (d * d + 2 * d * dff) * 2
           + tm * FF_CHUNK * 6 + 3 * tm * d * 4)
    resident = lambda a: pl.BlockSpec(a.shape, lambda i: (0, 0), **_ONCE)
    return pl.pallas_call(
        functools.partial(_ffn_kernel, final=final),
        grid=(t // tm,),
        in_specs=[pl.BlockSpec((tm, d), lambda i: (i, 0)),
                  pl.BlockSpec((tm, d), lambda i: (i, 0)),
                  resident(wo),
                  _layer_slab(dense["g_ff"], layer),
                  resident(w1),
                  resident(w2),
                  pl.BlockSpec((1, d), lambda i: (0, 0))],
        out_specs=pl.BlockSpec((tm, d), lambda i: (i, 0)),
        out_shape=jax.ShapeDtypeStruct((t, d), F32),
        compiler_params=pltpu.CompilerParams(
            dimension_semantics=("arbitrary",), vmem_limit_bytes=_vmem_limit(est)),
        name="ffn_final" if final else "ffn",
    )(mix, x2, wo, dense["g_ff"], w1, w2, dense["g_final"])


def _tables(lay, seq):
    rd, half = lay.rd, lay.rd // 2
    l = CHUNK
    pos = jnp.arange(seq, dtype=F32)
    freqs = ROPE_THETA ** (-jnp.arange(half, dtype=F32) / half)
    ang = pos[:, None] * freqs[None, :]
    sign = jnp.asarray(np.where(np.arange(LANES) % rd < half, -1.0, 1.0), F32)
    cos = jnp.tile(jnp.cos(ang), (1, LANES // half))
    sin = jnp.tile(jnp.sin(ang), (1, LANES // half)) * sign[None, :]
    log_gamma = jnp.log(1.0 - 2.0 ** (-5.0 - jnp.arange(R_HEADS, dtype=F32)))
    p = jnp.arange(l, dtype=F32)
    causal = jnp.tril(jnp.ones((l, l), dtype=bool))
    rel = jnp.where(causal, p[:, None] - p[None, :], 0.0)
    decay = jnp.where(causal, jnp.exp(rel[None] * log_gamma[:, None, None]), 0.0)
    zeta = jnp.exp((l - 1.0 - p)[None, :] * log_gamma[:, None])
    xi = jnp.exp((p + 1.0)[None, :] * log_gamma[:, None])
    cdec = jnp.exp(l * log_gamma)
    npair = R_HEADS // 2
    decay_ab = decay.reshape(npair, 2, l, l).transpose(0, 2, 1, 3).reshape(npair, l, 2 * l)
    by_lane = lambda a: jnp.repeat(a.reshape(npair, 2, -1), rd, axis=1)
    xi_ab = by_lane(xi).transpose(0, 2, 1)
    zeta_k = by_lane(zeta).transpose(0, 2, 1)
    zeta_tile = jnp.tile(zeta_k.transpose(1, 0, 2).reshape(l, npair * LANES), (TOKEN_TILE // l, 1))
    cdt = jnp.broadcast_to(by_lane(cdec).transpose(0, 2, 1), (npair, LANES, LANES))
    head = np.arange(LANES) // rd
    bmaskt = jnp.asarray((head[:, None] == head[None, :]).astype(np.float32))
    return {"cos": cos, "sin": sin, "decay": decay_ab, "xi": xi_ab, "zeta_tile": zeta_tile, "cdt": cdt,
            "bmaskt": bmaskt}


def _block_diag_padded(w, lay):
    depth, nb = w.shape[0], w.shape[1]
    rows = w.reshape(depth, nb * M_BLOCK, M_BLOCK)
    tiled = jnp.tile(rows, (1, 1, lay.mpad // M_BLOCK))
    r = np.arange(nb * M_BLOCK)[:, None]
    c = np.arange(lay.mpad)[None, :]
    packed_c = (c // LANES) * lay.md + c % LANES
    keep = (c % LANES < lay.md) & (r // M_BLOCK == packed_c // M_BLOCK)
    return jnp.where(jnp.asarray(keep), tiled, 0.0).astype(BF16)


def _mixer_weights(lay, p):
    depth = p["m_wq"].shape[0]
    gbi = jnp.concatenate([p["m_i_bias"], jnp.zeros((depth, LANES - M_HEADS), F32)], axis=-1)
    gbf = jnp.concatenate(
        [p["m_f_bias"], p["m_f_bias"], jnp.zeros((depth, LANES - 2 * M_HEADS), F32)], axis=-1)
    causal = jnp.tril(jnp.ones((CHUNK, CHUNK), dtype=bool))
    wm = jnp.where(causal, p["g_ws"], 0.0)
    wcat = wm.transpose(0, 2, 1, 3).reshape(depth, CHUNK, G_GROUPS * CHUNK)
    gbmat = jnp.repeat(p["g_bs"].transpose(0, 2, 1), lay.gw // G_GROUPS, axis=2)
    row = lambda a: a[:, None, :]
    return {
        "convw": p["m_conv_w"], "convb": row(p["m_conv_b"]),
        "bdq": _block_diag_padded(p["m_wq"], lay),
        "bdk": _block_diag_padded(p["m_wk"], lay),
        "bdvt": _block_diag_padded(p["m_wv"], lay).transpose(0, 2, 1),
        "gbi": row(gbi), "gbf": row(gbf),
        "mnormt": jnp.broadcast_to(p["m_norm_g"][:, :, None], (depth, lay.mw, LANES)),
        "mskip": row(p["m_skip"]),
        "rnorm": row(p["r_norm_g"]),
        "gnorm": row(p["g_norm_g"]),
        "wcat": wcat.astype(BF16), "gbmat": gbmat,
    }


def kernel(x, norm_mix_g, w_in, m_conv_w, m_conv_b, m_wq, m_wk, m_wv, m_i_bias, m_f_bias, m_norm_g,
           m_skip, r_norm_g, g_norm_g, g_ws, g_bs, w_out, norm_ff_g, w_ff1, w_ff2, final_norm_g):
    batch, seq, d = x.shape
    depth = w_in.shape[0]
    lay = _Layout(d)
    assert seq % MIX_TILE == 0 and MIX_TILE % CHUNK == 0 and batch % MIX_ROWS == 0
    assert seq % TOKEN_TILE == 0 and TOKEN_TILE % CHUNK == 0 and w_ff1.shape[2] % FF_CHUNK == 0
    assert (batch * seq) % FFN_TILE == 0
    params = dict(m_conv_w=m_conv_w, m_conv_b=m_conv_b, m_wq=m_wq, m_wk=m_wk, m_wv=m_wv,
                  m_i_bias=m_i_bias, m_f_bias=m_f_bias, m_norm_g=m_norm_g, m_skip=m_skip,
                  r_norm_g=r_norm_g, g_norm_g=g_norm_g, g_ws=g_ws, g_bs=g_bs)
    tabs = _tables(lay, seq)
    lw = _mixer_weights(lay, params)
    dense = {
        "w_in_t": lay.relayout_in_proj_t(w_in), "w_out": w_out, "w_ff1": w_ff1, "w_ff2": w_ff2,
        "g_mix": norm_mix_g[:, None, :], "g_ff": norm_ff_g[:, None, :], "g_final": final_norm_g[None, :],
    }
    x2 = x.reshape(batch * seq, d)
    for layer in range(depth):
        (act, gates, vt), ffn_w = _inproj(x2, tabs, lw, dense, layer, lay, seq)
        mix = _mixer(act.reshape(batch, seq, lay.na), gates.reshape(batch, seq, 2 * LANES),
                     vt.reshape(batch, seq // CHUNK, lay.mpad, CHUNK), tabs, lw, layer, lay, batch, seq)
        x2 = _ffn(mix.reshape(batch * seq, lay.dmix), x2, ffn_w, dense, layer, final=(layer == depth - 1))
    return x2.reshape(batch, seq, d)
```

```python
import functools

import numpy as np
import jax
import jax.numpy as jnp
from jax import lax
from jax.experimental import pallas as pl
from jax.experimental.pallas import tpu as pltpu

M_HEADS = 4
M_CONV = 4
M_BLOCK = 4
R_HEADS = 6
G_GROUPS = 4
CHUNK = 128
ROPE_THETA = 10000.0
EPS = 1e-6

LANES = 128
V7X_VMEM_BYTES = 64 * 1024 * 1024

TOKEN_TILE = 512
FFN_TILE = 1024
MIX_TILE = 512
MIX_ROWS = 2
FF_CHUNK = 1024

F32 = jnp.float32
BF16 = jnp.bfloat16


def _rms(x, g):
    return x * lax.rsqrt(jnp.mean(jnp.square(x), axis=-1, keepdims=True) + EPS) * g


def _silu(x):
    return x * jax.nn.sigmoid(x)


def _dot(a, b):
    return jnp.dot(a, b, preferred_element_type=F32)


def _dot_nt(a, b):
    return lax.dot_general(a, b, (((1,), (1,)), ((), ())), preferred_element_type=F32)


def _dot_tn(a, b):
    return lax.dot_general(a, b, (((0,), (0,)), ((), ())), preferred_element_type=F32)


def _nbytes(shape, dtype):
    return int(np.prod(shape)) * jnp.dtype(dtype).itemsize


def _vmem_limit(nbytes):
    return int(min(nbytes + (12 << 20), V7X_VMEM_BYTES - (4 << 20)))


def _layer_slab(a, layer, **kw):
    return pl.BlockSpec((None,) + a.shape[1:], lambda *g: (layer,) + (0,) * (a.ndim - 1), **kw)


def _whole(a):
    return pl.BlockSpec(a.shape, lambda *g: (0,) * a.ndim)


_ONCE = dict(pipeline_mode=pl.Buffered(1))


class _Layout:
    def __init__(self, d_model):
        self.mw = 3 * d_model // 8
        self.rw = 3 * d_model // 8
        self.gw = d_model - self.mw - self.rw
        self.md = self.mw // M_HEADS
        self.rd = self.rw // R_HEADS
        assert self.md <= LANES and 2 * self.rd == LANES and self.gw // G_GROUPS * 2 == LANES
        assert self.mw % LANES == 0 and self.rw % LANES == 0 and self.gw % LANES == 0
        self.mpad = M_HEADS * LANES
        self.dmix = self.mw + self.rw + self.gw
        o = 0
        self.mx = o; o += self.mw
        self.mz = o; o += self.mw
        self.rq = o; o += self.rw
        self.rk = o; o += self.rw
        self.rv = o; o += self.rw
        self.rg = o; o += self.rw
        self.gu = o; o += self.gw
        self.gv = o; o += self.gw
        self.gi = o; o += LANES
        self.gf = o; o += LANES
        self.n = o
        o = 0
        self.a_q = o; o += self.mpad
        self.a_k = o; o += self.mpad
        self.a_rq = o; o += self.rw
        self.a_rka = o; o += self.rw
        self.a_rkb = o; o += self.rw
        self.a_kz = o; o += self.rw
        self.a_rva = o; o += self.rw
        self.a_rvb = o; o += self.rw
        self.a_gva = o; o += self.gw
        self.a_gvb = o; o += self.gw
        self.a_zs = o; o += self.mw
        self.a_sxz = o; o += self.mw
        self.a_rgs = o; o += self.rw
        self.a_gug = o; o += self.gw
        self.na = o


RELAYOUT_ROWS = 256


def _relayout_kernel(w_ref, o_ref, *, lay):
    d = w_ref.shape[1]
    o_gate = 2 * lay.mw
    o_rest = o_gate + 2 * M_HEADS
    n_rest = 4 * lay.rw + 2 * lay.gw
    for src, dst, n in ((0, 0, o_gate), (o_rest, lay.rq, n_rest)):
        for r in range(0, n, RELAYOUT_ROWS):
            m = min(RELAYOUT_ROWS, n - r)
            o_ref[dst + r:dst + r + m, :] = w_ref[src + r:src + r + m, :].astype(o_ref.dtype)
    gi = w_ref[o_gate:o_gate + M_HEADS, :]
    gf = w_ref[o_gate + M_HEADS:o_gate + 2 * M_HEADS, :]
    o_ref[lay.gi:lay.gi + LANES, :] = jnp.concatenate(
        [gi, jnp.zeros((LANES - M_HEADS, d), F32)], axis=0).astype(o_ref.dtype)
    o_ref[lay.gf:lay.gf + LANES, :] = jnp.concatenate(
        [gf, gf, jnp.zeros((LANES - 2 * M_HEADS, d), F32)], axis=0).astype(o_ref.dtype)


def _relayout_in_proj_t(w_in, lay):
    depth, d, n_in = w_in.shape
    wt = jnp.swapaxes(w_in, 1, 2)
    est = 2 * n_in * d * 4 + 2 * lay.n * d * 2 + 4 * RELAYOUT_ROWS * d * 4
    return pl.pallas_call(
        functools.partial(_relayout_kernel, lay=lay),
        grid=(depth,),
        in_specs=[pl.BlockSpec((None, n_in, d), lambda l: (l, 0, 0))],
        out_specs=pl.BlockSpec((None, lay.n, d), lambda l: (l, 0, 0)),
        out_shape=jax.ShapeDtypeStruct((depth, lay.n, d), BF16),
        compiler_params=pltpu.CompilerParams(
            dimension_semantics=("arbitrary",), vmem_limit_bytes=_vmem_limit(est)),
        name="relayout_w_in",
    )(wt)


_INPROJ_PARAMS = ("convw", "convb", "bdq", "bdk", "bdvt", "gbi", "gbf", "mskip", "gnorm")
_N_SIDE = 3


def _inproj_kernel(*refs, lay, nj):
    n_p = len(_INPROJ_PARAMS)
    x_ref, g_ref, w_ref, cos_ref, sin_ref, zeta_ref = refs[0:6]
    mp = dict(zip(_INPROJ_PARAMS, refs[6:6 + n_p]))
    wf = refs[6 + n_p:6 + n_p + _N_SIDE]
    act_ref, gates_ref, vt_ref = refs[6 + n_p + _N_SIDE:9 + n_p + _N_SIDE]
    wb = refs[9 + n_p + _N_SIDE:9 + n_p + 2 * _N_SIDE]
    xbuf = refs[9 + n_p + 2 * _N_SIDE]
    tm = x_ref.shape[0]
    for src, dst in zip(wf, wb):
        dst[...] = src[...].astype(dst.dtype)

    md, mw, rw, gw = lay.md, lay.mw, lay.rw, lay.gw
    first_of_seq = lax.rem(pl.program_id(0), nj) == 0

    @pl.when(first_of_seq)
    def _():
        xbuf[0:8, :] = jnp.zeros((8, mw), F32)

    @pl.when(jnp.logical_not(first_of_seq))
    def _():
        xbuf[0:8, :] = xbuf[tm:tm + 8, :]

    proj = _dot_nt(_rms(x_ref[...], g_ref[...]).astype(BF16), w_ref[...])

    def put(col, val):
        act_ref[:, col:col + val.shape[1]] = val.astype(act_ref.dtype)

    mx = proj[:, lay.mx:lay.mx + mw]
    xbuf[8:tm + 8, :] = mx
    conv = mp["convb"][...] + mp["convw"][M_CONV - 1:M_CONV, :] * mx
    for t in range(M_CONV - 1):
        conv = conv + mp["convw"][t:t + 1, :] * xbuf[8 - (M_CONV - 1) + t:8 - (M_CONV - 1) + t + tm, :]
    xc = _silu(conv)
    xcb = xc.astype(BF16)
    put(lay.a_q, _dot(xcb, mp["bdq"][...]))
    put(lay.a_k, _dot(xcb, mp["bdk"][...]) * (md ** -0.5))
    row_p = lax.broadcasted_iota(jnp.int32, (lay.mpad, 1), 0) % LANES
    vt = _dot_nt(mp["bdvt"][...], mx.astype(BF16)) + (row_p == md).astype(F32)
    for c in range(tm // CHUNK):
        vt_ref[c] = vt[:, c * CHUNK:(c + 1) * CHUNK]
    zs = _silu(proj[:, lay.mz:lay.mz + mw])
    put(lay.a_zs, zs)
    put(lay.a_sxz, mp["mskip"][...] * xc * zs)

    li = lax.broadcasted_iota(jnp.int32, (tm, LANES), 1)
    gates_ref[:, 0:LANES] = proj[:, lay.gi:lay.gi + LANES] + mp["gbi"][...]
    gf = proj[:, lay.gf:lay.gf + LANES] + mp["gbf"][...]
    gates_ref[:, LANES:2 * LANES] = jnp.where(li < 2 * M_HEADS, jax.nn.log_sigmoid(gf), 0.0)

    lo_half = li < (LANES // 2)
    half = lay.rd // 2
    first_half = (li % lay.rd) < half
    cs = cos_ref[...]
    sn = sin_ref[...]

    def rotary(v):
        partner = jnp.where(first_half, pltpu.roll(v, LANES - half, axis=1), pltpu.roll(v, half, axis=1))
        return v * cs + partner * sn

    for p_i in range(rw // LANES):
        off = p_i * LANES
        rq = proj[:, lay.rq + off:lay.rq + off + LANES]
        rk = proj[:, lay.rk + off:lay.rk + off + LANES]
        rv = proj[:, lay.rv + off:lay.rv + off + LANES]
        put(lay.a_rq + off, rotary(rq))
        kr = rotary(rk) * (lay.rd ** -0.5)
        put(lay.a_rka + off, jnp.where(lo_half, kr, 0.0))
        put(lay.a_rkb + off, jnp.where(lo_half, 0.0, kr))
        put(lay.a_kz + off, kr * zeta_ref[:, off:off + LANES])
        put(lay.a_rva + off, jnp.where(lo_half, rv, 0.0))
        put(lay.a_rvb + off, jnp.where(lo_half, 0.0, rv))
    put(lay.a_rgs, _silu(proj[:, lay.rg:lay.rg + rw]))

    put(lay.a_gug, jax.nn.gelu(proj[:, lay.gu:lay.gu + gw]))
    gv = jax.nn.gelu(proj[:, lay.gv:lay.gv + gw])
    gvn = gv * lax.rsqrt(jnp.mean(gv * gv, axis=-1, keepdims=True) + EPS) * mp["gnorm"][...]
    for q_i in range(gw // LANES):
        off = q_i * LANES
        put(lay.a_gva + off, jnp.where(lo_half, gvn[:, off:off + LANES], 0.0))
        put(lay.a_gvb + off, jnp.where(lo_half, 0.0, gvn[:, off:off + LANES]))


def _inproj(x2, tabs, lw, dense, layer, lay, seq):
    t, d = x2.shape
    tm = TOKEN_TILE
    nj = seq // tm
    steps = t // tm
    mps = [lw[k] for k in _INPROJ_PARAMS]
    pos = lambda i: (lax.rem(i, nj), 0)
    side = [dense["w_out"], dense["w_ff1"], dense["w_ff2"]]
    assert len(side) == _N_SIDE
    side_in, side_out, side_shape = [], [], []
    for a in side:
        rows, cols = a.shape[1], a.shape[2]
        assert rows % steps == 0 and (rows // steps) % 16 == 0
        side_in.append(pl.BlockSpec((None, rows // steps, cols), lambda i: (layer, i, 0)))
        side_out.append(pl.BlockSpec((rows // steps, cols), lambda i: (i, 0)))
        side_shape.append(jax.ShapeDtypeStruct((rows, cols), BF16))
    est = (2 * tm * d * 4 + d * lay.n * 2 + 2 * tm * lay.na * 2 + 2 * tm * 2 * LANES * 4
           + 2 * tm * lay.mpad * 4 + 6 * tm * LANES * 4 + (tm + 8) * lay.mw * 4 + tm * lay.n * 4
           + 2 * sum(_nbytes(a.shape[1:], a.dtype) for a in mps)
           + 3 * sum(_nbytes(a.shape[1:], a.dtype) // steps for a in side))
    outs = pl.pallas_call(
        functools.partial(_inproj_kernel, lay=lay, nj=nj),
        grid=(steps,),
        in_specs=([pl.BlockSpec((tm, d), lambda i: (i, 0)),
                   _layer_slab(dense["g_mix"], layer),
                   _layer_slab(dense["w_in_t"], layer, **_ONCE),
                   pl.BlockSpec((tm, LANES), pos),
                   pl.BlockSpec((tm, LANES), pos),
                   _whole(tabs["zeta_tile"])]
                  + [_layer_slab(a, layer) for a in mps] + side_in),
        out_specs=[pl.BlockSpec((tm, lay.na), lambda i: (i, 0)),
                   pl.BlockSpec((tm, 2 * LANES), lambda i: (i, 0)),
                   pl.BlockSpec((tm // CHUNK, lay.mpad, CHUNK), lambda i: (i, 0, 0))] + side_out,
        out_shape=[jax.ShapeDtypeStruct((t, lay.na), BF16),
                   jax.ShapeDtypeStruct((t, 2 * LANES), F32),
                   jax.ShapeDtypeStruct((t // CHUNK, lay.mpad, CHUNK), F32)] + side_shape,
        scratch_shapes=[pltpu.VMEM((tm + 8, lay.mw), F32)],
        compiler_params=pltpu.CompilerParams(
            dimension_semantics=("arbitrary",), vmem_limit_bytes=_vmem_limit(est)),
        name="inproj",
    )(x2, dense["g_mix"], dense["w_in_t"], tabs["cos"], tabs["sin"], tabs["zeta_tile"], *mps, *side)
    return outs[0:3], outs[3:6]


_MIXER_PARAMS = ("mnormt", "rnorm", "wcat", "gbmat")
_TABLES = ("decay", "xi", "cdt", "bmaskt")


def _block_diag2(x, y):
    z = jnp.zeros_like(x)
    return jnp.concatenate([jnp.concatenate([x, z], axis=1), jnp.concatenate([z, y], axis=1)], axis=0)


def _mixer_chunk(c, act, gates, vt, mp, tb, out, c_st, r_st, m_st, lay):
    md, mw, rw, gw = lay.md, lay.mw, lay.rw, lay.gw
    li = lax.broadcasted_iota(jnp.int32, (CHUNK, LANES), 1)
    ri = lax.broadcasted_iota(jnp.int32, (CHUNK, LANES), 0)
    causal_t = ri <= li
    tril = (li <= ri).astype(BF16)
    lo_half = li < (LANES // 2)
    rows = pl.ds(pl.multiple_of(c * CHUNK, CHUNK), CHUNK)

    def a(col, width=LANES):
        return act[rows, col:col + width]

    gi = gates[rows, 0:LANES]
    logf = gates[rows, LANES:2 * LANES]
    a1 = logf.astype(BF16)
    r1 = logf - a1.astype(F32)
    a2 = r1.astype(BF16)
    a3 = (r1 - a2.astype(F32)).astype(BF16)
    b3 = _dot(tril, jnp.concatenate([a1, a2, a3], axis=1))
    bcum = (b3[:, 2 * LANES:3 * LANES] + b3[:, LANES:2 * LANES]) + b3[:, 0:LANES]
    wb = jnp.where(li < M_HEADS, gi - bcum, bcum)
    wb_t = wb.T

    causal_t2 = jnp.concatenate([causal_t, causal_t], axis=1)
    hn_t = []
    for hp in range(M_HEADS // 2):
        heads = (2 * hp, 2 * hp + 1)
        pair = lambda f: jnp.concatenate([f(h) for h in heads], axis=1)
        rep = lambda v: jnp.broadcast_to(v, (1, LANES))
        q2 = a(lay.a_q + 2 * hp * LANES, 2 * LANES)
        k2 = a(lay.a_k + 2 * hp * LANES, 2 * LANES)
        vt2 = pair(lambda h: vt[c, h * LANES:(h + 1) * LANES, :])
        w_row = pair(lambda h: wb_t[h:h + 1, :])
        b_row = pair(lambda h: wb_t[M_HEADS + h:M_HEADS + h + 1, :])
        b_end = pair(lambda h: rep(wb_t[M_HEADS + h:M_HEADS + h + 1, CHUNK - 1:CHUNK]))
        w_end = b_end + w_row
        a_c = pair(lambda h: rep(jnp.max(w_end[:, (h % 2) * LANES:(h % 2 + 1) * LANES], axis=-1, keepdims=True)))
        e_row = jnp.exp(w_end - a_c)
        m_prev = m_st[hp:hp + 1, :]
        m_new = jnp.maximum(b_end + m_prev, a_c)
        s_old = jnp.exp(b_end + m_prev - m_new)
        s_new = jnp.exp(a_c - m_new)
        m_st[hp:hp + 1, :] = m_new

        il = b_row + m_prev
        dm = jnp.where(causal_t2, pair(lambda h: jnp.broadcast_to(wb[:, h:h + 1], (CHUNK, LANES))) + b_row,
                       -jnp.inf)
        m_t = jnp.maximum(il, jnp.max(dm, axis=0, keepdims=True))
        s_inter = jnp.exp(il - m_t)
        c_prev = c_st[hp]
        kq = _dot_nt(jnp.concatenate([k2, c_prev.astype(BF16)], axis=0),
                     _block_diag2(q2[:, 0:LANES], q2[:, LANES:2 * LANES]))
        p_t = kq[0:CHUNK] * jnp.exp(dm - m_t)
        inter = s_inter * kq[CHUNK:2 * CHUNK]
        p_b = p_t.astype(BF16)
        tot = _dot(vt2.astype(BF16), _block_diag2(p_b[:, 0:LANES], p_b[:, LANES:2 * LANES])) + inter
        den = jnp.sum(p_t, axis=0, keepdims=True) + inter[md:md + 1, :]
        rden = 1.0 / jnp.maximum(jnp.abs(den), jnp.exp(-m_t))
        hout = tot[0:md, :] * rden
        ms = jnp.sum(hout * hout, axis=0, keepdims=True) * (1.0 / md)
        hn = hout * lax.rsqrt(ms + EPS)
        for i, h in enumerate(heads):
            hn_t.append(hn[:, i * LANES:(i + 1) * LANES] * mp["mnormt"][h * md:(h + 1) * md, :])
        c_st[hp] = s_old * c_prev + s_new * _dot((vt2 * e_row).astype(BF16),
                                                 _block_diag2(k2[:, 0:LANES], k2[:, LANES:2 * LANES]))

    hm_t = jnp.concatenate(hn_t, axis=0)
    hm = jnp.concatenate([hm_t[t * LANES:(t + 1) * LANES, :].T for t in range(mw // LANES)], axis=1)
    out[rows, 0:mw] = (hm * a(lay.a_zs, mw).astype(F32) + a(lay.a_sxz, mw).astype(F32)).astype(out.dtype)

    for p_i in range(rw // LANES):
        off = p_i * LANES
        qb = a(lay.a_rq + off)
        rva, rvb = a(lay.a_rva + off), a(lay.a_rvb + off)
        r_prev = r_st[p_i]
        sqr = _dot_nt(qb, jnp.concatenate([a(lay.a_rka + off), a(lay.a_rkb + off), r_prev.astype(BF16)],
                                          axis=0))
        p_ab = (sqr[:, 0:2 * LANES] * tb["decay"][p_i]).astype(BF16)
        hr = _dot(p_ab, jnp.concatenate([rva, rvb], axis=0)) + sqr[:, 2 * LANES:3 * LANES] * tb["xi"][p_i]
        sq = hr * hr
        ms_a = jnp.sum(jnp.where(lo_half, sq, 0.0), axis=-1, keepdims=True)
        ms_b = jnp.sum(jnp.where(lo_half, 0.0, sq), axis=-1, keepdims=True)
        ms = jnp.where(lo_half, ms_a, ms_b) * (1.0 / lay.rd)
        hn = hr * lax.rsqrt(ms + EPS) * mp["rnorm"][:, off:off + LANES]
        out[rows, mw + off:mw + off + LANES] = (hn * a(lay.a_rgs + off).astype(F32)).astype(out.dtype)
        r_st[p_i] = tb["cdt"][p_i] * r_prev + tb["bmaskt"][...] * _dot_tn(rva + rvb, a(lay.a_kz + off))

    zero = jnp.zeros((2 * CHUNK, LANES), BF16)
    vst = [jnp.concatenate([a(lay.a_gva + q_i * LANES), a(lay.a_gvb + q_i * LANES)], axis=0)
           for q_i in range(gw // LANES)]
    assert len(vst) == 2
    mixed = _dot(mp["wcat"][...], jnp.concatenate([jnp.concatenate([vst[0], zero], axis=1),
                                                   jnp.concatenate([zero, vst[1]], axis=1)], axis=0))
    out[rows, mw + rw:mw + rw + gw] = (a(lay.a_gug, gw).astype(F32) * (mixed + mp["gbmat"][...])).astype(out.dtype)


def _mixer_kernel(*refs, lay):
    n_mp, n_tb = len(_MIXER_PARAMS), len(_TABLES)
    act_ref, gates_ref, vt_ref = refs[0:3]
    mp = dict(zip(_MIXER_PARAMS, refs[3:3 + n_mp]))
    tb = dict(zip(_TABLES, refs[3 + n_mp:3 + n_mp + n_tb]))
    o_ref = refs[3 + n_mp + n_tb]
    c_st, r_st, m_st = refs[4 + n_mp + n_tb:]
    nrow, ts = act_ref.shape[0], act_ref.shape[1]

    @pl.when(pl.program_id(1) == 0)
    def _():
        c_st[...] = jnp.zeros_like(c_st)
        r_st[...] = jnp.zeros_like(r_st)
        m_st[...] = jnp.zeros_like(m_st)

    def step(c, carry):
        for r in range(nrow):
            _mixer_chunk(c, act_ref.at[r], gates_ref.at[r], vt_ref.at[r], mp, tb, o_ref.at[r],
                         c_st.at[r], r_st.at[r], m_st.at[r], lay)
        return carry

    lax.fori_loop(0, ts // CHUNK, step, 0)


def _mixer(act, gates, vt, tabs, lw, layer, lay, batch, seq):
    ts, nrow = MIX_TILE, MIX_ROWS
    mps = [lw[k] for k in _MIXER_PARAMS]
    tbs = [tabs[k] for k in _TABLES]
    scratch = [pltpu.VMEM((nrow, M_HEADS // 2, LANES, 2 * LANES), F32),
               pltpu.VMEM((nrow, lay.rw // LANES, LANES, LANES), F32),
               pltpu.VMEM((nrow, 8, 2 * LANES), F32)]
    est = (2 * nrow * ts * (lay.na * 2 + 2 * LANES * 4 + lay.mpad * 4 + lay.dmix * 2)
           + 2 * sum(_nbytes(a.shape[1:], a.dtype) for a in mps) + 2 * sum(_nbytes(a.shape, a.dtype) for a in tbs)
           + sum(_nbytes(s.shape, s.dtype) for s in scratch))
    return pl.pallas_call(
        functools.partial(_mixer_kernel, lay=lay),
        grid=(batch // nrow, seq // ts),
        in_specs=([pl.BlockSpec((nrow, ts, lay.na), lambda b, j: (b, j, 0)),
                   pl.BlockSpec((nrow, ts, 2 * LANES), lambda b, j: (b, j, 0)),
                   pl.BlockSpec((nrow, ts // CHUNK, lay.mpad, CHUNK), lambda b, j: (b, j, 0, 0))]
                  + [_layer_slab(a, layer) for a in mps] + [_whole(a) for a in tbs]),
        out_specs=pl.BlockSpec((nrow, ts, lay.dmix), lambda b, j: (b, j, 0)),
        out_shape=jax.ShapeDtypeStruct((batch, seq, lay.dmix), BF16),
        scratch_shapes=scratch,
        compiler_params=pltpu.CompilerParams(
            dimension_semantics=("arbitrary", "arbitrary"), vmem_limit_bytes=_vmem_limit(est)),
        name="mixer",
    )(act, gates, vt, *mps, *tbs)


def _ffn_kernel(mix_ref, x_ref, wo_ref, g_ref, w1_ref, w2_ref, gf_ref, o_ref, *, final):
    x1 = x_ref[...] + _dot(mix_ref[...], wo_ref[...])
    h2 = _rms(x1, g_ref[...]).astype(BF16)
    acc = x1
    for j in range(w1_ref.shape[1] // FF_CHUNK):
        cols = slice(j * FF_CHUNK, (j + 1) * FF_CHUNK)
        a = jnp.square(jnp.maximum(_dot(h2, w1_ref[:, cols]), 0.0)).astype(BF16)
        acc = acc + _dot(a, w2_ref[cols, :])
    if final:
        acc = _rms(acc, gf_ref[...])
    o_ref[...] = acc


def _ffn(mix, x2, wts, dense, layer, final):
    t, d = x2.shape
    wo, w1, w2 = wts
    dff = w1.shape[1]
    tm = FFN_TILE
    est = (2 * tm * d * 2 + 4 * tm * d * 4 + (d * d + 2 * d * dff) * 2
           + tm * FF_CHUNK * 6 + 3 * tm * d * 4)
    resident = lambda a: pl.BlockSpec(a.shape, lambda i: (0, 0), **_ONCE)
    return pl.pallas_call(
        functools.partial(_ffn_kernel, final=final),
        grid=(t // tm,),
        in_specs=[pl.BlockSpec((tm, d), lambda i: (i, 0)),
                  pl.BlockSpec((tm, d), lambda i: (i, 0)),
                  resident(wo),
                  _layer_slab(dense["g_ff"], layer),
                  resident(w1),
                  resident(w2),
                  pl.BlockSpec((1, d), lambda i: (0, 0))],
        out_specs=pl.BlockSpec((tm, d), lambda i: (i, 0)),
        out_shape=jax.ShapeDtypeStruct((t, d), F32),
        compiler_params=pltpu.CompilerParams(
            dimension_semantics=("arbitrary",), vmem_limit_bytes=_vmem_limit(est)),
        name="ffn_final" if final else "ffn",
    )(mix, x2, wo, dense["g_ff"], w1, w2, dense["g_final"])


def _tables(lay, seq):
    rd, half = lay.rd, lay.rd // 2
    l = CHUNK
    pos = jnp.arange(seq, dtype=F32)
    freqs = ROPE_THETA ** (-jnp.arange(half, dtype=F32) / half)
    ang = pos[:, None] * freqs[None, :]
    sign = jnp.asarray(np.where(np.arange(LANES) % rd < half, -1.0, 1.0), F32)
    cos = jnp.tile(jnp.cos(ang), (1, LANES // half))
    sin = jnp.tile(jnp.sin(ang), (1, LANES // half)) * sign[None, :]
    log_gamma = jnp.log(1.0 - 2.0 ** (-5.0 - jnp.arange(R_HEADS, dtype=F32)))
    p = jnp.arange(l, dtype=F32)
    causal = jnp.tril(jnp.ones((l, l), dtype=bool))
    rel = jnp.where(causal, p[:, None] - p[None, :], 0.0)
    decay = jnp.where(causal, jnp.exp(rel[None] * log_gamma[:, None, None]), 0.0)
    zeta = jnp.exp((l - 1.0 - p)[None, :] * log_gamma[:, None])
    xi = jnp.exp((p + 1.0)[None, :] * log_gamma[:, None])
    cdec = jnp.exp(l * log_gamma)
    npair = R_HEADS // 2
    decay_ab = decay.reshape(npair, 2, l, l).transpose(0, 2, 1, 3).reshape(npair, l, 2 * l)
    by_lane = lambda a: jnp.repeat(a.reshape(npair, 2, -1), rd, axis=1)
    xi_ab = by_lane(xi).transpose(0, 2, 1)
    zeta_k = by_lane(zeta).transpose(0, 2, 1)
    zeta_tile = jnp.tile(zeta_k.transpose(1, 0, 2).reshape(l, npair * LANES), (TOKEN_TILE // l, 1))
    cdt = jnp.broadcast_to(by_lane(cdec).transpose(0, 2, 1), (npair, LANES, LANES))
    head = np.arange(LANES) // rd
    bmaskt = jnp.asarray((head[:, None] == head[None, :]).astype(np.float32))
    return {"cos": cos, "sin": sin, "decay": decay_ab, "xi": xi_ab, "zeta_tile": zeta_tile, "cdt": cdt,
            "bmaskt": bmaskt}


def _block_diag_padded(w, lay):
    depth, nb = w.shape[0], w.shape[1]
    rows = w.reshape(depth, nb * M_BLOCK, M_BLOCK)
    tiled = jnp.tile(rows, (1, 1, lay.mpad // M_BLOCK))
    r = np.arange(nb * M_BLOCK)[:, None]
    c = np.arange(lay.mpad)[None, :]
    packed_c = (c // LANES) * lay.md + c % LANES
    keep = (c % LANES < lay.md) & (r // M_BLOCK == packed_c // M_BLOCK)
    return jnp.where(jnp.asarray(keep), tiled, 0.0).astype(BF16)


def _mixer_weights(lay, p):
    depth = p["m_wq"].shape[0]
    gbi = jnp.concatenate([p["m_i_bias"], jnp.zeros((depth, LANES - M_HEADS), F32)], axis=-1)
    gbf = jnp.concatenate(
        [p["m_f_bias"], p["m_f_bias"], jnp.zeros((depth, LANES - 2 * M_HEADS), F32)], axis=-1)
    causal = jnp.tril(jnp.ones((CHUNK, CHUNK), dtype=bool))
    wm = jnp.where(causal, p["g_ws"], 0.0)
    wcat = wm.transpose(0, 2, 1, 3).reshape(depth, CHUNK, G_GROUPS * CHUNK)
    gbmat = jnp.repeat(p["g_bs"].transpose(0, 2, 1), lay.gw // G_GROUPS, axis=2)
    row = lambda a: a[:, None, :]
    return {
        "convw": p["m_conv_w"], "convb": row(p["m_conv_b"]),
        "bdq": _block_diag_padded(p["m_wq"], lay),
        "bdk": _block_diag_padded(p["m_wk"], lay),
        "bdvt": _block_diag_padded(p["m_wv"], lay).transpose(0, 2, 1),
        "gbi": row(gbi), "gbf": row(gbf),
        "mnormt": jnp.broadcast_to(p["m_norm_g"][:, :, None], (depth, lay.mw, LANES)),
        "mskip": row(p["m_skip"]),
        "rnorm": row(p["r_norm_g"]),
        "gnorm": row(p["g_norm_g"]),
        "wcat": wcat.astype(BF16), "gbmat": gbmat,
    }


def kernel(x, norm_mix_g, w_in, m_conv_w, m_conv_b, m_wq, m_wk, m_wv, m_i_bias, m_f_bias, m_norm_g,
           m_skip, r_norm_g, g_norm_g, g_ws, g_bs, w_out, norm_ff_g, w_ff1, w_ff2, final_norm_g):
    batch, seq, d = x.shape
    depth = w_in.shape[0]
    lay = _Layout(d)
    assert seq % MIX_TILE == 0 and MIX_TILE % CHUNK == 0 and batch % MIX_ROWS == 0
    assert seq % TOKEN_TILE == 0 and TOKEN_TILE % CHUNK == 0 and w_ff1.shape[2] % FF_CHUNK == 0
    assert (batch * seq) % FFN_TILE == 0
    params = dict(m_conv_w=m_conv_w, m_conv_b=m_conv_b, m_wq=m_wq, m_wk=m_wk, m_wv=m_wv,
                  m_i_bias=m_i_bias, m_f_bias=m_f_bias, m_norm_g=m_norm_g, m_skip=m_skip,
                  r_norm_g=r_norm_g, g_norm_g=g_norm_g, g_ws=g_ws, g_bs=g_bs)
    tabs = _tables(lay, seq)
    lw = _mixer_weights(lay, params)
    dense = {
        "w_in_t": _relayout_in_proj_t(w_in, lay), "w_out": w_out, "w_ff1": w_ff1, "w_ff2": w_ff2,
        "g_mix": norm_mix_g[:, None, :], "g_ff": norm_ff_g[:, None, :], "g_final": final_norm_g[None, :],
    }
    x2 = x.reshape(batch * seq, d)
    for layer in range(depth):
        (act, gates, vt), ffn_w = _inproj(x2, tabs, lw, dense, layer, lay, seq)
        mix = _mixer(act.reshape(batch, seq, lay.na), gates.reshape(batch, seq, 2 * LANES),
                     vt.reshape(batch, seq // CHUNK, lay.mpad, CHUNK), tabs, lw, layer, lay, batch, seq)
        x2 = _ffn(mix.reshape(batch * seq, lay.dmix), x2, ffn_w, dense, layer, final=(layer == depth - 1))
    return x2.reshape(batch, seq, d)
```

```python
import functools

import numpy as np
import jax
import jax.numpy as jnp
from jax import lax
from jax.experimental import pallas as pl
from jax.experimental.pallas import tpu as pltpu

M_HEADS = 4
M_CONV = 4
M_BLOCK = 4
R_HEADS = 6
G_GROUPS = 4
CHUNK = 128
ROPE_THETA = 10000.0
EPS = 1e-6

LANES = 128
V7X_VMEM_BYTES = 64 * 1024 * 1024

TOKEN_TILE = 512
FFN_TILE = 1024
MIX_TILE = 256
MIX_ROWS = 4
FF_CHUNK = 1024

F32 = jnp.float32
BF16 = jnp.bfloat16


def _rms(x, g):
    return x * lax.rsqrt(jnp.mean(jnp.square(x), axis=-1, keepdims=True) + EPS) * g


def _silu(x):
    return x * jax.nn.sigmoid(x)


def _dot(a, b):
    return jnp.dot(a, b, preferred_element_type=F32)


def _dot_nt(a, b):
    return lax.dot_general(a, b, (((1,), (1,)), ((), ())), preferred_element_type=F32)


def _dot_tn(a, b):
    return lax.dot_general(a, b, (((0,), (0,)), ((), ())), preferred_element_type=F32)


def _nbytes(shape, dtype):
    return int(np.prod(shape)) * jnp.dtype(dtype).itemsize


def _vmem_limit(nbytes):
    return int(min(nbytes + (12 << 20), V7X_VMEM_BYTES - (4 << 20)))


def _layer_slab(a, layer, **kw):
    return pl.BlockSpec((None,) + a.shape[1:], lambda *g: (layer,) + (0,) * (a.ndim - 1), **kw)


def _whole(a):
    return pl.BlockSpec(a.shape, lambda *g: (0,) * a.ndim)


_ONCE = dict(pipeline_mode=pl.Buffered(1))


class _Layout:
    def __init__(self, d_model):
        self.mw = 3 * d_model // 8
        self.rw = 3 * d_model // 8
        self.gw = d_model - self.mw - self.rw
        self.md = self.mw // M_HEADS
        self.rd = self.rw // R_HEADS
        assert self.md <= LANES and 2 * self.rd == LANES and self.gw // G_GROUPS * 2 == LANES
        assert self.mw % LANES == 0 and self.rw % LANES == 0 and self.gw % LANES == 0
        self.mpad = M_HEADS * LANES
        self.dmix = self.mw + self.rw + self.gw
        o = 0
        self.mx = o; o += self.mw
        self.mz = o; o += self.mw
        self.rq = o; o += self.rw
        self.rk = o; o += self.rw
        self.rv = o; o += self.rw
        self.rg = o; o += self.rw
        self.gu = o; o += self.gw
        self.gv = o; o += self.gw
        self.gi = o; o += LANES
        self.gf = o; o += LANES
        self.n = o
        o = 0
        self.a_q = o; o += self.mpad
        self.a_k = o; o += self.mpad
        self.a_rq = o; o += self.rw
        self.a_rka = o; o += self.rw
        self.a_rkb = o; o += self.rw
        self.a_kz = o; o += self.rw
        self.a_rva = o; o += self.rw
        self.a_rvb = o; o += self.rw
        self.a_gva = o; o += self.gw
        self.a_gvb = o; o += self.gw
        self.a_zs = o; o += self.mw
        self.a_sxz = o; o += self.mw
        self.a_rgs = o; o += self.rw
        self.a_gug = o; o += self.gw
        self.na = o


RELAYOUT_ROWS = 256


def _relayout_kernel(w_ref, o_ref, *, lay):
    d = w_ref.shape[1]
    o_gate = 2 * lay.mw
    o_rest = o_gate + 2 * M_HEADS
    n_rest = 4 * lay.rw + 2 * lay.gw
    for src, dst, n in ((0, 0, o_gate), (o_rest, lay.rq, n_rest)):
        for r in range(0, n, RELAYOUT_ROWS):
            m = min(RELAYOUT_ROWS, n - r)
            o_ref[dst + r:dst + r + m, :] = w_ref[src + r:src + r + m, :].astype(o_ref.dtype)
    gi = w_ref[o_gate:o_gate + M_HEADS, :]
    gf = w_ref[o_gate + M_HEADS:o_gate + 2 * M_HEADS, :]
    o_ref[lay.gi:lay.gi + LANES, :] = jnp.concatenate(
        [gi, jnp.zeros((LANES - M_HEADS, d), F32)], axis=0).astype(o_ref.dtype)
    o_ref[lay.gf:lay.gf + LANES, :] = jnp.concatenate(
        [gf, gf, jnp.zeros((LANES - 2 * M_HEADS, d), F32)], axis=0).astype(o_ref.dtype)


def _relayout_in_proj_t(w_in, lay):
    depth, d, n_in = w_in.shape
    wt = jnp.swapaxes(w_in, 1, 2)
    est = 2 * n_in * d * 4 + 2 * lay.n * d * 2 + 4 * RELAYOUT_ROWS * d * 4
    return pl.pallas_call(
        functools.partial(_relayout_kernel, lay=lay),
        grid=(depth,),
        in_specs=[pl.BlockSpec((None, n_in, d), lambda l: (l, 0, 0))],
        out_specs=pl.BlockSpec((None, lay.n, d), lambda l: (l, 0, 0)),
        out_shape=jax.ShapeDtypeStruct((depth, lay.n, d), BF16),
        compiler_params=pltpu.CompilerParams(
            dimension_semantics=("arbitrary",), vmem_limit_bytes=_vmem_limit(est)),
        name="relayout_w_in",
    )(wt)


_INPROJ_PARAMS = ("convw", "convb", "bdq", "bdk", "bdvt", "gbi", "gbf", "mskip", "gnorm")
_N_SIDE = 3


def _inproj_kernel(*refs, lay, nj):
    n_p = len(_INPROJ_PARAMS)
    x_ref, g_ref, w_ref, cos_ref, sin_ref, zeta_ref = refs[0:6]
    mp = dict(zip(_INPROJ_PARAMS, refs[6:6 + n_p]))
    wf = refs[6 + n_p:6 + n_p + _N_SIDE]
    act_ref, gates_ref, vt_ref = refs[6 + n_p + _N_SIDE:9 + n_p + _N_SIDE]
    wb = refs[9 + n_p + _N_SIDE:9 + n_p + 2 * _N_SIDE]
    xbuf = refs[9 + n_p + 2 * _N_SIDE]
    tm = x_ref.shape[0]
    for src, dst in zip(wf, wb):
        dst[...] = src[...].astype(dst.dtype)

    md, mw, rw, gw = lay.md, lay.mw, lay.rw, lay.gw
    first_of_seq = lax.rem(pl.program_id(0), nj) == 0

    @pl.when(first_of_seq)
    def _():
        xbuf[0:8, :] = jnp.zeros((8, mw), F32)

    @pl.when(jnp.logical_not(first_of_seq))
    def _():
        xbuf[0:8, :] = xbuf[tm:tm + 8, :]

    proj = _dot_nt(_rms(x_ref[...], g_ref[...]).astype(BF16), w_ref[...])

    def put(col, val):
        act_ref[:, col:col + val.shape[1]] = val.astype(act_ref.dtype)

    mx = proj[:, lay.mx:lay.mx + mw]
    xbuf[8:tm + 8, :] = mx
    conv = mp["convb"][...] + mp["convw"][M_CONV - 1:M_CONV, :] * mx
    for t in range(M_CONV - 1):
        conv = conv + mp["convw"][t:t + 1, :] * xbuf[8 - (M_CONV - 1) + t:8 - (M_CONV - 1) + t + tm, :]
    xc = _silu(conv)
    xcb = xc.astype(BF16)
    put(lay.a_q, _dot(xcb, mp["bdq"][...]))
    put(lay.a_k, _dot(xcb, mp["bdk"][...]) * (md ** -0.5))
    row_p = lax.broadcasted_iota(jnp.int32, (lay.mpad, 1), 0) % LANES
    vt = _dot_nt(mp["bdvt"][...], mx.astype(BF16)) + (row_p == md).astype(F32)
    for c in range(tm // CHUNK):
        vt_ref[c] = vt[:, c * CHUNK:(c + 1) * CHUNK]
    zs = _silu(proj[:, lay.mz:lay.mz + mw])
    put(lay.a_zs, zs)
    put(lay.a_sxz, mp["mskip"][...] * xc * zs)

    li = lax.broadcasted_iota(jnp.int32, (tm, LANES), 1)
    gates_ref[:, 0:LANES] = proj[:, lay.gi:lay.gi + LANES] + mp["gbi"][...]
    gf = proj[:, lay.gf:lay.gf + LANES] + mp["gbf"][...]
    gates_ref[:, LANES:2 * LANES] = jnp.where(li < 2 * M_HEADS, jax.nn.log_sigmoid(gf), 0.0)

    lo_half = li < (LANES // 2)
    half = lay.rd // 2
    first_half = (li % lay.rd) < half
    cs = cos_ref[...]
    sn = sin_ref[...]

    def rotary(v):
        partner = jnp.where(first_half, pltpu.roll(v, LANES - half, axis=1), pltpu.roll(v, half, axis=1))
        return v * cs + partner * sn

    for p_i in range(rw // LANES):
        off = p_i * LANES
        rq = proj[:, lay.rq + off:lay.rq + off + LANES]
        rk = proj[:, lay.rk + off:lay.rk + off + LANES]
        rv = proj[:, lay.rv + off:lay.rv + off + LANES]
        put(lay.a_rq + off, rotary(rq))
        kr = rotary(rk) * (lay.rd ** -0.5)
        put(lay.a_rka + off, jnp.where(lo_half, kr, 0.0))
        put(lay.a_rkb + off, jnp.where(lo_half, 0.0, kr))
        put(lay.a_kz + off, kr * zeta_ref[:, off:off + LANES])
        put(lay.a_rva + off, jnp.where(lo_half, rv, 0.0))
        put(lay.a_rvb + off, jnp.where(lo_half, 0.0, rv))
    put(lay.a_rgs, _silu(proj[:, lay.rg:lay.rg + rw]))

    put(lay.a_gug, jax.nn.gelu(proj[:, lay.gu:lay.gu + gw]))
    gv = jax.nn.gelu(proj[:, lay.gv:lay.gv + gw])
    gvn = gv * lax.rsqrt(jnp.mean(gv * gv, axis=-1, keepdims=True) + EPS) * mp["gnorm"][...]
    for q_i in range(gw // LANES):
        off = q_i * LANES
        put(lay.a_gva + off, jnp.where(lo_half, gvn[:, off:off + LANES], 0.0))
        put(lay.a_gvb + off, jnp.where(lo_half, 0.0, gvn[:, off:off + LANES]))


def _inproj(x2, tabs, lw, dense, layer, lay, seq):
    t, d = x2.shape
    tm = TOKEN_TILE
    nj = seq // tm
    steps = t // tm
    mps = [lw[k] for k in _INPROJ_PARAMS]
    pos = lambda i: (lax.rem(i, nj), 0)
    side = [dense["w_out"], dense["w_ff1"], dense["w_ff2"]]
    assert len(side) == _N_SIDE
    side_in, side_out, side_shape = [], [], []
    for a in side:
        rows, cols = a.shape[1], a.shape[2]
        assert rows % steps == 0 and (rows // steps) % 16 == 0
        side_in.append(pl.BlockSpec((None, rows // steps, cols), lambda i: (layer, i, 0)))
        side_out.append(pl.BlockSpec((rows // steps, cols), lambda i: (i, 0)))
        side_shape.append(jax.ShapeDtypeStruct((rows, cols), BF16))
    est = (2 * tm * d * 4 + d * lay.n * 2 + 2 * tm * lay.na * 2 + 2 * tm * 2 * LANES * 4
           + 2 * tm * lay.mpad * 4 + 6 * tm * LANES * 4 + (tm + 8) * lay.mw * 4 + tm * lay.n * 4
           + 2 * sum(_nbytes(a.shape[1:], a.dtype) for a in mps)
           + 3 * sum(_nbytes(a.shape[1:], a.dtype) // steps for a in side))
    outs = pl.pallas_call(
        functools.partial(_inproj_kernel, lay=lay, nj=nj),
        grid=(steps,),
        in_specs=([pl.BlockSpec((tm, d), lambda i: (i, 0)),
                   _layer_slab(dense["g_mix"], layer),
                   _layer_slab(dense["w_in_t"], layer, **_ONCE),
                   pl.BlockSpec((tm, LANES), pos),
                   pl.BlockSpec((tm, LANES), pos),
                   _whole(tabs["zeta_tile"])]
                  + [_layer_slab(a, layer) for a in mps] + side_in),
        out_specs=[pl.BlockSpec((tm, lay.na), lambda i: (i, 0)),
                   pl.BlockSpec((tm, 2 * LANES), lambda i: (i, 0)),
                   pl.BlockSpec((tm // CHUNK, lay.mpad, CHUNK), lambda i: (i, 0, 0))] + side_out,
        out_shape=[jax.ShapeDtypeStruct((t, lay.na), BF16),
                   jax.ShapeDtypeStruct((t, 2 * LANES), F32),
                   jax.ShapeDtypeStruct((t // CHUNK, lay.mpad, CHUNK), F32)] + side_shape,
        scratch_shapes=[pltpu.VMEM((tm + 8, lay.mw), F32)],
        compiler_params=pltpu.CompilerParams(
            dimension_semantics=("arbitrary",), vmem_limit_bytes=_vmem_limit(est)),
        name="inproj",
    )(x2, dense["g_mix"], dense["w_in_t"], tabs["cos"], tabs["sin"], tabs["zeta_tile"], *mps, *side)
    return outs[0:3], outs[3:6]


_MIXER_PARAMS = ("mnormt", "rnorm", "wcat", "gbmat")
_TABLES = ("decay", "xi", "cdt", "bmaskt")


def _block_diag2(x, y):
    z = jnp.zeros_like(x)
    return jnp.concatenate([jnp.concatenate([x, z], axis=1), jnp.concatenate([z, y], axis=1)], axis=0)


def _mixer_chunk(c, act, gates, vt, mp, tb, out, c_st, r_st, m_st, lay):
    md, mw, rw, gw = lay.md, lay.mw, lay.rw, lay.gw
    li = lax.broadcasted_iota(jnp.int32, (CHUNK, LANES), 1)
    ri = lax.broadcasted_iota(jnp.int32, (CHUNK, LANES), 0)
    causal_t = ri <= li
    tril = (li <= ri).astype(BF16)
    lo_half = li < (LANES // 2)
    rows = pl.ds(pl.multiple_of(c * CHUNK, CHUNK), CHUNK)

    def a(col, width=LANES):
        return act[rows, col:col + width]

    gi = gates[rows, 0:LANES]
    logf = gates[rows, LANES:2 * LANES]
    a1 = logf.astype(BF16)
    r1 = logf - a1.astype(F32)
    a2 = r1.astype(BF16)
    a3 = (r1 - a2.astype(F32)).astype(BF16)
    b3 = _dot(tril, jnp.concatenate([a1, a2, a3], axis=1))
    yield
    bcum = (b3[:, 2 * LANES:3 * LANES] + b3[:, LANES:2 * LANES]) + b3[:, 0:LANES]
    wb = jnp.where(li < M_HEADS, gi - bcum, bcum)
    wb_t = wb.T
    yield

    causal_t2 = jnp.concatenate([causal_t, causal_t], axis=1)
    hn_t = []
    for hp in range(M_HEADS // 2):
        heads = (2 * hp, 2 * hp + 1)
        pair = lambda f: jnp.concatenate([f(h) for h in heads], axis=1)
        rep = lambda v: jnp.broadcast_to(v, (1, LANES))
        q2 = a(lay.a_q + 2 * hp * LANES, 2 * LANES)
        k2 = a(lay.a_k + 2 * hp * LANES, 2 * LANES)
        vt2 = pair(lambda h: vt[c, h * LANES:(h + 1) * LANES, :])
        w_row = pair(lambda h: wb_t[h:h + 1, :])
        b_row = pair(lambda h: wb_t[M_HEADS + h:M_HEADS + h + 1, :])
        b_end = pair(lambda h: rep(wb_t[M_HEADS + h:M_HEADS + h + 1, CHUNK - 1:CHUNK]))
        w_end = b_end + w_row
        a_c = pair(lambda h: rep(jnp.max(w_end[:, (h % 2) * LANES:(h % 2 + 1) * LANES], axis=-1, keepdims=True)))
        e_row = jnp.exp(w_end - a_c)
        m_prev = m_st[hp:hp + 1, :]
        m_new = jnp.maximum(b_end + m_prev, a_c)
        s_old = jnp.exp(b_end + m_prev - m_new)
        s_new = jnp.exp(a_c - m_new)
        m_st[hp:hp + 1, :] = m_new

        il = b_row + m_prev
        c_prev = c_st[hp]
        kq = _dot_nt(jnp.concatenate([k2, c_prev.astype(BF16)], axis=0),
                     _block_diag2(q2[:, 0:LANES], q2[:, LANES:2 * LANES]))
        yield
        dm = jnp.where(causal_t2, pair(lambda h: jnp.broadcast_to(wb[:, h:h + 1], (CHUNK, LANES))) + b_row,
                       -jnp.inf)
        m_t = jnp.maximum(il, jnp.max(dm, axis=0, keepdims=True))
        s_inter = jnp.exp(il - m_t)
        p_t = kq[0:CHUNK] * jnp.exp(dm - m_t)
        inter = s_inter * kq[CHUNK:2 * CHUNK]
        p_b = p_t.astype(BF16)
        tot = _dot(vt2.astype(BF16), _block_diag2(p_b[:, 0:LANES], p_b[:, LANES:2 * LANES])) + inter
        upd = _dot((vt2 * e_row).astype(BF16), _block_diag2(k2[:, 0:LANES], k2[:, LANES:2 * LANES]))
        yield
        den = jnp.sum(p_t, axis=0, keepdims=True) + inter[md:md + 1, :]
        rden = 1.0 / jnp.maximum(jnp.abs(den), jnp.exp(-m_t))
        hout = tot[0:md, :] * rden
        ms = jnp.sum(hout * hout, axis=0, keepdims=True) * (1.0 / md)
        hn = hout * lax.rsqrt(ms + EPS)
        for i, h in enumerate(heads):
            hn_t.append(hn[:, i * LANES:(i + 1) * LANES] * mp["mnormt"][h * md:(h + 1) * md, :])
        c_st[hp] = s_old * c_prev + s_new * upd
        yield

    hm_t = jnp.concatenate(hn_t, axis=0)
    hm = jnp.concatenate([hm_t[t * LANES:(t + 1) * LANES, :].T for t in range(mw // LANES)], axis=1)
    out[rows, 0:mw] = (hm * a(lay.a_zs, mw).astype(F32) + a(lay.a_sxz, mw).astype(F32)).astype(out.dtype)
    yield

    for p_i in range(rw // LANES):
        off = p_i * LANES
        qb = a(lay.a_rq + off)
        rva, rvb = a(lay.a_rva + off), a(lay.a_rvb + off)
        r_prev = r_st[p_i]
        sqr = _dot_nt(qb, jnp.concatenate([a(lay.a_rka + off), a(lay.a_rkb + off), r_prev.astype(BF16)],
                                          axis=0))
        kv = _dot_tn(rva + rvb, a(lay.a_kz + off))
        yield
        p_ab = (sqr[:, 0:2 * LANES] * tb["decay"][p_i]).astype(BF16)
        hr = _dot(p_ab, jnp.concatenate([rva, rvb], axis=0)) + sqr[:, 2 * LANES:3 * LANES] * tb["xi"][p_i]
        r_st[p_i] = tb["cdt"][p_i] * r_prev + tb["bmaskt"][...] * kv
        yield
        sq = hr * hr
        ms_a = jnp.sum(jnp.where(lo_half, sq, 0.0), axis=-1, keepdims=True)
        ms_b = jnp.sum(jnp.where(lo_half, 0.0, sq), axis=-1, keepdims=True)
        ms = jnp.where(lo_half, ms_a, ms_b) * (1.0 / lay.rd)
        hn = hr * lax.rsqrt(ms + EPS) * mp["rnorm"][:, off:off + LANES]
        out[rows, mw + off:mw + off + LANES] = (hn * a(lay.a_rgs + off).astype(F32)).astype(out.dtype)
        yield

    zero = jnp.zeros((2 * CHUNK, LANES), BF16)
    vst = [jnp.concatenate([a(lay.a_gva + q_i * LANES), a(lay.a_gvb + q_i * LANES)], axis=0)
           for q_i in range(gw // LANES)]
    assert len(vst) == 2
    mixed = _dot(mp["wcat"][...], jnp.concatenate([jnp.concatenate([vst[0], zero], axis=1),
                                                   jnp.concatenate([zero, vst[1]], axis=1)], axis=0))
    yield
    out[rows, mw + rw:mw + rw + gw] = (a(lay.a_gug, gw).astype(F32) * (mixed + mp["gbmat"][...])).astype(out.dtype)


def _round_robin(gens):
    live = list(gens)
    while live:
        live = [g for g in live if next(g, live) is not live]


def _mixer_kernel(*refs, lay):
    n_mp, n_tb = len(_MIXER_PARAMS), len(_TABLES)
    act_ref, gates_ref, vt_ref = refs[0:3]
    mp = dict(zip(_MIXER_PARAMS, refs[3:3 + n_mp]))
    tb = dict(zip(_TABLES, refs[3 + n_mp:3 + n_mp + n_tb]))
    o_ref = refs[3 + n_mp + n_tb]
    c_st, r_st, m_st = refs[4 + n_mp + n_tb:]
    nrow, ts = act_ref.shape[0], act_ref.shape[1]

    @pl.when(pl.program_id(1) == 0)
    def _():
        c_st[...] = jnp.zeros_like(c_st)
        r_st[...] = jnp.zeros_like(r_st)
        m_st[...] = jnp.zeros_like(m_st)

    def step(c, carry):
        _round_robin([_mixer_chunk(c, act_ref.at[r], gates_ref.at[r], vt_ref.at[r], mp, tb, o_ref.at[r],
                                   c_st.at[r], r_st.at[r], m_st.at[r], lay) for r in range(nrow)])
        return carry

    lax.fori_loop(0, ts // CHUNK, step, 0)


def _mixer(act, gates, vt, tabs, lw, layer, lay, batch, seq):
    ts, nrow = MIX_TILE, MIX_ROWS
    mps = [lw[k] for k in _MIXER_PARAMS]
    tbs = [tabs[k] for k in _TABLES]
    scratch = [pltpu.VMEM((nrow, M_HEADS // 2, LANES, 2 * LANES), F32),
               pltpu.VMEM((nrow, lay.rw // LANES, LANES, LANES), F32),
               pltpu.VMEM((nrow, 8, 2 * LANES), F32)]
    est = (2 * nrow * ts * (lay.na * 2 + 2 * LANES * 4 + lay.mpad * 4 + lay.dmix * 2)
           + 2 * sum(_nbytes(a.shape[1:], a.dtype) for a in mps) + 2 * sum(_nbytes(a.shape, a.dtype) for a in tbs)
           + sum(_nbytes(s.shape, s.dtype) for s in scratch))
    return pl.pallas_call(
        functools.partial(_mixer_kernel, lay=lay),
        grid=(batch // nrow, seq // ts),
        in_specs=([pl.BlockSpec((nrow, ts, lay.na), lambda b, j: (b, j, 0)),
                   pl.BlockSpec((nrow, ts, 2 * LANES), lambda b, j: (b, j, 0)),
                   pl.BlockSpec((nrow, ts // CHUNK, lay.mpad, CHUNK), lambda b, j: (b, j, 0, 0))]
                  + [_layer_slab(a, layer) for a in mps] + [_whole(a) for a in tbs]),
        out_specs=pl.BlockSpec((nrow, ts, lay.dmix), lambda b, j: (b, j, 0)),
        out_shape=jax.ShapeDtypeStruct((batch, seq, lay.dmix), BF16),
        scratch_shapes=scratch,
        compiler_params=pltpu.CompilerParams(
            dimension_semantics=("arbitrary", "arbitrary"), vmem_limit_bytes=_vmem_limit(est)),
        name="mixer",
    )(act, gates, vt, *mps, *tbs)


def _ffn_kernel(mix_ref, x_ref, wo_ref, g_ref, w1_ref, w2_ref, gf_ref, o_ref, *, final):
    x1 = x_ref[...] + _dot(mix_ref[...], wo_ref[...])
    h2 = _rms(x1, g_ref[...]).astype(BF16)
    acc = x1
    for j in range(w1_ref.shape[1] // FF_CHUNK):
        cols = slice(j * FF_CHUNK, (j + 1) * FF_CHUNK)
        a = jnp.square(jnp.maximum(_dot(h2, w1_ref[:, cols]), 0.0)).astype(BF16)
        acc = acc + _dot(a, w2_ref[cols, :])
    if final:
        acc = _rms(acc, gf_ref[...])
    o_ref[...] = acc


def _ffn(mix, x2, wts, dense, layer, final):
    t, d = x2.shape
    wo, w1, w2 = wts
    dff = w1.shape[1]
    tm = FFN_TILE
    est = (2 * tm * d * 2 + 4 * tm * d * 4 + (d * d + 2 * d * dff) * 2
           + tm * FF_CHUNK * 6 + 3 * tm * d * 4)
    resident = lambda a: pl.BlockSpec(a.shape, lambda i: (0, 0), **_ONCE)
    return pl.pallas_call(
        functools.partial(_ffn_kernel, final=final),
        grid=(t // tm,),
        in_specs=[pl.BlockSpec((tm, d), lambda i: (i, 0)),
                  pl.BlockSpec((tm, d), lambda i: (i, 0)),
                  resident(wo),
                  _layer_slab(dense["g_ff"], layer),
                  resident(w1),
                  resident(w2),
                  pl.BlockSpec((1, d), lambda i: (0, 0))],
        out_specs=pl.BlockSpec((tm, d), lambda i: (i, 0)),
        out_shape=jax.ShapeDtypeStruct((t, d), F32),
        compiler_params=pltpu.CompilerParams(
            dimension_semantics=("arbitrary",), vmem_limit_bytes=_vmem_limit(est)),
        name="ffn_final" if final else "ffn",
    )(mix, x2, wo, dense["g_ff"], w1, w2, dense["g_final"])


def _tables(lay, seq):
    rd, half = lay.rd, lay.rd // 2
    l = CHUNK
    pos = jnp.arange(seq, dtype=F32)
    freqs = ROPE_THETA ** (-jnp.arange(half, dtype=F32) / half)
    ang = pos[:, None] * freqs[None, :]
    sign = jnp.asarray(np.where(np.arange(LANES) % rd < half, -1.0, 1.0), F32)
    cos = jnp.tile(jnp.cos(ang), (1, LANES // half))
    sin = jnp.tile(jnp.sin(ang), (1, LANES // half)) * sign[None, :]
    log_gamma = jnp.log(1.0 - 2.0 ** (-5.0 - jnp.arange(R_HEADS, dtype=F32)))
    p = jnp.arange(l, dtype=F32)
    causal = jnp.tril(jnp.ones((l, l), dtype=bool))
    rel = jnp.where(causal, p[:, None] - p[None, :], 0.0)
    decay = jnp.where(causal, jnp.exp(rel[None] * log_gamma[:, None, None]), 0.0)
    zeta = jnp.exp((l - 1.0 - p)[None, :] * log_gamma[:, None])
    xi = jnp.exp((p + 1.0)[None, :] * log_gamma[:, None])
    cdec = jnp.exp(l * log_gamma)
    npair = R_HEADS // 2
    decay_ab = decay.reshape(npair, 2, l, l).transpose(0, 2, 1, 3).reshape(npair, l, 2 * l)
    by_lane = lambda a: jnp.repeat(a.reshape(npair, 2, -1), rd, axis=1)
    xi_ab = by_lane(xi).transpose(0, 2, 1)
    zeta_k = by_lane(zeta).transpose(0, 2, 1)
    zeta_tile = jnp.tile(zeta_k.transpose(1, 0, 2).reshape(l, npair * LANES), (TOKEN_TILE // l, 1))
    cdt = jnp.broadcast_to(by_lane(cdec).transpose(0, 2, 1), (npair, LANES, LANES))
    head = np.arange(LANES) // rd
    bmaskt = jnp.asarray((head[:, None] == head[None, :]).astype(np.float32))
    return {"cos": cos, "sin": sin, "decay": decay_ab, "xi": xi_ab, "zeta_tile": zeta_tile, "cdt": cdt,
            "bmaskt": bmaskt}


def _block_diag_padded(w, lay):
    depth, nb = w.shape[0], w.shape[1]
    rows = w.reshape(depth, nb * M_BLOCK, M_BLOCK)
    tiled = jnp.tile(rows, (1, 1, lay.mpad // M_BLOCK))
    r = np.arange(nb * M_BLOCK)[:, None]
    c = np.arange(lay.mpad)[None, :]
    packed_c = (c // LANES) * lay.md + c % LANES
    keep = (c % LANES < lay.md) & (r // M_BLOCK == packed_c // M_BLOCK)
    return jnp.where(jnp.asarray(keep), tiled, 0.0).astype(BF16)


def _mixer_weights(lay, p):
    depth = p["m_wq"].shape[0]
    gbi = jnp.concatenate([p["m_i_bias"], jnp.zeros((depth, LANES - M_HEADS), F32)], axis=-1)
    gbf = jnp.concatenate(
        [p["m_f_bias"], p["m_f_bias"], jnp.zeros((depth, LANES - 2 * M_HEADS), F32)], axis=-1)
    causal = jnp.tril(jnp.ones((CHUNK, CHUNK), dtype=bool))
    wm = jnp.where(causal, p["g_ws"], 0.0)
    wcat = wm.transpose(0, 2, 1, 3).reshape(depth, CHUNK, G_GROUPS * CHUNK)
    gbmat = jnp.repeat(p["g_bs"].transpose(0, 2, 1), lay.gw // G_GROUPS, axis=2)
    row = lambda a: a[:, None, :]
    return {
        "convw": p["m_conv_w"], "convb": row(p["m_conv_b"]),
        "bdq": _block_diag_padded(p["m_wq"], lay),
        "bdk": _block_diag_padded(p["m_wk"], lay),
        "bdvt": _block_diag_padded(p["m_wv"], lay).transpose(0, 2, 1),
        "gbi": row(gbi), "gbf": row(gbf),
        "mnormt": jnp.broadcast_to(p["m_norm_g"][:, :, None], (depth, lay.mw, LANES)),
        "mskip": row(p["m_skip"]),
        "rnorm": row(p["r_norm_g"]),
        "gnorm": row(p["g_norm_g"]),
        "wcat": wcat.astype(BF16), "gbmat": gbmat,
    }


def kernel(x, norm_mix_g, w_in, m_conv_w, m_conv_b, m_wq, m_wk, m_wv, m_i_bias, m_f_bias, m_norm_g,
           m_skip, r_norm_g, g_norm_g, g_ws, g_bs, w_out, norm_ff_g, w_ff1, w_ff2, final_norm_g):
    batch, seq, d = x.shape
    depth = w_in.shape[0]
    lay = _Layout(d)
    assert seq % MIX_TILE == 0 and MIX_TILE % CHUNK == 0 and batch % MIX_ROWS == 0
    assert seq % TOKEN_TILE == 0 and TOKEN_TILE % CHUNK == 0 and w_ff1.shape[2] % FF_CHUNK == 0
    assert (batch * seq) % FFN_TILE == 0
    params = dict(m_conv_w=m_conv_w, m_conv_b=m_conv_b, m_wq=m_wq, m_wk=m_wk, m_wv=m_wv,
                  m_i_bias=m_i_bias, m_f_bias=m_f_bias, m_norm_g=m_norm_g, m_skip=m_skip,
                  r_norm_g=r_norm_g, g_norm_g=g_norm_g, g_ws=g_ws, g_bs=g_bs)
    tabs = _tables(lay, seq)
    lw = _mixer_weights(lay, params)
    dense = {
        "w_in_t": _relayout_in_proj_t(w_in, lay), "w_out": w_out, "w_ff1": w_ff1, "w_ff2": w_ff2,
        "g_mix": norm_mix_g[:, None, :], "g_ff": norm_ff_g[:, None, :], "g_final": final_norm_g[None, :],
    }
    x2 = x.reshape(batch * seq, d)
    for layer in range(depth):
        (act, gates, vt), ffn_w = _inproj(x2, tabs, lw, dense, layer, lay, seq)
        mix = _mixer(act.reshape(batch, seq, lay.na), gates.reshape(batch, seq, 2 * LANES),
                     vt.reshape(batch, seq // CHUNK, lay.mpad, CHUNK), tabs, lw, layer, lay, batch, seq)
        x2 = _ffn(mix.reshape(batch * seq, lay.dmix), x2, ffn_w, dense, layer, final=(layer == depth - 1))
    return x2.reshape(batch, seq, d)
```

```python
import functools

import numpy as np
import jax
import jax.numpy as jnp
from jax import lax
from jax.experimental import pallas as pl
from jax.experimental.pallas import tpu as pltpu

M_HEADS = 4
M_CONV = 4
M_BLOCK = 4
R_HEADS = 6
G_GROUPS = 4
CHUNK = 128
ROPE_THETA = 10000.0
EPS = 1e-6

LANES = 128
V7X_VMEM_BYTES = 64 * 1024 * 1024

TOKEN_TILE = 512
TOKEN_SPLIT = 2
FFN_TILE = 1024
MIX_TILE = 256
MIX_ROWS = 4
FF_CHUNK = 1024

F32 = jnp.float32
BF16 = jnp.bfloat16


def _rms(x, g):
    return x * lax.rsqrt(jnp.mean(jnp.square(x), axis=-1, keepdims=True) + EPS) * g


def _silu(x):
    return x * jax.nn.sigmoid(x)


def _dot(a, b):
    return jnp.dot(a, b, preferred_element_type=F32)


def _dot_nt(a, b):
    return lax.dot_general(a, b, (((1,), (1,)), ((), ())), preferred_element_type=F32)


def _dot_tn(a, b):
    return lax.dot_general(a, b, (((0,), (0,)), ((), ())), preferred_element_type=F32)


def _nbytes(shape, dtype):
    return int(np.prod(shape)) * jnp.dtype(dtype).itemsize


def _vmem_limit(nbytes):
    return int(min(nbytes + (12 << 20), V7X_VMEM_BYTES - (4 << 20)))


def _layer_slab(a, layer, **kw):
    return pl.BlockSpec((None,) + a.shape[1:], lambda *g: (layer,) + (0,) * (a.ndim - 1), **kw)


def _whole(a):
    return pl.BlockSpec(a.shape, lambda *g: (0,) * a.ndim)


_ONCE = dict(pipeline_mode=pl.Buffered(1))


class _Layout:
    def __init__(self, d_model):
        self.mw = 3 * d_model // 8
        self.rw = 3 * d_model // 8
        self.gw = d_model - self.mw - self.rw
        self.md = self.mw // M_HEADS
        self.rd = self.rw // R_HEADS
        assert self.md <= LANES and 2 * self.rd == LANES and self.gw // G_GROUPS * 2 == LANES
        assert self.mw % LANES == 0 and self.rw % LANES == 0 and self.gw % LANES == 0
        self.mpad = M_HEADS * LANES
        self.dmix = self.mw + self.rw + self.gw
        o = 0
        self.mx = o; o += self.mw
        self.mz = o; o += self.mw
        self.gu = o; o += self.gw
        self.gv = o; o += self.gw
        self.gi = o; o += LANES
        self.gf = o; o += LANES
        self.rq = o; o += self.rw
        self.rk = o; o += self.rw
        self.rg = o; o += self.rw
        self.rv = o; o += self.rw
        self.n = o
        o = 0
        self.a_q = o; o += self.mpad
        self.a_k = o; o += self.mpad
        self.a_rq = o; o += self.rw
        self.a_rka = o; o += self.rw
        self.a_rkb = o; o += self.rw
        self.a_kz = o; o += self.rw
        self.a_rva = o; o += self.rw
        self.a_rvb = o; o += self.rw
        self.a_gva = o; o += self.gw
        self.a_gvb = o; o += self.gw
        self.a_zs = o; o += self.mw
        self.a_sxz = o; o += self.mw
        self.a_rgs = o; o += self.rw
        self.a_gug = o; o += self.gw
        self.na = o


RELAYOUT_ROWS = 256


def _relayout_kernel(w_ref, o_ref, *, lay):
    d = w_ref.shape[1]
    mw, rw, gw = lay.mw, lay.rw, lay.gw
    o_gate = 2 * mw
    o_rq = o_gate + 2 * M_HEADS
    pieces = ((0, lay.mx, 2 * mw), (o_rq, lay.rq, rw), (o_rq + rw, lay.rk, rw), (o_rq + 2 * rw, lay.rv, rw),
              (o_rq + 3 * rw, lay.rg, rw), (o_rq + 4 * rw, lay.gu, gw), (o_rq + 4 * rw + gw, lay.gv, gw))
    for src, dst, n in pieces:
        for r in range(0, n, RELAYOUT_ROWS):
            m = min(RELAYOUT_ROWS, n - r)
            o_ref[dst + r:dst + r + m, :] = w_ref[src + r:src + r + m, :].astype(o_ref.dtype)
    gi = w_ref[o_gate:o_gate + M_HEADS, :]
    gf = w_ref[o_gate + M_HEADS:o_gate + 2 * M_HEADS, :]
    o_ref[lay.gi:lay.gi + LANES, :] = jnp.concatenate(
        [gi, jnp.zeros((LANES - M_HEADS, d), F32)], axis=0).astype(o_ref.dtype)
    o_ref[lay.gf:lay.gf + LANES, :] = jnp.concatenate(
        [gf, gf, jnp.zeros((LANES - 2 * M_HEADS, d), F32)], axis=0).astype(o_ref.dtype)


def _relayout_in_proj_t(w_in, lay):
    depth, d, n_in = w_in.shape
    wt = jnp.swapaxes(w_in, 1, 2)
    est = 2 * n_in * d * 4 + 2 * lay.n * d * 2 + 4 * RELAYOUT_ROWS * d * 4
    return pl.pallas_call(
        functools.partial(_relayout_kernel, lay=lay),
        grid=(depth,),
        in_specs=[pl.BlockSpec((None, n_in, d), lambda l: (l, 0, 0))],
        out_specs=pl.BlockSpec((None, lay.n, d), lambda l: (l, 0, 0)),
        out_shape=jax.ShapeDtypeStruct((depth, lay.n, d), BF16),
        compiler_params=pltpu.CompilerParams(
            dimension_semantics=("arbitrary",), vmem_limit_bytes=_vmem_limit(est)),
        name="relayout_w_in",
    )(wt)


_INPROJ_PARAMS = ("convw", "convb", "bdq", "bdk", "bdvt", "gbi", "gbf", "mskip", "gnorm")
_N_SIDE = 3


def _inproj_kernel(*refs, lay, nj):
    n_p = len(_INPROJ_PARAMS)
    x_ref, g_ref, w_ref, cos_ref, sin_ref, zeta_ref = refs[0:6]
    mp = dict(zip(_INPROJ_PARAMS, refs[6:6 + n_p]))
    wf = refs[6 + n_p:6 + n_p + _N_SIDE]
    act_ref, gates_ref, vt_ref = refs[6 + n_p + _N_SIDE:9 + n_p + _N_SIDE]
    wb = refs[9 + n_p + _N_SIDE:9 + n_p + 2 * _N_SIDE]
    xbuf = refs[9 + n_p + 2 * _N_SIDE]
    tm = x_ref.shape[0]
    for src, dst in zip(wf, wb):
        dst[...] = src[...].astype(dst.dtype)

    md, mw, rw, gw = lay.md, lay.mw, lay.rw, lay.gw
    first_of_seq = lax.rem(pl.program_id(0), nj) == 0

    @pl.when(first_of_seq)
    def _():
        xbuf[0:8, :] = jnp.zeros((8, mw), F32)

    @pl.when(jnp.logical_not(first_of_seq))
    def _():
        xbuf[0:8, :] = xbuf[tm:tm + 8, :]

    hm = tm // TOKEN_SPLIT
    assert hm % CHUNK == 0
    _round_robin([_inproj_rows(r0, hm, x_ref, g_ref, w_ref, cos_ref, sin_ref, zeta_ref, mp,
                               act_ref, gates_ref, vt_ref, xbuf, lay) for r0 in range(0, tm, hm)])


def _inproj_rows(r0, nr, x_ref, g_ref, w_ref, cos_ref, sin_ref, zeta_ref, mp, act_ref, gates_ref, vt_ref,
                 xbuf, lay):
    md, mw, rw, gw = lay.md, lay.mw, lay.rw, lay.gw
    rs = slice(r0, r0 + nr)

    proj = _dot_nt(_rms(x_ref[rs, :], g_ref[...]).astype(BF16), w_ref[...])
    yield

    def put(col, val):
        act_ref[rs, col:col + val.shape[1]] = val.astype(act_ref.dtype)

    mx = proj[:, lay.mx:lay.mx + mw]
    xbuf[8 + r0:8 + r0 + nr, :] = mx
    conv = mp["convb"][...] + mp["convw"][M_CONV - 1:M_CONV, :] * mx
    for t in range(M_CONV - 1):
        lo = 8 + r0 - (M_CONV - 1) + t
        conv = conv + mp["convw"][t:t + 1, :] * xbuf[lo:lo + nr, :]
    xc = _silu(conv)
    xcb = xc.astype(BF16)
    yield
    q = _dot(xcb, mp["bdq"][...])
    k = _dot(xcb, mp["bdk"][...])
    row_p = lax.broadcasted_iota(jnp.int32, (lay.mpad, 1), 0) % LANES
    vt = _dot_nt(mp["bdvt"][...], mx.astype(BF16)) + (row_p == md).astype(F32)
    yield
    put(lay.a_q, q)
    put(lay.a_k, k * (md ** -0.5))
    for c in range(nr // CHUNK):
        vt_ref[r0 // CHUNK + c] = vt[:, c * CHUNK:(c + 1) * CHUNK]
    zs = _silu(proj[:, lay.mz:lay.mz + mw])
    put(lay.a_zs, zs)
    put(lay.a_sxz, mp["mskip"][...] * xc * zs)
    yield

    li = lax.broadcasted_iota(jnp.int32, (nr, LANES), 1)
    lo_half = li < (LANES // 2)
    put(lay.a_gug, jax.nn.gelu(proj[:, lay.gu:lay.gu + gw]))
    yield
    gv = jax.nn.gelu(proj[:, lay.gv:lay.gv + gw])
    gvn = gv * lax.rsqrt(jnp.mean(gv * gv, axis=-1, keepdims=True) + EPS) * mp["gnorm"][...]
    for q_i in range(gw // LANES):
        off = q_i * LANES
        put(lay.a_gva + off, jnp.where(lo_half, gvn[:, off:off + LANES], 0.0))
        put(lay.a_gvb + off, jnp.where(lo_half, 0.0, gvn[:, off:off + LANES]))
    yield

    gates_ref[rs, 0:LANES] = proj[:, lay.gi:lay.gi + LANES] + mp["gbi"][...]
    gf = proj[:, lay.gf:lay.gf + LANES] + mp["gbf"][...]
    gates_ref[rs, LANES:2 * LANES] = jnp.where(li < 2 * M_HEADS, jax.nn.log_sigmoid(gf), 0.0)
    yield

    half = lay.rd // 2
    first_half = (li % lay.rd) < half
    cs = cos_ref[rs, :]
    sn = sin_ref[rs, :]

    def rotary(v):
        partner = jnp.where(first_half, pltpu.roll(v, LANES - half, axis=1), pltpu.roll(v, half, axis=1))
        return v * cs + partner * sn

    for p_i in range(rw // LANES):
        off = p_i * LANES
        rq = proj[:, lay.rq + off:lay.rq + off + LANES]
        rk = proj[:, lay.rk + off:lay.rk + off + LANES]
        put(lay.a_rq + off, rotary(rq))
        kr = rotary(rk) * (lay.rd ** -0.5)
        put(lay.a_rka + off, jnp.where(lo_half, kr, 0.0))
        put(lay.a_rkb + off, jnp.where(lo_half, 0.0, kr))
        put(lay.a_kz + off, kr * zeta_ref[rs, off:off + LANES])
        yield
    put(lay.a_rgs, _silu(proj[:, lay.rg:lay.rg + rw]))
    yield
    for p_i in range(rw // LANES):
        off = p_i * LANES
        rv = proj[:, lay.rv + off:lay.rv + off + LANES]
        put(lay.a_rva + off, jnp.where(lo_half, rv, 0.0))
        put(lay.a_rvb + off, jnp.where(lo_half, 0.0, rv))


def _inproj(x2, tabs, lw, dense, layer, lay, seq):
    t, d = x2.shape
    tm = TOKEN_TILE
    nj = seq // tm
    steps = t // tm
    mps = [lw[k] for k in _INPROJ_PARAMS]
    pos = lambda i: (lax.rem(i, nj), 0)
    side = [dense["w_out"], dense["w_ff1"], dense["w_ff2"]]
    assert len(side) == _N_SIDE
    side_in, side_out, side_shape = [], [], []
    for a in side:
        rows, cols = a.shape[1], a.shape[2]
        assert rows % steps == 0 and (rows // steps) % 16 == 0
        side_in.append(pl.BlockSpec((None, rows // steps, cols), lambda i: (layer, i, 0)))
        side_out.append(pl.BlockSpec((rows // steps, cols), lambda i: (i, 0)))
        side_shape.append(jax.ShapeDtypeStruct((rows, cols), BF16))
    est = (2 * tm * d * 4 + d * lay.n * 2 + 2 * tm * lay.na * 2 + 2 * tm * 2 * LANES * 4
           + 2 * tm * lay.mpad * 4 + 6 * tm * LANES * 4 + (tm + 8) * lay.mw * 4 + tm * lay.n * 4
           + 2 * sum(_nbytes(a.shape[1:], a.dtype) for a in mps)
           + 3 * sum(_nbytes(a.shape[1:], a.dtype) // steps for a in side))
    outs = pl.pallas_call(
        functools.partial(_inproj_kernel, lay=lay, nj=nj),
        grid=(steps,),
        in_specs=([pl.BlockSpec((tm, d), lambda i: (i, 0)),
                   _layer_slab(dense["g_mix"], layer),
                   _layer_slab(dense["w_in_t"], layer, **_ONCE),
                   pl.BlockSpec((tm, LANES), pos),
                   pl.BlockSpec((tm, LANES), pos),
                   _whole(tabs["zeta_tile"])]
                  + [_layer_slab(a, layer) for a in mps] + side_in),
        out_specs=[pl.BlockSpec((tm, lay.na), lambda i: (i, 0)),
                   pl.BlockSpec((tm, 2 * LANES), lambda i: (i, 0)),
                   pl.BlockSpec((tm // CHUNK, lay.mpad, CHUNK), lambda i: (i, 0, 0))] + side_out,
        out_shape=[jax.ShapeDtypeStruct((t, lay.na), BF16),
                   jax.ShapeDtypeStruct((t, 2 * LANES), F32),
                   jax.ShapeDtypeStruct((t // CHUNK, lay.mpad, CHUNK), F32)] + side_shape,
        scratch_shapes=[pltpu.VMEM((tm + 8, lay.mw), F32)],
        compiler_params=pltpu.CompilerParams(
            dimension_semantics=("arbitrary",), vmem_limit_bytes=_vmem_limit(est)),
        name="inproj",
    )(x2, dense["g_mix"], dense["w_in_t"], tabs["cos"], tabs["sin"], tabs["zeta_tile"], *mps, *side)
    return outs[0:3], outs[3:6]


_MIXER_PARAMS = ("mnormt", "rnorm", "wcat", "gbmat")
_TABLES = ("decay", "xi", "cdt", "bmaskt")


def _block_diag2(x, y):
    z = jnp.zeros_like(x)
    return jnp.concatenate([jnp.concatenate([x, z], axis=1), jnp.concatenate([z, y], axis=1)], axis=0)


def _mixer_chunk(c, act, gates, vt, mp, tb, out, c_st, r_st, m_st, lay):
    md, mw, rw, gw = lay.md, lay.mw, lay.rw, lay.gw
    li = lax.broadcasted_iota(jnp.int32, (CHUNK, LANES), 1)
    ri = lax.broadcasted_iota(jnp.int32, (CHUNK, LANES), 0)
    causal_t = ri <= li
    tril = (li <= ri).astype(BF16)
    lo_half = li < (LANES // 2)
    rows = pl.ds(pl.multiple_of(c * CHUNK, CHUNK), CHUNK)

    def a(col, width=LANES):
        return act[rows, col:col + width]

    gi = gates[rows, 0:LANES]
    logf = gates[rows, LANES:2 * LANES]
    a1 = logf.astype(BF16)
    r1 = logf - a1.astype(F32)
    a2 = r1.astype(BF16)
    a3 = (r1 - a2.astype(F32)).astype(BF16)
    b3 = _dot(tril, jnp.concatenate([a1, a2, a3], axis=1))
    yield
    bcum = (b3[:, 2 * LANES:3 * LANES] + b3[:, LANES:2 * LANES]) + b3[:, 0:LANES]
    wb = jnp.where(li < M_HEADS, gi - bcum, bcum)
    wb_t = wb.T
    yield

    causal_t2 = jnp.concatenate([causal_t, causal_t], axis=1)
    hn_t = []
    for hp in range(M_HEADS // 2):
        heads = (2 * hp, 2 * hp + 1)
        pair = lambda f: jnp.concatenate([f(h) for h in heads], axis=1)
        rep = lambda v: jnp.broadcast_to(v, (1, LANES))
        q2 = a(lay.a_q + 2 * hp * LANES, 2 * LANES)
        k2 = a(lay.a_k + 2 * hp * LANES, 2 * LANES)
        vt2 = pair(lambda h: vt[c, h * LANES:(h + 1) * LANES, :])
        w_row = pair(lambda h: wb_t[h:h + 1, :])
        b_row = pair(lambda h: wb_t[M_HEADS + h:M_HEADS + h + 1, :])
        b_end = pair(lambda h: rep(wb_t[M_HEADS + h:M_HEADS + h + 1, CHUNK - 1:CHUNK]))
        w_end = b_end + w_row
        a_c = pair(lambda h: rep(jnp.max(w_end[:, (h % 2) * LANES:(h % 2 + 1) * LANES], axis=-1, keepdims=True)))
        e_row = jnp.exp(w_end - a_c)
        m_prev = m_st[hp:hp + 1, :]
        m_new = jnp.maximum(b_end + m_prev, a_c)
        s_old = jnp.exp(b_end + m_prev - m_new)
        s_new = jnp.exp(a_c - m_new)
        m_st[hp:hp + 1, :] = m_new

        il = b_row + m_prev
        c_prev = c_st[hp]
        kq = _dot_nt(jnp.concatenate([k2, c_prev.astype(BF16)], axis=0),
                     _block_diag2(q2[:, 0:LANES], q2[:, LANES:2 * LANES]))
        yield
        dm = jnp.where(causal_t2, pair(lambda h: jnp.broadcast_to(wb[:, h:h + 1], (CHUNK, LANES))) + b_row,
                       -jnp.inf)
        m_t = jnp.maximum(il, jnp.max(dm, axis=0, keepdims=True))
        s_inter = jnp.exp(il - m_t)
        p_t = kq[0:CHUNK] * jnp.exp(dm - m_t)
        inter = s_inter * kq[CHUNK:2 * CHUNK]
        p_b = p_t.astype(BF16)
        tot = _dot(vt2.astype(BF16), _block_diag2(p_b[:, 0:LANES], p_b[:, LANES:2 * LANES])) + inter
        upd = _dot((vt2 * e_row).astype(BF16), _block_diag2(k2[:, 0:LANES], k2[:, LANES:2 * LANES]))
        yield
        den = jnp.sum(p_t, axis=0, keepdims=True) + inter[md:md + 1, :]
        rden = 1.0 / jnp.maximum(jnp.abs(den), jnp.exp(-m_t))
        hout = tot[0:md, :] * rden
        ms = jnp.sum(hout * hout, axis=0, keepdims=True) * (1.0 / md)
        hn = hout * lax.rsqrt(ms + EPS)
        for i, h in enumerate(heads):
            hn_t.append(hn[:, i * LANES:(i + 1) * LANES] * mp["mnormt"][h * md:(h + 1) * md, :])
        c_st[hp] = s_old * c_prev + s_new * upd
        yield

    hm_t = jnp.concatenate(hn_t, axis=0)
    hm = jnp.concatenate([hm_t[t * LANES:(t + 1) * LANES, :].T for t in range(mw // LANES)], axis=1)
    out[rows, 0:mw] = (hm * a(lay.a_zs, mw).astype(F32) + a(lay.a_sxz, mw).astype(F32)).astype(out.dtype)
    yield

    for p_i in range(rw // LANES):
        off = p_i * LANES
        qb = a(lay.a_rq + off)
        rva, rvb = a(lay.a_rva + off), a(lay.a_rvb + off)
        r_prev = r_st[p_i]
        sqr = _dot_nt(qb, jnp.concatenate([a(lay.a_rka + off), a(lay.a_rkb + off), r_prev.astype(BF16)],
                                          axis=0))
        kv = _dot_tn(rva + rvb, a(lay.a_kz + off))
        yield
        p_ab = (sqr[:, 0:2 * LANES] * tb["decay"][p_i]).astype(BF16)
        hr = _dot(p_ab, jnp.concatenate([rva, rvb], axis=0)) + sqr[:, 2 * LANES:3 * LANES] * tb["xi"][p_i]
        r_st[p_i] = tb["cdt"][p_i] * r_prev + tb["bmaskt"][...] * kv
        yield
        sq = hr * hr
        ms_a = jnp.sum(jnp.where(lo_half, sq, 0.0), axis=-1, keepdims=True)
        ms_b = jnp.sum(jnp.where(lo_half, 0.0, sq), axis=-1, keepdims=True)
        ms = jnp.where(lo_half, ms_a, ms_b) * (1.0 / lay.rd)
        hn = hr * lax.rsqrt(ms + EPS) * mp["rnorm"][:, off:off + LANES]
        out[rows, mw + off:mw + off + LANES] = (hn * a(lay.a_rgs + off).astype(F32)).astype(out.dtype)
        yield

    zero = jnp.zeros((2 * CHUNK, LANES), BF16)
    vst = [jnp.concatenate([a(lay.a_gva + q_i * LANES), a(lay.a_gvb + q_i * LANES)], axis=0)
           for q_i in range(gw // LANES)]
    assert len(vst) == 2
    mixed = _dot(mp["wcat"][...], jnp.concatenate([jnp.concatenate([vst[0], zero], axis=1),
                                                   jnp.concatenate([zero, vst[1]], axis=1)], axis=0))
    yield
    out[rows, mw + rw:mw + rw + gw] = (a(lay.a_gug, gw).astype(F32) * (mixed + mp["gbmat"][...])).astype(out.dtype)


def _round_robin(gens):
    live = list(gens)
    while live:
        live = [g for g in live if next(g, live) is not live]


def _mixer_kernel(*refs, lay):
    n_mp, n_tb = len(_MIXER_PARAMS), len(_TABLES)
    act_ref, gates_ref, vt_ref = refs[0:3]
    mp = dict(zip(_MIXER_PARAMS, refs[3:3 + n_mp]))
    tb = dict(zip(_TABLES, refs[3 + n_mp:3 + n_mp + n_tb]))
    o_ref = refs[3 + n_mp + n_tb]
    c_st, r_st, m_st = refs[4 + n_mp + n_tb:]
    nrow, ts = act_ref.shape[0], act_ref.shape[1]

    @pl.when(pl.program_id(1) == 0)
    def _():
        c_st[...] = jnp.zeros_like(c_st)
        r_st[...] = jnp.zeros_like(r_st)
        m_st[...] = jnp.zeros_like(m_st)

    def step(c, carry):
        _round_robin([_mixer_chunk(c, act_ref.at[r], gates_ref.at[r], vt_ref.at[r], mp, tb, o_ref.at[r],
                                   c_st.at[r], r_st.at[r], m_st.at[r], lay) for r in range(nrow)])
        return carry

    lax.fori_loop(0, ts // CHUNK, step, 0)


def _mixer(act, gates, vt, tabs, lw, layer, lay, batch, seq):
    ts, nrow = MIX_TILE, MIX_ROWS
    mps = [lw[k] for k in _MIXER_PARAMS]
    tbs = [tabs[k] for k in _TABLES]
    scratch = [pltpu.VMEM((nrow, M_HEADS // 2, LANES, 2 * LANES), F32),
               pltpu.VMEM((nrow, lay.rw // LANES, LANES, LANES), F32),
               pltpu.VMEM((nrow, 8, 2 * LANES), F32)]
    est = (2 * nrow * ts * (lay.na * 2 + 2 * LANES * 4 + lay.mpad * 4 + lay.dmix * 2)
           + 2 * sum(_nbytes(a.shape[1:], a.dtype) for a in mps) + 2 * sum(_nbytes(a.shape, a.dtype) for a in tbs)
           + sum(_nbytes(s.shape, s.dtype) for s in scratch))
    return pl.pallas_call(
        functools.partial(_mixer_kernel, lay=lay),
        grid=(batch // nrow, seq // ts),
        in_specs=([pl.BlockSpec((nrow, ts, lay.na), lambda b, j: (b, j, 0)),
                   pl.BlockSpec((nrow, ts, 2 * LANES), lambda b, j: (b, j, 0)),
                   pl.BlockSpec((nrow, ts // CHUNK, lay.mpad, CHUNK), lambda b, j: (b, j, 0, 0))]
                  + [_layer_slab(a, layer) for a in mps] + [_whole(a) for a in tbs]),
        out_specs=pl.BlockSpec((nrow, ts, lay.dmix), lambda b, j: (b, j, 0)),
        out_shape=jax.ShapeDtypeStruct((batch, seq, lay.dmix), BF16),
        scratch_shapes=scratch,
        compiler_params=pltpu.CompilerParams(
            dimension_semantics=("arbitrary", "arbitrary"), vmem_limit_bytes=_vmem_limit(est)),
        name="mixer",
    )(act, gates, vt, *mps, *tbs)


def _ffn_kernel(mix_ref, x_ref, wo_ref, g_ref, w1_ref, w2_ref, gf_ref, o_ref, *, final):
    x1 = x_ref[...] + _dot(mix_ref[...], wo_ref[...])
    h2 = _rms(x1, g_ref[...]).astype(BF16)
    acc = x1
    for j in range(w1_ref.shape[1] // FF_CHUNK):
        cols = slice(j * FF_CHUNK, (j + 1) * FF_CHUNK)
        a = jnp.square(jnp.maximum(_dot(h2, w1_ref[:, cols]), 0.0)).astype(BF16)
        acc = acc + _dot(a, w2_ref[cols, :])
    if final:
        acc = _rms(acc, gf_ref[...])
    o_ref[...] = acc


def _ffn(mix, x2, wts, dense, layer, final):
    t, d = x2.shape
    wo, w1, w2 = wts
    dff = w1.shape[1]
    tm = FFN_TILE
    est = (2 * tm * d * 2 + 4 * tm * d * 4 + (d * d + 2 * d * dff) * 2
           + tm * FF_CHUNK * 6 + 3 * tm * d * 4)
    resident = lambda a: pl.BlockSpec(a.shape, lambda i: (0, 0), **_ONCE)
    return pl.pallas_call(
        functools.partial(_ffn_kernel, final=final),
        grid=(t // tm,),
        in_specs=[pl.BlockSpec((tm, d), lambda i: (i, 0)),
                  pl.BlockSpec((tm, d), lambda i: (i, 0)),
                  resident(wo),
                  _layer_slab(dense["g_ff"], layer),
                  resident(w1),
                  resident(w2),
                  pl.BlockSpec((1, d), lambda i: (0, 0))],
        out_specs=pl.BlockSpec((tm, d), lambda i: (i, 0)),
        out_shape=jax.ShapeDtypeStruct((t, d), F32),
        compiler_params=pltpu.CompilerParams(
            dimension_semantics=("arbitrary",), vmem_limit_bytes=_vmem_limit(est)),
        name="ffn_final" if final else "ffn",
    )(mix, x2, wo, dense["g_ff"], w1, w2, dense["g_final"])


def _tables(lay, seq):
    rd, half = lay.rd, lay.rd // 2
    l = CHUNK
    pos = jnp.arange(seq, dtype=F32)
    freqs = ROPE_THETA ** (-jnp.arange(half, dtype=F32) / half)
    ang = pos[:, None] * freqs[None, :]
    sign = jnp.asarray(np.where(np.arange(LANES) % rd < half, -1.0, 1.0), F32)
    cos = jnp.tile(jnp.cos(ang), (1, LANES // half))
    sin = jnp.tile(jnp.sin(ang), (1, LANES // half)) * sign[None, :]
    log_gamma = jnp.log(1.0 - 2.0 ** (-5.0 - jnp.arange(R_HEADS, dtype=F32)))
    p = jnp.arange(l, dtype=F32)
    causal = jnp.tril(jnp.ones((l, l), dtype=bool))
    rel = jnp.where(causal, p[:, None] - p[None, :], 0.0)
    decay = jnp.where(causal, jnp.exp(rel[None] * log_gamma[:, None, None]), 0.0)
    zeta = jnp.exp((l - 1.0 - p)[None, :] * log_gamma[:, None])
    xi = jnp.exp((p + 1.0)[None, :] * log_gamma[:, None])
    cdec = jnp.exp(l * log_gamma)
    npair = R_HEADS // 2
    decay_ab = decay.reshape(npair, 2, l, l).transpose(0, 2, 1, 3).reshape(npair, l, 2 * l)
    by_lane = lambda a: jnp.repeat(a.reshape(npair, 2, -1), rd, axis=1)
    xi_ab = by_lane(xi).transpose(0, 2, 1)
    zeta_k = by_lane(zeta).transpose(0, 2, 1)
    zeta_tile = jnp.tile(zeta_k.transpose(1, 0, 2).reshape(l, npair * LANES), (TOKEN_TILE // l, 1))
    cdt = jnp.broadcast_to(by_lane(cdec).transpose(0, 2, 1), (npair, LANES, LANES))
    head = np.arange(LANES) // rd
    bmaskt = jnp.asarray((head[:, None] == head[None, :]).astype(np.float32))
    return {"cos": cos, "sin": sin, "decay": decay_ab, "xi": xi_ab, "zeta_tile": zeta_tile, "cdt": cdt,
            "bmaskt": bmaskt}


def _block_diag_padded(w, lay):
    depth, nb = w.shape[0], w.shape[1]
    rows = w.reshape(depth, nb * M_BLOCK, M_BLOCK)
    tiled = jnp.tile(rows, (1, 1, lay.mpad // M_BLOCK))
    r = np.arange(nb * M_BLOCK)[:, None]
    c = np.arange(lay.mpad)[None, :]
    packed_c = (c // LANES) * lay.md + c % LANES
    keep = (c % LANES < lay.md) & (r // M_BLOCK == packed_c // M_BLOCK)
    return jnp.where(jnp.asarray(keep), tiled, 0.0).astype(BF16)


def _mixer_weights(lay, p):
    depth = p["m_wq"].shape[0]
    gbi = jnp.concatenate([p["m_i_bias"], jnp.zeros((depth, LANES - M_HEADS), F32)], axis=-1)
    gbf = jnp.concatenate(
        [p["m_f_bias"], p["m_f_bias"], jnp.zeros((depth, LANES - 2 * M_HEADS), F32)], axis=-1)
    causal = jnp.tril(jnp.ones((CHUNK, CHUNK), dtype=bool))
    wm = jnp.where(causal, p["g_ws"], 0.0)
    wcat = wm.transpose(0, 2, 1, 3).reshape(depth, CHUNK, G_GROUPS * CHUNK)
    gbmat = jnp.repeat(p["g_bs"].transpose(0, 2, 1), lay.gw // G_GROUPS, axis=2)
    row = lambda a: a[:, None, :]
    return {
        "convw": p["m_conv_w"], "convb": row(p["m_conv_b"]),
        "bdq": _block_diag_padded(p["m_wq"], lay),
        "bdk": _block_diag_padded(p["m_wk"], lay),
        "bdvt": _block_diag_padded(p["m_wv"], lay).transpose(0, 2, 1),
        "gbi": row(gbi), "gbf": row(gbf),
        "mnormt": jnp.broadcast_to(p["m_norm_g"][:, :, None], (depth, lay.mw, LANES)),
        "mskip": row(p["m_skip"]),
        "rnorm": row(p["r_norm_g"]),
        "gnorm": row(p["g_norm_g"]),
        "wcat": wcat.astype(BF16), "gbmat": gbmat,
    }


def kernel(x, norm_mix_g, w_in, m_conv_w, m_conv_b, m_wq, m_wk, m_wv, m_i_bias, m_f_bias, m_norm_g,
           m_skip, r_norm_g, g_norm_g, g_ws, g_bs, w_out, norm_ff_g, w_ff1, w_ff2, final_norm_g):
    batch, seq, d = x.shape
    depth = w_in.shape[0]
    lay = _Layout(d)
    assert seq % MIX_TILE == 0 and MIX_TILE % CHUNK == 0 and batch % MIX_ROWS == 0
    assert seq % TOKEN_TILE == 0 and TOKEN_TILE % CHUNK == 0 and w_ff1.shape[2] % FF_CHUNK == 0
    assert (batch * seq) % FFN_TILE == 0
    params = dict(m_conv_w=m_conv_w, m_conv_b=m_conv_b, m_wq=m_wq, m_wk=m_wk, m_wv=m_wv,
                  m_i_bias=m_i_bias, m_f_bias=m_f_bias, m_norm_g=m_norm_g, m_skip=m_skip,
                  r_norm_g=r_norm_g, g_norm_g=g_norm_g, g_ws=g_ws, g_bs=g_bs)
    tabs = _tables(lay, seq)
    lw = _mixer_weights(lay, params)
    dense = {
        "w_in_t": _relayout_in_proj_t(w_in, lay), "w_out": w_out, "w_ff1": w_ff1, "w_ff2": w_ff2,
        "g_mix": norm_mix_g[:, None, :], "g_ff": norm_ff_g[:, None, :], "g_final": final_norm_g[None, :],
    }
    x2 = x.reshape(batch * seq, d)
    for layer in range(depth):
        (act, gates, vt), ffn_w = _inproj(x2, tabs, lw, dense, layer, lay, seq)
        mix = _mixer(act.reshape(batch, seq, lay.na), gates.reshape(batch, seq, 2 * LANES),
                     vt.reshape(batch, seq // CHUNK, lay.mpad, CHUNK), tabs, lw, layer, lay, batch, seq)
        x2 = _ffn(mix.reshape(batch * seq, lay.dmix), x2, ffn_w, dense, layer, final=(layer == depth - 1))
    return x2.reshape(batch, seq, d)
```

```python
import functools

import numpy as np
import jax
import jax.numpy as jnp
from jax import lax
from jax.experimental import pallas as pl
from jax.experimental.pallas import tpu as pltpu

M_HEADS = 4
M_CONV = 4
M_BLOCK = 4
R_HEADS = 6
G_GROUPS = 4
CHUNK = 128
ROPE_THETA = 10000.0
EPS = 1e-6

LANES = 128
V7X_VMEM_BYTES = 64 * 1024 * 1024

TOKEN_TILE = 512
TOKEN_SPLIT = 2
FFN_TILE = 1024
MIX_TILE = 256
MIX_ROWS = 4
FF_CHUNK = 1024

F32 = jnp.float32
BF16 = jnp.bfloat16


def _rms(x, g):
    return x * lax.rsqrt(jnp.mean(jnp.square(x), axis=-1, keepdims=True) + EPS) * g


def _silu(x):
    return x * jax.nn.sigmoid(x)


def _dot(a, b):
    return jnp.dot(a, b, preferred_element_type=F32)


def _dot_nt(a, b):
    return lax.dot_general(a, b, (((1,), (1,)), ((), ())), preferred_element_type=F32)


def _dot_tn(a, b):
    return lax.dot_general(a, b, (((0,), (0,)), ((), ())), preferred_element_type=F32)


def _nbytes(shape, dtype):
    return int(np.prod(shape)) * jnp.dtype(dtype).itemsize


def _vmem_limit(nbytes):
    return int(min(nbytes + (12 << 20), V7X_VMEM_BYTES - (4 << 20)))


def _layer_slab(a, layer, **kw):
    return pl.BlockSpec((None,) + a.shape[1:], lambda *g: (layer,) + (0,) * (a.ndim - 1), **kw)


def _whole(a):
    return pl.BlockSpec(a.shape, lambda *g: (0,) * a.ndim)


_ONCE = dict(pipeline_mode=pl.Buffered(1))


class _Layout:
    def __init__(self, d_model):
        self.mw = 3 * d_model // 8
        self.rw = 3 * d_model // 8
        self.gw = d_model - self.mw - self.rw
        self.md = self.mw // M_HEADS
        self.rd = self.rw // R_HEADS
        assert self.md <= LANES and 2 * self.rd == LANES and self.gw // G_GROUPS * 2 == LANES
        assert self.mw % LANES == 0 and self.rw % LANES == 0 and self.gw % LANES == 0
        self.mpad = M_HEADS * LANES
        self.dmix = self.mw + self.rw + self.gw
        o = 0
        self.mx = o; o += self.mw
        self.mz = o; o += self.mw
        self.gu = o; o += self.gw
        self.gv = o; o += self.gw
        self.gi = o; o += LANES
        self.gf = o; o += LANES
        self.rq = o; o += self.rw
        self.rk = o; o += self.rw
        self.rg = o; o += self.rw
        self.rv = o; o += self.rw
        self.n = o
        o = 0
        self.a_q = o; o += self.mpad
        self.a_k = o; o += self.mpad
        self.a_rq = o; o += self.rw
        self.a_rka = o; o += self.rw
        self.a_rkb = o; o += self.rw
        self.a_kz = o; o += self.rw
        self.a_rva = o; o += self.rw
        self.a_rvb = o; o += self.rw
        self.a_gva = o; o += self.gw
        self.a_gvb = o; o += self.gw
        self.a_zs = o; o += self.mw
        self.a_sxz = o; o += self.mw
        self.a_rgs = o; o += self.rw
        self.a_gug = o; o += self.gw
        self.na = o


RELAYOUT_ROWS = 256


def _relayout_kernel(w_ref, o_ref, *, lay):
    d = w_ref.shape[1]
    mw, rw, gw = lay.mw, lay.rw, lay.gw
    o_gate = 2 * mw
    o_rq = o_gate + 2 * M_HEADS
    pieces = ((0, lay.mx, 2 * mw), (o_rq, lay.rq, rw), (o_rq + rw, lay.rk, rw), (o_rq + 2 * rw, lay.rv, rw),
              (o_rq + 3 * rw, lay.rg, rw), (o_rq + 4 * rw, lay.gu, gw), (o_rq + 4 * rw + gw, lay.gv, gw))
    for src, dst, n in pieces:
        for r in range(0, n, RELAYOUT_ROWS):
            m = min(RELAYOUT_ROWS, n - r)
            o_ref[dst + r:dst + r + m, :] = w_ref[src + r:src + r + m, :].astype(o_ref.dtype)
    gi = w_ref[o_gate:o_gate + M_HEADS, :]
    gf = w_ref[o_gate + M_HEADS:o_gate + 2 * M_HEADS, :]
    o_ref[lay.gi:lay.gi + LANES, :] = jnp.concatenate(
        [gi, jnp.zeros((LANES - M_HEADS, d), F32)], axis=0).astype(o_ref.dtype)
    o_ref[lay.gf:lay.gf + LANES, :] = jnp.concatenate(
        [gf, gf, jnp.zeros((LANES - 2 * M_HEADS, d), F32)], axis=0).astype(o_ref.dtype)


def _relayout_in_proj_t(w_in, lay):
    depth, d, n_in = w_in.shape
    wt = jnp.swapaxes(w_in, 1, 2)
    est = 2 * n_in * d * 4 + 2 * lay.n * d * 2 + 4 * RELAYOUT_ROWS * d * 4
    return pl.pallas_call(
        functools.partial(_relayout_kernel, lay=lay),
        grid=(depth,),
        in_specs=[pl.BlockSpec((None, n_in, d), lambda l: (l, 0, 0))],
        out_specs=pl.BlockSpec((None, lay.n, d), lambda l: (l, 0, 0)),
        out_shape=jax.ShapeDtypeStruct((depth, lay.n, d), BF16),
        compiler_params=pltpu.CompilerParams(
            dimension_semantics=("arbitrary",), vmem_limit_bytes=_vmem_limit(est)),
        name="relayout_w_in",
    )(wt)


_INPROJ_PARAMS = ("convw", "convb", "bdq", "bdk", "bdvt", "gbi", "gbf", "mskip", "gnorm")
_N_SIDE = 3


def _inproj_kernel(*refs, lay, nj):
    n_p = len(_INPROJ_PARAMS)
    x_ref, g_ref, w_ref, cos_ref, sin_ref, zeta_ref = refs[0:6]
    mp = dict(zip(_INPROJ_PARAMS, refs[6:6 + n_p]))
    wf = refs[6 + n_p:6 + n_p + _N_SIDE]
    act_ref, gates_ref, vt_ref = refs[6 + n_p + _N_SIDE:9 + n_p + _N_SIDE]
    wb = refs[9 + n_p + _N_SIDE:9 + n_p + 2 * _N_SIDE]
    xbuf = refs[9 + n_p + 2 * _N_SIDE]
    tm = x_ref.shape[0]
    for src, dst in zip(wf, wb):
        dst[...] = src[...].astype(dst.dtype)

    md, mw, rw, gw = lay.md, lay.mw, lay.rw, lay.gw
    first_of_seq = lax.rem(pl.program_id(0), nj) == 0

    @pl.when(first_of_seq)
    def _():
        xbuf[0:8, :] = jnp.zeros((8, mw), F32)

    @pl.when(jnp.logical_not(first_of_seq))
    def _():
        xbuf[0:8, :] = xbuf[tm:tm + 8, :]

    hm = tm // TOKEN_SPLIT
    assert hm % CHUNK == 0
    _round_robin([_inproj_rows(r0, hm, x_ref, g_ref, w_ref, cos_ref, sin_ref, zeta_ref, mp,
                               act_ref, gates_ref, vt_ref, xbuf, lay) for r0 in range(0, tm, hm)])


def _inproj_rows(r0, nr, x_ref, g_ref, w_ref, cos_ref, sin_ref, zeta_ref, mp, act_ref, gates_ref, vt_ref,
                 xbuf, lay):
    md, mw, rw, gw = lay.md, lay.mw, lay.rw, lay.gw
    rs = slice(r0, r0 + nr)

    proj = _dot_nt(_rms(x_ref[rs, :], g_ref[...]).astype(BF16), w_ref[...])
    yield

    def put(col, val):
        act_ref[rs, col:col + val.shape[1]] = val.astype(act_ref.dtype)

    mx = proj[:, lay.mx:lay.mx + mw]
    xbuf[8 + r0:8 + r0 + nr, :] = mx
    conv = mp["convb"][...] + mp["convw"][M_CONV - 1:M_CONV, :] * mx
    for t in range(M_CONV - 1):
        lo = 8 + r0 - (M_CONV - 1) + t
        conv = conv + mp["convw"][t:t + 1, :] * xbuf[lo:lo + nr, :]
    xc = _silu(conv)
    xcb = xc.astype(BF16)
    yield
    q = _dot(xcb, mp["bdq"][...])
    k = _dot(xcb, mp["bdk"][...])
    row_p = lax.broadcasted_iota(jnp.int32, (lay.mpad, 1), 0) % LANES
    vt = _dot_nt(mp["bdvt"][...], mx.astype(BF16)) + (row_p == md).astype(F32)
    yield
    put(lay.a_q, q)
    put(lay.a_k, k * (md ** -0.5))
    for c in range(nr // CHUNK):
        vt_ref[r0 // CHUNK + c] = vt[:, c * CHUNK:(c + 1) * CHUNK]
    zs = _silu(proj[:, lay.mz:lay.mz + mw])
    put(lay.a_zs, zs)
    put(lay.a_sxz, mp["mskip"][...] * xc * zs)
    yield

    li = lax.broadcasted_iota(jnp.int32, (nr, LANES), 1)
    lo_half = li < (LANES // 2)
    put(lay.a_gug, jax.nn.gelu(proj[:, lay.gu:lay.gu + gw]))
    yield
    gv = jax.nn.gelu(proj[:, lay.gv:lay.gv + gw])
    gvn = gv * lax.rsqrt(jnp.mean(gv * gv, axis=-1, keepdims=True) + EPS) * mp["gnorm"][...]
    for q_i in range(gw // LANES):
        off = q_i * LANES
        put(lay.a_gva + off, jnp.where(lo_half, gvn[:, off:off + LANES], 0.0))
        put(lay.a_gvb + off, jnp.where(lo_half, 0.0, gvn[:, off:off + LANES]))
    yield

    gates_ref[rs, 0:LANES] = proj[:, lay.gi:lay.gi + LANES] + mp["gbi"][...]
    gf = proj[:, lay.gf:lay.gf + LANES] + mp["gbf"][...]
    gates_ref[rs, LANES:2 * LANES] = jnp.where(li < 2 * M_HEADS, jax.nn.log_sigmoid(gf), 0.0)
    yield

    half = lay.rd // 2
    first_half = (li % lay.rd) < half
    cs = cos_ref[rs, :]
    sn = sin_ref[rs, :]

    def rotary(v):
        partner = jnp.where(first_half, pltpu.roll(v, LANES - half, axis=1), pltpu.roll(v, half, axis=1))
        return v * cs + partner * sn

    for p_i in range(rw // LANES):
        off = p_i * LANES
        rq = proj[:, lay.rq + off:lay.rq + off + LANES]
        rk = proj[:, lay.rk + off:lay.rk + off + LANES]
        put(lay.a_rq + off, rotary(rq))
        kr = rotary(rk) * (lay.rd ** -0.5)
        put(lay.a_rka + off, jnp.where(lo_half, kr, 0.0))
        put(lay.a_rkb + off, jnp.where(lo_half, 0.0, kr))
        put(lay.a_kz + off, kr * zeta_ref[rs, off:off + LANES])
        yield
    put(lay.a_rgs, _silu(proj[:, lay.rg:lay.rg + rw]))
    yield
    for p_i in range(rw // LANES):
        off = p_i * LANES
        rv = proj[:, lay.rv + off:lay.rv + off + LANES]
        put(lay.a_rva + off, jnp.where(lo_half, rv, 0.0))
        put(lay.a_rvb + off, jnp.where(lo_half, 0.0, rv))


def _inproj(x2, tabs, lw, dense, layer, lay, seq):
    t, d = x2.shape
    tm = TOKEN_TILE
    nj = seq // tm
    steps = t // tm
    mps = [lw[k] for k in _INPROJ_PARAMS]
    pos = lambda i: (lax.rem(i, nj), 0)
    side = [dense["w_out"], dense["w_ff1"], dense["w_ff2"]]
    assert len(side) == _N_SIDE
    side_in, side_out, side_shape = [], [], []
    for a in side:
        rows, cols = a.shape[1], a.shape[2]
        assert rows % steps == 0 and (rows // steps) % 16 == 0
        side_in.append(pl.BlockSpec((None, rows // steps, cols), lambda i: (layer, i, 0)))
        side_out.append(pl.BlockSpec((rows // steps, cols), lambda i: (i, 0)))
        side_shape.append(jax.ShapeDtypeStruct((rows, cols), BF16))
    est = (2 * tm * d * 4 + d * lay.n * 2 + 2 * tm * lay.na * 2 + 2 * tm * 2 * LANES * 4
           + 2 * tm * lay.mpad * 4 + 6 * tm * LANES * 4 + (tm + 8) * lay.mw * 4 + tm * lay.n * 4
           + 2 * sum(_nbytes(a.shape[1:], a.dtype) for a in mps)
           + 3 * sum(_nbytes(a.shape[1:], a.dtype) // steps for a in side))
    outs = pl.pallas_call(
        functools.partial(_inproj_kernel, lay=lay, nj=nj),
        grid=(steps,),
        in_specs=([pl.BlockSpec((tm, d), lambda i: (i, 0)),
                   _layer_slab(dense["g_mix"], layer),
                   _layer_slab(dense["w_in_t"], layer, **_ONCE),
                   pl.BlockSpec((tm, LANES), pos),
                   pl.BlockSpec((tm, LANES), pos),
                   _whole(tabs["zeta_tile"])]
                  + [_layer_slab(a, layer) for a in mps] + side_in),
        out_specs=[pl.BlockSpec((tm, lay.na), lambda i: (i, 0)),
                   pl.BlockSpec((tm, 2 * LANES), lambda i: (i, 0)),
                   pl.BlockSpec((tm // CHUNK, lay.mpad, CHUNK), lambda i: (i, 0, 0))] + side_out,
        out_shape=[jax.ShapeDtypeStruct((t, lay.na), BF16),
                   jax.ShapeDtypeStruct((t, 2 * LANES), F32),
                   jax.ShapeDtypeStruct((t // CHUNK, lay.mpad, CHUNK), F32)] + side_shape,
        scratch_shapes=[pltpu.VMEM((tm + 8, lay.mw), F32)],
        compiler_params=pltpu.CompilerParams(
            dimension_semantics=("arbitrary",), vmem_limit_bytes=_vmem_limit(est)),
        name="inproj",
    )(x2, dense["g_mix"], dense["w_in_t"], tabs["cos"], tabs["sin"], tabs["zeta_tile"], *mps, *side)
    return outs[0:3], outs[3:6]


_MIXER_PARAMS = ("mnormt", "rnorm", "wcat", "gbmat")
_TABLES = ("decay", "xi", "cdt", "bmaskt")


def _block_diag2(x, y):
    z = jnp.zeros_like(x)
    return jnp.concatenate([jnp.concatenate([x, z], axis=1), jnp.concatenate([z, y], axis=1)], axis=0)


def _mixer_chunk(c, act, gates, vt, mp, tb, out, c_st, r_st, m_st, lay):
    md, mw, rw, gw = lay.md, lay.mw, lay.rw, lay.gw
    li = lax.broadcasted_iota(jnp.int32, (CHUNK, LANES), 1)
    ri = lax.broadcasted_iota(jnp.int32, (CHUNK, LANES), 0)
    causal_t = ri <= li
    tril = (li <= ri).astype(BF16)
    lo_half = li < (LANES // 2)
    rows = pl.ds(pl.multiple_of(c * CHUNK, CHUNK), CHUNK)

    def a(col, width=LANES):
        return act[rows, col:col + width]

    gi = gates[rows, 0:LANES]
    logf = gates[rows, LANES:2 * LANES]
    a1 = logf.astype(BF16)
    r1 = logf - a1.astype(F32)
    a2 = r1.astype(BF16)
    a3 = (r1 - a2.astype(F32)).astype(BF16)
    b3 = _dot(tril, jnp.concatenate([a1, a2, a3], axis=1))
    yield
    bcum = (b3[:, 2 * LANES:3 * LANES] + b3[:, LANES:2 * LANES]) + b3[:, 0:LANES]
    wb = jnp.where(li < M_HEADS, gi - bcum, bcum)
    wb_t = wb.T
    yield

    causal_t2 = jnp.concatenate([causal_t, causal_t], axis=1)
    hn_t = []
    for hp in range(M_HEADS // 2):
        heads = (2 * hp, 2 * hp + 1)
        pair = lambda f: jnp.concatenate([f(h) for h in heads], axis=1)
        rep = lambda v: jnp.broadcast_to(v, (1, LANES))
        q2 = a(lay.a_q + 2 * hp * LANES, 2 * LANES)
        k2 = a(lay.a_k + 2 * hp * LANES, 2 * LANES)
        vt2 = pair(lambda h: vt[c, h * LANES:(h + 1) * LANES, :])
        w_row = pair(lambda h: wb_t[h:h + 1, :])
        b_row = pair(lambda h: wb_t[M_HEADS + h:M_HEADS + h + 1, :])
        b_end = pair(lambda h: rep(wb_t[M_HEADS + h:M_HEADS + h + 1, CHUNK - 1:CHUNK]))
        w_end = b_end + w_row
        a_c = pair(lambda h: rep(jnp.max(w_end[:, (h % 2) * LANES:(h % 2 + 1) * LANES], axis=-1, keepdims=True)))
        e_row = jnp.exp(w_end - a_c)
        m_prev = m_st[hp:hp + 1, :]
        m_new = jnp.maximum(b_end + m_prev, a_c)
        s_old = jnp.exp(b_end + m_prev - m_new)
        s_new = jnp.exp(a_c - m_new)
        m_st[hp:hp + 1, :] = m_new

        il = b_row + m_prev
        c_prev = c_st[hp]
        kq = _dot_nt(jnp.concatenate([k2, c_prev.astype(BF16)], axis=0),
                     _block_diag2(q2[:, 0:LANES], q2[:, LANES:2 * LANES]))
        yield
        dm = jnp.where(causal_t2, pair(lambda h: jnp.broadcast_to(wb[:, h:h + 1], (CHUNK, LANES))) + b_row,
                       -jnp.inf)
        m_t = jnp.maximum(il, jnp.max(dm, axis=0, keepdims=True))
        s_inter = jnp.exp(il - m_t)
        p_t = kq[0:CHUNK] * jnp.exp(dm - m_t)
        inter = s_inter * kq[CHUNK:2 * CHUNK]
        p_b = p_t.astype(BF16)
        tot = _dot(vt2.astype(BF16), _block_diag2(p_b[:, 0:LANES], p_b[:, LANES:2 * LANES])) + inter
        upd = _dot((vt2 * e_row).astype(BF16), _block_diag2(k2[:, 0:LANES], k2[:, LANES:2 * LANES]))
        yield
        den = jnp.sum(p_t, axis=0, keepdims=True) + inter[md:md + 1, :]
        rden = 1.0 / jnp.maximum(jnp.abs(den), jnp.exp(-m_t))
        hout = tot[0:md, :] * rden
        ms = jnp.sum(hout * hout, axis=0, keepdims=True) * (1.0 / md)
        hn = hout * lax.rsqrt(ms + EPS)
        for i, h in enumerate(heads):
            hn_t.append(hn[:, i * LANES:(i + 1) * LANES] * mp["mnormt"][h * md:(h + 1) * md, :])
        c_st[hp] = s_old * c_prev + s_new * upd
        yield

    hm_t = jnp.concatenate(hn_t, axis=0)
    hm = jnp.concatenate([hm_t[t * LANES:(t + 1) * LANES, :].T for t in range(mw // LANES)], axis=1)
    out[rows, 0:mw] = (hm * a(lay.a_zs, mw).astype(F32) + a(lay.a_sxz, mw).astype(F32)).astype(out.dtype)
    yield

    for p_i in range(rw // LANES):
        off = p_i * LANES
        qb = a(lay.a_rq + off)
        rva, rvb = a(lay.a_rva + off), a(lay.a_rvb + off)
        r_prev = r_st[p_i]
        sqr = _dot_nt(qb, jnp.concatenate([a(lay.a_rka + off), a(lay.a_rkb + off), r_prev.astype(BF16)],
                                          axis=0))
        kv = _dot_tn(rva + rvb, a(lay.a_kz + off))
        yield
        p_ab = (sqr[:, 0:2 * LANES] * tb["decay"][p_i]).astype(BF16)
        hr = _dot(p_ab, jnp.concatenate([rva, rvb], axis=0)) + sqr[:, 2 * LANES:3 * LANES] * tb["xi"][p_i]
        r_st[p_i] = tb["cdt"][p_i] * r_prev + tb["bmaskt"][...] * kv
        yield
        sq = hr * hr
        ms_a = jnp.sum(jnp.where(lo_half, sq, 0.0), axis=-1, keepdims=True)
        ms_b = jnp.sum(jnp.where(lo_half, 0.0, sq), axis=-1, keepdims=True)
        ms = jnp.where(lo_half, ms_a, ms_b) * (1.0 / lay.rd)
        hn = hr * lax.rsqrt(ms + EPS) * mp["rnorm"][:, off:off + LANES]
        out[rows, mw + off:mw + off + LANES] = (hn * a(lay.a_rgs + off).astype(F32)).astype(out.dtype)
        yield

    zero = jnp.zeros((2 * CHUNK, LANES), BF16)
    vst = [jnp.concatenate([a(lay.a_gva + q_i * LANES), a(lay.a_gvb + q_i * LANES)], axis=0)
           for q_i in range(gw // LANES)]
    assert len(vst) == 2
    mixed = _dot(mp["wcat"][...], jnp.concatenate([jnp.concatenate([vst[0], zero], axis=1),
                                                   jnp.concatenate([zero, vst[1]], axis=1)], axis=0))
    yield
    out[rows, mw + rw:mw + rw + gw] = (a(lay.a_gug, gw).astype(F32) * (mixed + mp["gbmat"][...])).astype(out.dtype)


def _round_robin(gens):
    live = list(gens)
    while live:
        live = [g for g in live if next(g, live) is not live]


def _mixer_kernel(*refs, lay):
    n_mp, n_tb = len(_MIXER_PARAMS), len(_TABLES)
    act_ref, gates_ref, vt_ref = refs[0:3]
    mp = dict(zip(_MIXER_PARAMS, refs[3:3 + n_mp]))
    tb = dict(zip(_TABLES, refs[3 + n_mp:3 + n_mp + n_tb]))
    o_ref = refs[3 + n_mp + n_tb]
    c_st, r_st, m_st = refs[4 + n_mp + n_tb:]
    nrow, ts = act_ref.shape[0], act_ref.shape[1]

    @pl.when(pl.program_id(1) == 0)
    def _():
        c_st[...] = jnp.zeros_like(c_st)
        r_st[...] = jnp.zeros_like(r_st)
        m_st[...] = jnp.zeros_like(m_st)

    def step(c, carry):
        _round_robin([_mixer_chunk(c, act_ref.at[r], gates_ref.at[r], vt_ref.at[r], mp, tb, o_ref.at[r],
                                   c_st.at[r], r_st.at[r], m_st.at[r], lay) for r in range(nrow)])
        return carry

    lax.fori_loop(0, ts // CHUNK, step, 0)


def _mixer(act, gates, vt, tabs, lw, layer, lay, batch, seq):
    ts, nrow = MIX_TILE, MIX_ROWS
    mps = [lw[k] for k in _MIXER_PARAMS]
    tbs = [tabs[k] for k in _TABLES]
    scratch = [pltpu.VMEM((nrow, M_HEADS // 2, LANES, 2 * LANES), F32),
               pltpu.VMEM((nrow, lay.rw // LANES, LANES, LANES), F32),
               pltpu.VMEM((nrow, 8, 2 * LANES), F32)]
    est = (2 * nrow * ts * (lay.na * 2 + 2 * LANES * 4 + lay.mpad * 4 + lay.dmix * 2)
           + 2 * sum(_nbytes(a.shape[1:], a.dtype) for a in mps) + 2 * sum(_nbytes(a.shape, a.dtype) for a in tbs)
           + sum(_nbytes(s.shape, s.dtype) for s in scratch))
    return pl.pallas_call(
        functools.partial(_mixer_kernel, lay=lay),
        grid=(batch // nrow, seq // ts),
        in_specs=([pl.BlockSpec((nrow, ts, lay.na), lambda b, j: (b, j, 0)),
                   pl.BlockSpec((nrow, ts, 2 * LANES), lambda b, j: (b, j, 0)),
                   pl.BlockSpec((nrow, ts // CHUNK, lay.mpad, CHUNK), lambda b, j: (b, j, 0, 0))]
                  + [_layer_slab(a, layer) for a in mps] + [_whole(a) for a in tbs]),
        out_specs=pl.BlockSpec((nrow, ts, lay.dmix), lambda b, j: (b, j, 0)),
        out_shape=jax.ShapeDtypeStruct((batch, seq, lay.dmix), BF16),
        scratch_shapes=scratch,
        compiler_params=pltpu.CompilerParams(
            dimension_semantics=("arbitrary", "arbitrary"), vmem_limit_bytes=_vmem_limit(est)),
        name="mixer",
    )(act, gates, vt, *mps, *tbs)


def _ffn_kernel(mix_ref, x_ref, wo_ref, g_ref, w1_ref, w2_ref, gf_ref, o_ref, *, final):
    x1 = x_ref[...] + _dot(mix_ref[...], wo_ref[...])
    h2 = _rms(x1, g_ref[...]).astype(BF16)
    acc = x1
    for j in range(w1_ref.shape[1] // FF_CHUNK):
        cols = slice(j * FF_CHUNK, (j + 1) * FF_CHUNK)
        a = jnp.square(jnp.maximum(_dot(h2, w1_ref[:, cols]), 0.0)).astype(BF16)
        acc = acc + _dot(a, w2_ref[cols, :])
    if final:
        acc = _rms(acc, gf_ref[...])
    o_ref[...] = acc


def _ffn(mix, x2, wts, dense, layer, final):
    t, d = x2.shape
    wo, w1, w2 = wts
    dff = w1.shape[1]
    tm = FFN_TILE
    est = (2 * tm * d * 2 + 4 * tm * d * 4 + (d * d + 2 * d * dff) * 2
           + tm * FF_CHUNK * 6 + 3 * tm * d * 4)
    resident = lambda a: pl.BlockSpec(a.shape, lambda i: (0, 0), **_ONCE)
    return pl.pallas_call(
        functools.partial(_ffn_kernel, final=final),
        grid=(t // tm,),
        in_specs=[pl.BlockSpec((tm, d), lambda i: (i, 0)),
                  pl.BlockSpec((tm, d), lambda i: (i, 0)),
                  resident(wo),
                  _layer_slab(dense["g_ff"], layer),
                  resident(w1),
                  resident(w2),
                  pl.BlockSpec((1, d), lambda i: (0, 0))],
        out_specs=pl.BlockSpec((tm, d), lambda i: (i, 0)),
        out_shape=jax.ShapeDtypeStruct((t, d), F32),
        compiler_params=pltpu.CompilerParams(
            dimension_semantics=("arbitrary",), vmem_limit_bytes=_vmem_limit(est)),
        name="ffn_final" if final else "ffn",
    )(mix, x2, wo, dense["g_ff"], w1, w2, dense["g_final"])


def _tables(lay, seq):
    rd, half = lay.rd, lay.rd // 2
    l = CHUNK
    pos = np.arange(seq, dtype=np.float64)
    freqs = ROPE_THETA ** (-np.arange(half, dtype=np.float64) / half)
    ang = pos[:, None] * freqs[None, :]
    sign = np.where(np.arange(LANES) % rd < half, -1.0, 1.0)
    cos = np.tile(np.cos(ang), (1, LANES // half))
    sin = np.tile(np.sin(ang), (1, LANES // half)) * sign[None, :]
    log_gamma = np.log(1.0 - 2.0 ** (-5.0 - np.arange(R_HEADS, dtype=np.float64)))
    p = np.arange(l, dtype=np.float64)
    causal = np.tril(np.ones((l, l), dtype=bool))
    rel = np.where(causal, p[:, None] - p[None, :], 0.0)
    decay = np.where(causal, np.exp(rel[None] * log_gamma[:, None, None]), 0.0)
    zeta = np.exp((l - 1.0 - p)[None, :] * log_gamma[:, None])
    xi = np.exp((p + 1.0)[None, :] * log_gamma[:, None])
    cdec = np.exp(l * log_gamma)
    npair = R_HEADS // 2
    decay_ab = decay.reshape(npair, 2, l, l).transpose(0, 2, 1, 3).reshape(npair, l, 2 * l)
    by_lane = lambda a: np.repeat(a.reshape(npair, 2, -1), rd, axis=1)
    xi_ab = by_lane(xi).transpose(0, 2, 1)
    zeta_k = by_lane(zeta).transpose(0, 2, 1)
    zeta_tile = np.tile(zeta_k.transpose(1, 0, 2).reshape(l, npair * LANES), (TOKEN_TILE // l, 1))
    cdt = np.broadcast_to(by_lane(cdec).transpose(0, 2, 1), (npair, LANES, LANES))
    head = np.arange(LANES) // rd
    bmaskt = head[:, None] == head[None, :]
    tabs = {"cos": cos, "sin": sin, "decay": decay_ab, "xi": xi_ab, "zeta_tile": zeta_tile, "cdt": cdt,
            "bmaskt": bmaskt}
    return {k: jnp.asarray(np.ascontiguousarray(v), F32) for k, v in tabs.items()}


def _block_diag_padded(w, lay):
    depth, nb = w.shape[0], w.shape[1]
    rows = w.reshape(depth, nb * M_BLOCK, M_BLOCK)
    tiled = jnp.tile(rows, (1, 1, lay.mpad // M_BLOCK))
    r = np.arange(nb * M_BLOCK)[:, None]
    c = np.arange(lay.mpad)[None, :]
    packed_c = (c // LANES) * lay.md + c % LANES
    keep = (c % LANES < lay.md) & (r // M_BLOCK == packed_c // M_BLOCK)
    return jnp.where(jnp.asarray(keep), tiled, 0.0).astype(BF16)


def _mixer_weights(lay, p):
    depth = p["m_wq"].shape[0]
    gbi = jnp.concatenate([p["m_i_bias"], jnp.zeros((depth, LANES - M_HEADS), F32)], axis=-1)
    gbf = jnp.concatenate(
        [p["m_f_bias"], p["m_f_bias"], jnp.zeros((depth, LANES - 2 * M_HEADS), F32)], axis=-1)
    causal = jnp.tril(jnp.ones((CHUNK, CHUNK), dtype=bool))
    wm = jnp.where(causal, p["g_ws"], 0.0)
    wcat = wm.transpose(0, 2, 1, 3).reshape(depth, CHUNK, G_GROUPS * CHUNK)
    gbmat = jnp.repeat(p["g_bs"].transpose(0, 2, 1), lay.gw // G_GROUPS, axis=2)
    row = lambda a: a[:, None, :]
    return {
        "convw": p["m_conv_w"], "convb": row(p["m_conv_b"]),
        "bdq": _block_diag_padded(p["m_wq"], lay),
        "bdk": _block_diag_padded(p["m_wk"], lay),
        "bdvt": _block_diag_padded(p["m_wv"], lay).transpose(0, 2, 1),
        "gbi": row(gbi), "gbf": row(gbf),
        "mnormt": jnp.broadcast_to(p["m_norm_g"][:, :, None], (depth, lay.mw, LANES)),
        "mskip": row(p["m_skip"]),
        "rnorm": row(p["r_norm_g"]),
        "gnorm": row(p["g_norm_g"]),
        "wcat": wcat.astype(BF16), "gbmat": gbmat,
    }


def kernel(x, norm_mix_g, w_in, m_conv_w, m_conv_b, m_wq, m_wk, m_wv, m_i_bias, m_f_bias, m_norm_g,
           m_skip, r_norm_g, g_norm_g, g_ws, g_bs, w_out, norm_ff_g, w_ff1, w_ff2, final_norm_g):
    batch, seq, d = x.shape
    depth = w_in.shape[0]
    lay = _Layout(d)
    assert seq % MIX_TILE == 0 and MIX_TILE % CHUNK == 0 and batch % MIX_ROWS == 0
    assert seq % TOKEN_TILE == 0 and TOKEN_TILE % CHUNK == 0 and w_ff1.shape[2] % FF_CHUNK == 0
    assert (batch * seq) % FFN_TILE == 0
    params = dict(m_conv_w=m_conv_w, m_conv_b=m_conv_b, m_wq=m_wq, m_wk=m_wk, m_wv=m_wv,
                  m_i_bias=m_i_bias, m_f_bias=m_f_bias, m_norm_g=m_norm_g, m_skip=m_skip,
                  r_norm_g=r_norm_g, g_norm_g=g_norm_g, g_ws=g_ws, g_bs=g_bs)
    tabs = _tables(lay, seq)
    lw = _mixer_weights(lay, params)
    dense = {
        "w_in_t": _relayout_in_proj_t(w_in, lay), "w_out": w_out, "w_ff1": w_ff1, "w_ff2": w_ff2,
        "g_mix": norm_mix_g[:, None, :], "g_ff": norm_ff_g[:, None, :], "g_final": final_norm_g[None, :],
    }
    x2 = x.reshape(batch * seq, d)
    for layer in range(depth):
        (act, gates, vt), ffn_w = _inproj(x2, tabs, lw, dense, layer, lay, seq)
        mix = _mixer(act.reshape(batch, seq, lay.na), gates.reshape(batch, seq, 2 * LANES),
                     vt.reshape(batch, seq // CHUNK, lay.mpad, CHUNK), tabs, lw, layer, lay, batch, seq)
        x2 = _ffn(mix.reshape(batch * seq, lay.dmix), x2, ffn_w, dense, layer, final=(layer == depth - 1))
    return x2.reshape(batch, seq, d)
```

```python
import functools

import numpy as np
import jax
import jax.numpy as jnp
from jax import lax
from jax.experimental import pallas as pl
from jax.experimental.pallas import tpu as pltpu

M_HEADS = 4
M_CONV = 4
M_BLOCK = 4
R_HEADS = 6
G_GROUPS = 4
CHUNK = 128
ROPE_THETA = 10000.0
EPS = 1e-6

LANES = 128
V7X_VMEM_BYTES = 64 * 1024 * 1024

TOKEN_TILE = 512
TOKEN_SPLIT = 2
FFN_TILE = 1024
MIX_TILE = 256
MIX_ROWS = 4
FF_CHUNK = 1024

F32 = jnp.float32
BF16 = jnp.bfloat16


def _rms(x, g):
    return x * lax.rsqrt(jnp.mean(jnp.square(x), axis=-1, keepdims=True) + EPS) * g


def _silu(x):
    return x * jax.nn.sigmoid(x)


def _dot(a, b):
    return jnp.dot(a, b, preferred_element_type=F32)


def _dot_nt(a, b):
    return lax.dot_general(a, b, (((1,), (1,)), ((), ())), preferred_element_type=F32)


def _dot_tn(a, b):
    return lax.dot_general(a, b, (((0,), (0,)), ((), ())), preferred_element_type=F32)


def _nbytes(shape, dtype):
    return int(np.prod(shape)) * jnp.dtype(dtype).itemsize


def _vmem_limit(nbytes):
    return int(min(nbytes + (12 << 20), V7X_VMEM_BYTES - (4 << 20)))


def _layer_slab(a, layer, **kw):
    return pl.BlockSpec((None,) + a.shape[1:], lambda *g: (layer,) + (0,) * (a.ndim - 1), **kw)


def _whole(a):
    return pl.BlockSpec(a.shape, lambda *g: (0,) * a.ndim)


_ONCE = dict(pipeline_mode=pl.Buffered(1))


class _Layout:
    def __init__(self, d_model):
        self.mw = 3 * d_model // 8
        self.rw = 3 * d_model // 8
        self.gw = d_model - self.mw - self.rw
        self.md = self.mw // M_HEADS
        self.rd = self.rw // R_HEADS
        assert self.md <= LANES and 2 * self.rd == LANES and self.gw // G_GROUPS * 2 == LANES
        assert self.mw % LANES == 0 and self.rw % LANES == 0 and self.gw % LANES == 0
        self.mpad = M_HEADS * LANES
        self.dmix = self.mw + self.rw + self.gw
        o = 0
        self.mx = o; o += self.mw
        self.mz = o; o += self.mw
        self.gu = o; o += self.gw
        self.gv = o; o += self.gw
        self.gi = o; o += LANES
        self.gf = o; o += LANES
        self.rq = o; o += self.rw
        self.rk = o; o += self.rw
        self.rg = o; o += self.rw
        self.rv = o; o += self.rw
        self.n = o
        o = 0
        self.a_q = o; o += self.mpad
        self.a_k = o; o += self.mpad
        self.a_rq = o; o += self.rw
        self.a_rk = o; o += self.rw
        self.a_kz = o; o += self.rw
        self.a_rv = o; o += self.rw
        self.a_gv = o; o += self.gw
        self.a_zs = o; o += self.mw
        self.a_sxz = o; o += self.mw
        self.a_rgs = o; o += self.rw
        self.a_gug = o; o += self.gw
        self.na = o


RELAYOUT_ROWS = 256


def _relayout_kernel(w_ref, o_ref, *, lay):
    d = w_ref.shape[1]
    mw, rw, gw = lay.mw, lay.rw, lay.gw
    o_gate = 2 * mw
    o_rq = o_gate + 2 * M_HEADS
    pieces = ((0, lay.mx, 2 * mw), (o_rq, lay.rq, rw), (o_rq + rw, lay.rk, rw), (o_rq + 2 * rw, lay.rv, rw),
              (o_rq + 3 * rw, lay.rg, rw), (o_rq + 4 * rw, lay.gu, gw), (o_rq + 4 * rw + gw, lay.gv, gw))
    for src, dst, n in pieces:
        for r in range(0, n, RELAYOUT_ROWS):
            m = min(RELAYOUT_ROWS, n - r)
            o_ref[dst + r:dst + r + m, :] = w_ref[src + r:src + r + m, :].astype(o_ref.dtype)
    gi = w_ref[o_gate:o_gate + M_HEADS, :]
    gf = w_ref[o_gate + M_HEADS:o_gate + 2 * M_HEADS, :]
    o_ref[lay.gi:lay.gi + LANES, :] = jnp.concatenate(
        [gi, jnp.zeros((LANES - M_HEADS, d), F32)], axis=0).astype(o_ref.dtype)
    o_ref[lay.gf:lay.gf + LANES, :] = jnp.concatenate(
        [gf, gf, jnp.zeros((LANES - 2 * M_HEADS, d), F32)], axis=0).astype(o_ref.dtype)


def _relayout_in_proj_t(w_in, lay):
    depth, d, n_in = w_in.shape
    wt = jnp.swapaxes(w_in, 1, 2)
    est = 2 * n_in * d * 4 + 2 * lay.n * d * 2 + 4 * RELAYOUT_ROWS * d * 4
    return pl.pallas_call(
        functools.partial(_relayout_kernel, lay=lay),
        grid=(depth,),
        in_specs=[pl.BlockSpec((None, n_in, d), lambda l: (l, 0, 0))],
        out_specs=pl.BlockSpec((None, lay.n, d), lambda l: (l, 0, 0)),
        out_shape=jax.ShapeDtypeStruct((depth, lay.n, d), BF16),
        compiler_params=pltpu.CompilerParams(
            dimension_semantics=("arbitrary",), vmem_limit_bytes=_vmem_limit(est)),
        name="relayout_w_in",
    )(wt)


_INPROJ_PARAMS = ("convw", "convb", "bdq", "bdk", "bdvt", "gbi", "gbf", "mskip", "gnorm")
_N_SIDE = 3


def _inproj_kernel(*refs, lay, nj):
    n_p = len(_INPROJ_PARAMS)
    x_ref, g_ref, w_ref, cos_ref, sin_ref, zeta_ref = refs[0:6]
    mp = dict(zip(_INPROJ_PARAMS, refs[6:6 + n_p]))
    wf = refs[6 + n_p:6 + n_p + _N_SIDE]
    act_ref, gates_ref, vt_ref = refs[6 + n_p + _N_SIDE:9 + n_p + _N_SIDE]
    wb = refs[9 + n_p + _N_SIDE:9 + n_p + 2 * _N_SIDE]
    xbuf = refs[9 + n_p + 2 * _N_SIDE]
    tm = x_ref.shape[0]
    for src, dst in zip(wf, wb):
        dst[...] = src[...].astype(dst.dtype)

    md, mw, rw, gw = lay.md, lay.mw, lay.rw, lay.gw
    first_of_seq = lax.rem(pl.program_id(0), nj) == 0

    @pl.when(first_of_seq)
    def _():
        xbuf[0:8, :] = jnp.zeros((8, mw), F32)

    @pl.when(jnp.logical_not(first_of_seq))
    def _():
        xbuf[0:8, :] = xbuf[tm:tm + 8, :]

    hm = tm // TOKEN_SPLIT
    assert hm % CHUNK == 0
    _round_robin([_inproj_rows(r0, hm, x_ref, g_ref, w_ref, cos_ref, sin_ref, zeta_ref, mp,
                               act_ref, gates_ref, vt_ref, xbuf, lay) for r0 in range(0, tm, hm)])


def _inproj_rows(r0, nr, x_ref, g_ref, w_ref, cos_ref, sin_ref, zeta_ref, mp, act_ref, gates_ref, vt_ref,
                 xbuf, lay):
    md, mw, rw, gw = lay.md, lay.mw, lay.rw, lay.gw
    rs = slice(r0, r0 + nr)

    proj = _dot_nt(_rms(x_ref[rs, :], g_ref[...]).astype(BF16), w_ref[...])
    yield

    def put(col, val):
        act_ref[rs, col:col + val.shape[1]] = val.astype(act_ref.dtype)

    mx = proj[:, lay.mx:lay.mx + mw]
    xbuf[8 + r0:8 + r0 + nr, :] = mx
    conv = mp["convb"][...] + mp["convw"][M_CONV - 1:M_CONV, :] * mx
    for t in range(M_CONV - 1):
        lo = 8 + r0 - (M_CONV - 1) + t
        conv = conv + mp["convw"][t:t + 1, :] * xbuf[lo:lo + nr, :]
    xc = _silu(conv)
    xcb = xc.astype(BF16)
    yield
    q = _dot(xcb, mp["bdq"][...])
    k = _dot(xcb, mp["bdk"][...])
    row_p = lax.broadcasted_iota(jnp.int32, (lay.mpad, 1), 0) % LANES
    vt = _dot_nt(mp["bdvt"][...], mx.astype(BF16)) + (row_p == md).astype(F32)
    yield
    put(lay.a_q, q)
    put(lay.a_k, k * (md ** -0.5))
    for c in range(nr // CHUNK):
        vt_ref[r0 // CHUNK + c] = vt[:, c * CHUNK:(c + 1) * CHUNK]
    zs = _silu(proj[:, lay.mz:lay.mz + mw])
    put(lay.a_zs, zs)
    put(lay.a_sxz, mp["mskip"][...] * xc * zs)
    yield

    li = lax.broadcasted_iota(jnp.int32, (nr, LANES), 1)
    put(lay.a_gug, jax.nn.gelu(proj[:, lay.gu:lay.gu + gw]))
    yield
    gv = jax.nn.gelu(proj[:, lay.gv:lay.gv + gw])
    put(lay.a_gv, gv * lax.rsqrt(jnp.mean(gv * gv, axis=-1, keepdims=True) + EPS) * mp["gnorm"][...])
    yield

    gates_ref[rs, 0:LANES] = proj[:, lay.gi:lay.gi + LANES] + mp["gbi"][...]
    gf = proj[:, lay.gf:lay.gf + LANES] + mp["gbf"][...]
    gates_ref[rs, LANES:2 * LANES] = jnp.where(li < 2 * M_HEADS, jax.nn.log_sigmoid(gf), 0.0)
    yield

    half = lay.rd // 2
    first_half = (li % lay.rd) < half
    cs = cos_ref[rs, :]
    sn = sin_ref[rs, :]

    def rotary(v):
        partner = jnp.where(first_half, pltpu.roll(v, LANES - half, axis=1), pltpu.roll(v, half, axis=1))
        return v * cs + partner * sn

    for p_i in range(rw // LANES):
        off = p_i * LANES
        rq = proj[:, lay.rq + off:lay.rq + off + LANES]
        rk = proj[:, lay.rk + off:lay.rk + off + LANES]
        put(lay.a_rq + off, rotary(rq))
        kr = rotary(rk) * (lay.rd ** -0.5)
        put(lay.a_rk + off, kr)
        put(lay.a_kz + off, kr * zeta_ref[rs, off:off + LANES])
        yield
    put(lay.a_rgs, _silu(proj[:, lay.rg:lay.rg + rw]))
    yield
    put(lay.a_rv, proj[:, lay.rv:lay.rv + rw])


def _inproj(x2, tabs, lw, dense, layer, lay, seq):
    t, d = x2.shape
    tm = TOKEN_TILE
    nj = seq // tm
    steps = t // tm
    mps = [lw[k] for k in _INPROJ_PARAMS]
    pos = lambda i: (lax.rem(i, nj), 0)
    side = [dense["w_out"], dense["w_ff1"], dense["w_ff2"]]
    assert len(side) == _N_SIDE
    side_in, side_out, side_shape = [], [], []
    for a in side:
        rows, cols = a.shape[1], a.shape[2]
        assert rows % steps == 0 and (rows // steps) % 16 == 0
        side_in.append(pl.BlockSpec((None, rows // steps, cols), lambda i: (layer, i, 0)))
        side_out.append(pl.BlockSpec((rows // steps, cols), lambda i: (i, 0)))
        side_shape.append(jax.ShapeDtypeStruct((rows, cols), BF16))
    est = (2 * tm * d * 4 + d * lay.n * 2 + 2 * tm * lay.na * 2 + 2 * tm * 2 * LANES * 4
           + 2 * tm * lay.mpad * 4 + 6 * tm * LANES * 4 + (tm + 8) * lay.mw * 4 + tm * lay.n * 4
           + 2 * sum(_nbytes(a.shape[1:], a.dtype) for a in mps)
           + 3 * sum(_nbytes(a.shape[1:], a.dtype) // steps for a in side))
    outs = pl.pallas_call(
        functools.partial(_inproj_kernel, lay=lay, nj=nj),
        grid=(steps,),
        in_specs=([pl.BlockSpec((tm, d), lambda i: (i, 0)),
                   _layer_slab(dense["g_mix"], layer),
                   _layer_slab(dense["w_in_t"], layer, **_ONCE),
                   pl.BlockSpec((tm, LANES), pos),
                   pl.BlockSpec((tm, LANES), pos),
                   _whole(tabs["zeta_tile"])]
                  + [_layer_slab(a, layer) for a in mps] + side_in),
        out_specs=[pl.BlockSpec((tm, lay.na), lambda i: (i, 0)),
                   pl.BlockSpec((tm, 2 * LANES), lambda i: (i, 0)),
                   pl.BlockSpec((tm // CHUNK, lay.mpad, CHUNK), lambda i: (i, 0, 0))] + side_out,
        out_shape=[jax.ShapeDtypeStruct((t, lay.na), BF16),
                   jax.ShapeDtypeStruct((t, 2 * LANES), F32),
                   jax.ShapeDtypeStruct((t // CHUNK, lay.mpad, CHUNK), F32)] + side_shape,
        scratch_shapes=[pltpu.VMEM((tm + 8, lay.mw), F32)],
        compiler_params=pltpu.CompilerParams(
            dimension_semantics=("arbitrary",), vmem_limit_bytes=_vmem_limit(est)),
        name="inproj",
    )(x2, dense["g_mix"], dense["w_in_t"], tabs["cos"], tabs["sin"], tabs["zeta_tile"], *mps, *side)
    return outs[0:3], outs[3:6]


_MIXER_PARAMS = ("mnormt", "rnorm", "wcat", "gbmat")
_TABLES = ("decay", "xi", "cdt", "bmaskt")


def _block_diag2(x, y):
    z = jnp.zeros_like(x)
    return jnp.concatenate([jnp.concatenate([x, z], axis=1), jnp.concatenate([z, y], axis=1)], axis=0)


def _mixer_chunk(c, act, gates, vt, mp, tb, out, c_st, r_st, m_st, lay):
    md, mw, rw, gw = lay.md, lay.mw, lay.rw, lay.gw
    li = lax.broadcasted_iota(jnp.int32, (CHUNK, LANES), 1)
    ri = lax.broadcasted_iota(jnp.int32, (CHUNK, LANES), 0)
    causal_t = ri <= li
    tril = (li <= ri).astype(BF16)
    lo_half = li < (LANES // 2)
    rows = pl.ds(pl.multiple_of(c * CHUNK, CHUNK), CHUNK)

    def a(col, width=LANES):
        return act[rows, col:col + width]

    gi = gates[rows, 0:LANES]
    logf = gates[rows, LANES:2 * LANES]
    a1 = logf.astype(BF16)
    r1 = logf - a1.astype(F32)
    a2 = r1.astype(BF16)
    a3 = (r1 - a2.astype(F32)).astype(BF16)
    b3 = _dot(tril, jnp.concatenate([a1, a2, a3], axis=1))
    yield
    bcum = (b3[:, 2 * LANES:3 * LANES] + b3[:, LANES:2 * LANES]) + b3[:, 0:LANES]
    wb = jnp.where(li < M_HEADS, gi - bcum, bcum)
    wb_t = wb.T
    yield

    causal_t2 = jnp.concatenate([causal_t, causal_t], axis=1)
    hn_t = []
    for hp in range(M_HEADS // 2):
        heads = (2 * hp, 2 * hp + 1)
        pair = lambda f: jnp.concatenate([f(h) for h in heads], axis=1)
        rep = lambda v: jnp.broadcast_to(v, (1, LANES))
        q2 = a(lay.a_q + 2 * hp * LANES, 2 * LANES)
        k2 = a(lay.a_k + 2 * hp * LANES, 2 * LANES)
        vt2 = pair(lambda h: vt[c, h * LANES:(h + 1) * LANES, :])
        w_row = pair(lambda h: wb_t[h:h + 1, :])
        b_row = pair(lambda h: wb_t[M_HEADS + h:M_HEADS + h + 1, :])
        b_end = pair(lambda h: rep(wb_t[M_HEADS + h:M_HEADS + h + 1, CHUNK - 1:CHUNK]))
        w_end = b_end + w_row
        a_c = pair(lambda h: rep(jnp.max(w_end[:, (h % 2) * LANES:(h % 2 + 1) * LANES], axis=-1, keepdims=True)))
        e_row = jnp.exp(w_end - a_c)
        m_prev = m_st[hp:hp + 1, :]
        m_new = jnp.maximum(b_end + m_prev, a_c)
        s_old = jnp.exp(b_end + m_prev - m_new)
        s_new = jnp.exp(a_c - m_new)
        m_st[hp:hp + 1, :] = m_new

        il = b_row + m_prev
        c_prev = c_st[hp]
        kq = _dot_nt(jnp.concatenate([k2, c_prev.astype(BF16)], axis=0),
                     _block_diag2(q2[:, 0:LANES], q2[:, LANES:2 * LANES]))
        yield
        dm = jnp.where(causal_t2, pair(lambda h: jnp.broadcast_to(wb[:, h:h + 1], (CHUNK, LANES))) + b_row,
                       -jnp.inf)
        m_t = jnp.maximum(il, jnp.max(dm, axis=0, keepdims=True))
        s_inter = jnp.exp(il - m_t)
        p_t = kq[0:CHUNK] * jnp.exp(dm - m_t)
        inter = s_inter * kq[CHUNK:2 * CHUNK]
        p_b = p_t.astype(BF16)
        tot = _dot(vt2.astype(BF16), _block_diag2(p_b[:, 0:LANES], p_b[:, LANES:2 * LANES])) + inter
        upd = _dot((vt2 * e_row).astype(BF16), _block_diag2(k2[:, 0:LANES], k2[:, LANES:2 * LANES]))
        yield
        den = jnp.sum(p_t, axis=0, keepdims=True) + inter[md:md + 1, :]
        rden = 1.0 / jnp.maximum(jnp.abs(den), jnp.exp(-m_t))
        hout = tot[0:md, :] * rden
        ms = jnp.sum(hout * hout, axis=0, keepdims=True) * (1.0 / md)
        hn = hout * lax.rsqrt(ms + EPS)
        for i, h in enumerate(heads):
            hn_t.append(hn[:, i * LANES:(i + 1) * LANES] * mp["mnormt"][h * md:(h + 1) * md, :])
        c_st[hp] = s_old * c_prev + s_new * upd
        yield

    hm_t = jnp.concatenate(hn_t, axis=0)
    hm = jnp.concatenate([hm_t[t * LANES:(t + 1) * LANES, :].T for t in range(mw // LANES)], axis=1)
    out[rows, 0:mw] = (hm * a(lay.a_zs, mw).astype(F32) + a(lay.a_sxz, mw).astype(F32)).astype(out.dtype)
    yield

    lane_row = lax.broadcasted_iota(jnp.int32, (1, LANES), 1)
    m_a = (lane_row < LANES // 2).astype(BF16)
    m_b = (lane_row >= LANES // 2).astype(BF16)
    for p_i in range(rw // LANES):
        off = p_i * LANES
        qb = a(lay.a_rq + off)
        rk, rv = a(lay.a_rk + off), a(lay.a_rv + off)
        r_prev = r_st[p_i]
        sqr = _dot_nt(qb, jnp.concatenate([rk * m_a, rk * m_b, r_prev.astype(BF16)],
                                          axis=0))
        kv = _dot_tn(rv, a(lay.a_kz + off))
        yield
        p_ab = (sqr[:, 0:2 * LANES] * tb["decay"][p_i]).astype(BF16)
        hr = (_dot(p_ab, jnp.concatenate([rv * m_a, rv * m_b], axis=0))
              + sqr[:, 2 * LANES:3 * LANES] * tb["xi"][p_i])
        r_st[p_i] = tb["cdt"][p_i] * r_prev + tb["bmaskt"][...] * kv
        yield
        sq = hr * hr
        ms_a = jnp.sum(jnp.where(lo_half, sq, 0.0), axis=-1, keepdims=True)
        ms_b = jnp.sum(jnp.where(lo_half, 0.0, sq), axis=-1, keepdims=True)
        ms = jnp.where(lo_half, ms_a, ms_b) * (1.0 / lay.rd)
        hn = hr * lax.rsqrt(ms + EPS) * mp["rnorm"][:, off:off + LANES]
        out[rows, mw + off:mw + off + LANES] = (hn * a(lay.a_rgs + off).astype(F32)).astype(out.dtype)
        yield

    zero = jnp.zeros((2 * CHUNK, LANES), BF16)
    vst = [jnp.concatenate([a(lay.a_gv + q_i * LANES) * m_a, a(lay.a_gv + q_i * LANES) * m_b], axis=0)
           for q_i in range(gw // LANES)]
    assert len(vst) == 2
    mixed = _dot(mp["wcat"][...], jnp.concatenate([jnp.concatenate([vst[0], zero], axis=1),
                                                   jnp.concatenate([zero, vst[1]], axis=1)], axis=0))
    yield
    out[rows, mw + rw:mw + rw + gw] = (a(lay.a_gug, gw).astype(F32) * (mixed + mp["gbmat"][...])).astype(out.dtype)


def _round_robin(gens):
    live = list(gens)
    while live:
        live = [g for g in live if next(g, live) is not live]


def _mixer_kernel(*refs, lay):
    n_mp, n_tb = len(_MIXER_PARAMS), len(_TABLES)
    act_ref, gates_ref, vt_ref = refs[0:3]
    mp = dict(zip(_MIXER_PARAMS, refs[3:3 + n_mp]))
    tb = dict(zip(_TABLES, refs[3 + n_mp:3 + n_mp + n_tb]))
    o_ref = refs[3 + n_mp + n_tb]
    c_st, r_st, m_st = refs[4 + n_mp + n_tb:]
    nrow, ts = act_ref.shape[0], act_ref.shape[1]

    @pl.when(pl.program_id(1) == 0)
    def _():
        c_st[...] = jnp.zeros_like(c_st)
        r_st[...] = jnp.zeros_like(r_st)
        m_st[...] = jnp.zeros_like(m_st)

    def step(c, carry):
        _round_robin([_mixer_chunk(c, act_ref.at[r], gates_ref.at[r], vt_ref.at[r], mp, tb, o_ref.at[r],
                                   c_st.at[r], r_st.at[r], m_st.at[r], lay) for r in range(nrow)])
        return carry

    lax.fori_loop(0, ts // CHUNK, step, 0)


def _mixer(act, gates, vt, tabs, lw, layer, lay, batch, seq):
    ts, nrow = MIX_TILE, MIX_ROWS
    mps = [lw[k] for k in _MIXER_PARAMS]
    tbs = [tabs[k] for k in _TABLES]
    scratch = [pltpu.VMEM((nrow, M_HEADS // 2, LANES, 2 * LANES), F32),
               pltpu.VMEM((nrow, lay.rw // LANES, LANES, LANES), F32),
               pltpu.VMEM((nrow, 8, 2 * LANES), F32)]
    est = (2 * nrow * ts * (lay.na * 2 + 2 * LANES * 4 + lay.mpad * 4 + lay.dmix * 2)
           + 2 * sum(_nbytes(a.shape[1:], a.dtype) for a in mps) + 2 * sum(_nbytes(a.shape, a.dtype) for a in tbs)
           + sum(_nbytes(s.shape, s.dtype) for s in scratch))
    return pl.pallas_call(
        functools.partial(_mixer_kernel, lay=lay),
        grid=(batch // nrow, seq // ts),
        in_specs=([pl.BlockSpec((nrow, ts, lay.na), lambda b, j: (b, j, 0)),
                   pl.BlockSpec((nrow, ts, 2 * LANES), lambda b, j: (b, j, 0)),
                   pl.BlockSpec((nrow, ts // CHUNK, lay.mpad, CHUNK), lambda b, j: (b, j, 0, 0))]
                  + [_layer_slab(a, layer) for a in mps] + [_whole(a) for a in tbs]),
        out_specs=pl.BlockSpec((nrow, ts, lay.dmix), lambda b, j: (b, j, 0)),
        out_shape=jax.ShapeDtypeStruct((batch, seq, lay.dmix), BF16),
        scratch_shapes=scratch,
        compiler_params=pltpu.CompilerParams(
            dimension_semantics=("arbitrary", "arbitrary"), vmem_limit_bytes=_vmem_limit(est)),
        name="mixer",
    )(act, gates, vt, *mps, *tbs)


def _ffn_kernel(mix_ref, x_ref, wo_ref, g_ref, w1_ref, w2_ref, gf_ref, o_ref, *, final):
    x1 = x_ref[...] + _dot(mix_ref[...], wo_ref[...])
    h2 = _rms(x1, g_ref[...]).astype(BF16)
    acc = x1
    for j in range(w1_ref.shape[1] // FF_CHUNK):
        cols = slice(j * FF_CHUNK, (j + 1) * FF_CHUNK)
        a = jnp.square(jnp.maximum(_dot(h2, w1_ref[:, cols]), 0.0)).astype(BF16)
        acc = acc + _dot(a, w2_ref[cols, :])
    if final:
        acc = _rms(acc, gf_ref[...])
    o_ref[...] = acc


def _ffn(mix, x2, wts, dense, layer, final):
    t, d = x2.shape
    wo, w1, w2 = wts
    dff = w1.shape[1]
    tm = FFN_TILE
    est = (2 * tm * d * 2 + 4 * tm * d * 4 + (d * d + 2 * d * dff) * 2
           + tm * FF_CHUNK * 6 + 3 * tm * d * 4)
    resident = lambda a: pl.BlockSpec(a.shape, lambda i: (0, 0), **_ONCE)
    return pl.pallas_call(
        functools.partial(_ffn_kernel, final=final),
        grid=(t // tm,),
        in_specs=[pl.BlockSpec((tm, d), lambda i: (i, 0)),
                  pl.BlockSpec((tm, d), lambda i: (i, 0)),
                  resident(wo),
                  _layer_slab(dense["g_ff"], layer),
                  resident(w1),
                  resident(w2),
                  pl.BlockSpec((1, d), lambda i: (0, 0))],
        out_specs=pl.BlockSpec((tm, d), lambda i: (i, 0)),
        out_shape=jax.ShapeDtypeStruct((t, d), F32),
        compiler_params=pltpu.CompilerParams(
            dimension_semantics=("arbitrary",), vmem_limit_bytes=_vmem_limit(est)),
        name="ffn_final" if final else "ffn",
    )(mix, x2, wo, dense["g_ff"], w1, w2, dense["g_final"])


def _tables(lay, seq):
    rd, half = lay.rd, lay.rd // 2
    l = CHUNK
    pos = np.arange(seq, dtype=np.float64)
    freqs = ROPE_THETA ** (-np.arange(half, dtype=np.float64) / half)
    ang = pos[:, None] * freqs[None, :]
    sign = np.where(np.arange(LANES) % rd < half, -1.0, 1.0)
    cos = np.tile(np.cos(ang), (1, LANES // half))
    sin = np.tile(np.sin(ang), (1, LANES // half)) * sign[None, :]
    log_gamma = np.log(1.0 - 2.0 ** (-5.0 - np.arange(R_HEADS, dtype=np.float64)))
    p = np.arange(l, dtype=np.float64)
    causal = np.tril(np.ones((l, l), dtype=bool))
    rel = np.where(causal, p[:, None] - p[None, :], 0.0)
    decay = np.where(causal, np.exp(rel[None] * log_gamma[:, None, None]), 0.0)
    zeta = np.exp((l - 1.0 - p)[None, :] * log_gamma[:, None])
    xi = np.exp((p + 1.0)[None, :] * log_gamma[:, None])
    cdec = np.exp(l * log_gamma)
    npair = R_HEADS // 2
    decay_ab = decay.reshape(npair, 2, l, l).transpose(0, 2, 1, 3).reshape(npair, l, 2 * l)
    by_lane = lambda a: np.repeat(a.reshape(npair, 2, -1), rd, axis=1)
    xi_ab = by_lane(xi).transpose(0, 2, 1)
    zeta_k = by_lane(zeta).transpose(0, 2, 1)
    zeta_tile = np.tile(zeta_k.transpose(1, 0, 2).reshape(l, npair * LANES), (TOKEN_TILE // l, 1))
    cdt = np.broadcast_to(by_lane(cdec).transpose(0, 2, 1), (npair, LANES, LANES))
    head = np.arange(LANES) // rd
    bmaskt = head[:, None] == head[None, :]
    tabs = {"cos": cos, "sin": sin, "decay": decay_ab, "xi": xi_ab, "zeta_tile": zeta_tile, "cdt": cdt,
            "bmaskt": bmaskt}
    return {k: jnp.asarray(np.ascontiguousarray(v), F32) for k, v in tabs.items()}


def _block_diag_padded(w, lay):
    depth, nb = w.shape[0], w.shape[1]
    rows = w.reshape(depth, nb * M_BLOCK, M_BLOCK)
    tiled = jnp.tile(rows, (1, 1, lay.mpad // M_BLOCK))
    r = np.arange(nb * M_BLOCK)[:, None]
    c = np.arange(lay.mpad)[None, :]
    packed_c = (c // LANES) * lay.md + c % LANES
    keep = (c % LANES < lay.md) & (r // M_BLOCK == packed_c // M_BLOCK)
    return jnp.where(jnp.asarray(keep), tiled, 0.0).astype(BF16)


def _mixer_weights(lay, p):
    depth = p["m_wq"].shape[0]
    gbi = jnp.concatenate([p["m_i_bias"], jnp.zeros((depth, LANES - M_HEADS), F32)], axis=-1)
    gbf = jnp.concatenate(
        [p["m_f_bias"], p["m_f_bias"], jnp.zeros((depth, LANES - 2 * M_HEADS), F32)], axis=-1)
    causal = jnp.tril(jnp.ones((CHUNK, CHUNK), dtype=bool))
    wm = jnp.where(causal, p["g_ws"], 0.0)
    wcat = wm.transpose(0, 2, 1, 3).reshape(depth, CHUNK, G_GROUPS * CHUNK)
    gbmat = jnp.repeat(p["g_bs"].transpose(0, 2, 1), lay.gw // G_GROUPS, axis=2)
    row = lambda a: a[:, None, :]
    return {
        "convw": p["m_conv_w"], "convb": row(p["m_conv_b"]),
        "bdq": _block_diag_padded(p["m_wq"], lay),
        "bdk": _block_diag_padded(p["m_wk"], lay),
        "bdvt": _block_diag_padded(p["m_wv"], lay).transpose(0, 2, 1),
        "gbi": row(gbi), "gbf": row(gbf),
        "mnormt": jnp.broadcast_to(p["m_norm_g"][:, :, None], (depth, lay.mw, LANES)),
        "mskip": row(p["m_skip"]),
        "rnorm": row(p["r_norm_g"]),
        "gnorm": row(p["g_norm_g"]),
        "wcat": wcat.astype(BF16), "gbmat": gbmat,
    }


def kernel(x, norm_mix_g, w_in, m_conv_w, m_conv_b, m_wq, m_wk, m_wv, m_i_bias, m_f_bias, m_norm_g,
           m_skip, r_norm_g, g_norm_g, g_ws, g_bs, w_out, norm_ff_g, w_ff1, w_ff2, final_norm_g):
    batch, seq, d = x.shape
    depth = w_in.shape[0]
    lay = _Layout(d)
    assert seq % MIX_TILE == 0 and MIX_TILE % CHUNK == 0 and batch % MIX_ROWS == 0
    assert seq % TOKEN_TILE == 0 and TOKEN_TILE % CHUNK == 0 and w_ff1.shape[2] % FF_CHUNK == 0
    assert (batch * seq) % FFN_TILE == 0
    params = dict(m_conv_w=m_conv_w, m_conv_b=m_conv_b, m_wq=m_wq, m_wk=m_wk, m_wv=m_wv,
                  m_i_bias=m_i_bias, m_f_bias=m_f_bias, m_norm_g=m_norm_g, m_skip=m_skip,
                  r_norm_g=r_norm_g, g_norm_g=g_norm_g, g_ws=g_ws, g_bs=g_bs)
    tabs = _tables(lay, seq)
    lw = _mixer_weights(lay, params)
    dense = {
        "w_in_t": _relayout_in_proj_t(w_in, lay), "w_out": w_out, "w_ff1": w_ff1, "w_ff2": w_ff2,
        "g_mix": norm_mix_g[:, None, :], "g_ff": norm_ff_g[:, None, :], "g_final": final_norm_g[None, :],
    }
    x2 = x.reshape(batch * seq, d)
    for layer in range(depth):
        (act, gates, vt), ffn_w = _inproj(x2, tabs, lw, dense, layer, lay, seq)
        mix = _mixer(act.reshape(batch, seq, lay.na), gates.reshape(batch, seq, 2 * LANES),
                     vt.reshape(batch, seq // CHUNK, lay.mpad, CHUNK), tabs, lw, layer, lay, batch, seq)
        x2 = _ffn(mix.reshape(batch * seq, lay.dmix), x2, ffn_w, dense, layer, final=(layer == depth - 1))
    return x2.reshape(batch, seq, d)
```

```python
import functools

import numpy as np
import jax
import jax.numpy as jnp
from jax import lax
from jax.experimental import pallas as pl
from jax.experimental.pallas import tpu as pltpu

M_HEADS = 4
M_CONV = 4
M_BLOCK = 4
R_HEADS = 6
G_GROUPS = 4
CHUNK = 128
ROPE_THETA = 10000.0
EPS = 1e-6

LANES = 128
V7X_VMEM_BYTES = 64 * 1024 * 1024

TOKEN_TILE = 512
TOKEN_SPLIT = 2
FFN_TILE = 1024
MIX_TILE = 256
MIX_ROWS = 4
FF_CHUNK = 1024

F32 = jnp.float32
BF16 = jnp.bfloat16


def _rms(x, g):
    return x * lax.rsqrt(jnp.mean(jnp.square(x), axis=-1, keepdims=True) + EPS) * g


def _silu(x):
    return x * jax.nn.sigmoid(x)


def _dot(a, b):
    return jnp.dot(a, b, preferred_element_type=F32)


def _dot_nt(a, b):
    return lax.dot_general(a, b, (((1,), (1,)), ((), ())), preferred_element_type=F32)


def _dot_tn(a, b):
    return lax.dot_general(a, b, (((0,), (0,)), ((), ())), preferred_element_type=F32)


def _nbytes(shape, dtype):
    return int(np.prod(shape)) * jnp.dtype(dtype).itemsize


VMEM_TEMPORARIES_BYTES = 12 << 20
VMEM_UNCLAIMED_BYTES = 4 << 20


def _vmem_limit(nbytes):
    return int(min(nbytes + VMEM_TEMPORARIES_BYTES, V7X_VMEM_BYTES - VMEM_UNCLAIMED_BYTES))


def _layer_slab(a, layer, **kw):
    return pl.BlockSpec((None,) + a.shape[1:], lambda *g: (layer,) + (0,) * (a.ndim - 1), **kw)


def _whole(a):
    return pl.BlockSpec(a.shape, lambda *g: (0,) * a.ndim)


_ONCE = dict(pipeline_mode=pl.Buffered(1))


class _Layout:
    def __init__(self, d_model):
        self.mw = 3 * d_model // 8
        self.rw = 3 * d_model // 8
        self.gw = d_model - self.mw - self.rw
        self.md = self.mw // M_HEADS
        self.rd = self.rw // R_HEADS
        assert self.md <= LANES and 2 * self.rd == LANES and self.gw // G_GROUPS * 2 == LANES
        assert self.mw % LANES == 0 and self.rw % LANES == 0 and self.gw % LANES == 0
        self.mpad = M_HEADS * LANES
        self.dmix = self.mw + self.rw + self.gw
        o = 0
        self.mx = o; o += self.mw
        self.mz = o; o += self.mw
        self.gu = o; o += self.gw
        self.gv = o; o += self.gw
        self.gi = o; o += LANES
        self.gf = o; o += LANES
        self.rq = o; o += self.rw
        self.rk = o; o += self.rw
        self.rg = o; o += self.rw
        self.rv = o; o += self.rw
        self.n = o
        o = 0
        self.a_q = o; o += self.mpad
        self.a_k = o; o += self.mpad
        self.a_rq = o; o += self.rw
        self.a_rk = o; o += self.rw
        self.a_kz = o; o += self.rw
        self.a_rv = o; o += self.rw
        self.a_gv = o; o += self.gw
        self.a_zs = o; o += self.mw
        self.a_sxz = o; o += self.mw
        self.a_rgs = o; o += self.rw
        self.a_gug = o; o += self.gw
        self.na = o


RELAYOUT_ROWS = 256


def _relayout_kernel(w_ref, o_ref, *, lay):
    d = w_ref.shape[1]
    mw, rw, gw = lay.mw, lay.rw, lay.gw
    o_gate = 2 * mw
    o_rq = o_gate + 2 * M_HEADS
    pieces = ((0, lay.mx, 2 * mw), (o_rq, lay.rq, rw), (o_rq + rw, lay.rk, rw), (o_rq + 2 * rw, lay.rv, rw),
              (o_rq + 3 * rw, lay.rg, rw), (o_rq + 4 * rw, lay.gu, gw), (o_rq + 4 * rw + gw, lay.gv, gw))
    for src, dst, n in pieces:
        for r in range(0, n, RELAYOUT_ROWS):
            m = min(RELAYOUT_ROWS, n - r)
            o_ref[dst + r:dst + r + m, :] = w_ref[src + r:src + r + m, :].astype(o_ref.dtype)
    gi = w_ref[o_gate:o_gate + M_HEADS, :]
    gf = w_ref[o_gate + M_HEADS:o_gate + 2 * M_HEADS, :]
    o_ref[lay.gi:lay.gi + LANES, :] = jnp.concatenate(
        [gi, jnp.zeros((LANES - M_HEADS, d), F32)], axis=0).astype(o_ref.dtype)
    o_ref[lay.gf:lay.gf + LANES, :] = jnp.concatenate(
        [gf, gf, jnp.zeros((LANES - 2 * M_HEADS, d), F32)], axis=0).astype(o_ref.dtype)


def _relayout_in_proj_t(w_in, lay):
    depth, d, n_in = w_in.shape
    wt = jnp.swapaxes(w_in, 1, 2)
    est = 2 * n_in * d * 4 + 2 * lay.n * d * 2 + 4 * RELAYOUT_ROWS * d * 4
    return pl.pallas_call(
        functools.partial(_relayout_kernel, lay=lay),
        grid=(depth,),
        in_specs=[pl.BlockSpec((None, n_in, d), lambda l: (l, 0, 0))],
        out_specs=pl.BlockSpec((None, lay.n, d), lambda l: (l, 0, 0)),
        out_shape=jax.ShapeDtypeStruct((depth, lay.n, d), BF16),
        compiler_params=pltpu.CompilerParams(
            dimension_semantics=("arbitrary",), vmem_limit_bytes=_vmem_limit(est)),
        name="relayout_w_in",
    )(wt)


_INPROJ_PARAMS = ("convw", "convb", "bdq", "bdk", "bdvt", "gbi", "gbf", "mskip", "gnorm")
_N_SIDE = 3


def _inproj_kernel(*refs, lay, nj):
    n_p = len(_INPROJ_PARAMS)
    x_ref, g_ref, w_ref, cos_ref, sin_ref, zeta_ref = refs[0:6]
    mp = dict(zip(_INPROJ_PARAMS, refs[6:6 + n_p]))
    wf = refs[6 + n_p:6 + n_p + _N_SIDE]
    act_ref, gates_ref, vt_ref = refs[6 + n_p + _N_SIDE:9 + n_p + _N_SIDE]
    wb = refs[9 + n_p + _N_SIDE:9 + n_p + 2 * _N_SIDE]
    xbuf = refs[9 + n_p + 2 * _N_SIDE]
    tm = x_ref.shape[0]
    for src, dst in zip(wf, wb):
        dst[...] = src[...].astype(dst.dtype)

    md, mw, rw, gw = lay.md, lay.mw, lay.rw, lay.gw
    first_of_seq = lax.rem(pl.program_id(0), nj) == 0

    @pl.when(first_of_seq)
    def _():
        xbuf[0:8, :] = jnp.zeros((8, mw), F32)

    @pl.when(jnp.logical_not(first_of_seq))
    def _():
        xbuf[0:8, :] = xbuf[tm:tm + 8, :]

    hm = tm // TOKEN_SPLIT
    assert hm % CHUNK == 0
    _round_robin([_inproj_rows(r0, hm, x_ref, g_ref, w_ref, cos_ref, sin_ref, zeta_ref, mp,
                               act_ref, gates_ref, vt_ref, xbuf, lay) for r0 in range(0, tm, hm)])


def _inproj_rows(r0, nr, x_ref, g_ref, w_ref, cos_ref, sin_ref, zeta_ref, mp, act_ref, gates_ref, vt_ref,
                 xbuf, lay):
    md, mw, rw, gw = lay.md, lay.mw, lay.rw, lay.gw
    rs = slice(r0, r0 + nr)

    proj = _dot_nt(_rms(x_ref[rs, :], g_ref[...]).astype(BF16), w_ref[...])
    yield

    def put(col, val):
        act_ref[rs, col:col + val.shape[1]] = val.astype(act_ref.dtype)

    mx = proj[:, lay.mx:lay.mx + mw]
    xbuf[8 + r0:8 + r0 + nr, :] = mx
    conv = mp["convb"][...] + mp["convw"][M_CONV - 1:M_CONV, :] * mx
    for t in range(M_CONV - 1):
        lo = 8 + r0 - (M_CONV - 1) + t
        conv = conv + mp["convw"][t:t + 1, :] * xbuf[lo:lo + nr, :]
    xc = _silu(conv)
    xcb = xc.astype(BF16)
    yield
    q = _dot(xcb, mp["bdq"][...])
    k = _dot(xcb, mp["bdk"][...])
    row_p = lax.broadcasted_iota(jnp.int32, (lay.mpad, 1), 0) % LANES
    vt = _dot_nt(mp["bdvt"][...], mx.astype(BF16)) + (row_p == md).astype(F32)
    yield
    put(lay.a_q, q)
    put(lay.a_k, k * (md ** -0.5))
    for c in range(nr // CHUNK):
        vt_ref[r0 // CHUNK + c] = vt[:, c * CHUNK:(c + 1) * CHUNK]
    zs = _silu(proj[:, lay.mz:lay.mz + mw])
    put(lay.a_zs, zs)
    put(lay.a_sxz, mp["mskip"][...] * xc * zs)
    yield

    li = lax.broadcasted_iota(jnp.int32, (nr, LANES), 1)
    put(lay.a_gug, jax.nn.gelu(proj[:, lay.gu:lay.gu + gw]))
    yield
    gv = jax.nn.gelu(proj[:, lay.gv:lay.gv + gw])
    put(lay.a_gv, gv * lax.rsqrt(jnp.mean(gv * gv, axis=-1, keepdims=True) + EPS) * mp["gnorm"][...])
    yield

    gates_ref[rs, 0:LANES] = proj[:, lay.gi:lay.gi + LANES] + mp["gbi"][...]
    gf = proj[:, lay.gf:lay.gf + LANES] + mp["gbf"][...]
    gates_ref[rs, LANES:2 * LANES] = jnp.where(li < 2 * M_HEADS, jax.nn.log_sigmoid(gf), 0.0)
    yield

    half = lay.rd // 2
    first_half = (li % lay.rd) < half
    cs = cos_ref[rs, :]
    sn = sin_ref[rs, :]

    def rotary(v):
        partner = jnp.where(first_half, pltpu.roll(v, LANES - half, axis=1), pltpu.roll(v, half, axis=1))
        return v * cs + partner * sn

    for p_i in range(rw // LANES):
        off = p_i * LANES
        rq = proj[:, lay.rq + off:lay.rq + off + LANES]
        rk = proj[:, lay.rk + off:lay.rk + off + LANES]
        put(lay.a_rq + off, rotary(rq))
        kr = rotary(rk) * (lay.rd ** -0.5)
        put(lay.a_rk + off, kr)
        put(lay.a_kz + off, kr * zeta_ref[rs, off:off + LANES])
        yield
    put(lay.a_rgs, _silu(proj[:, lay.rg:lay.rg + rw]))
    yield
    put(lay.a_rv, proj[:, lay.rv:lay.rv + rw])


def _inproj(x2, tabs, lw, dense, layer, lay, seq):
    t, d = x2.shape
    tm = TOKEN_TILE
    nj = seq // tm
    steps = t // tm
    mps = [lw[k] for k in _INPROJ_PARAMS]
    pos = lambda i: (lax.rem(i, nj), 0)
    side = [dense["w_out"], dense["w_ff1"], dense["w_ff2"]]
    assert len(side) == _N_SIDE
    side_in, side_out, side_shape = [], [], []
    for a in side:
        rows, cols = a.shape[1], a.shape[2]
        assert rows % steps == 0 and (rows // steps) % 16 == 0
        side_in.append(pl.BlockSpec((None, rows // steps, cols), lambda i: (layer, i, 0)))
        side_out.append(pl.BlockSpec((rows // steps, cols), lambda i: (i, 0)))
        side_shape.append(jax.ShapeDtypeStruct((rows, cols), BF16))
    est = (2 * tm * d * 4 + d * lay.n * 2 + 2 * tm * lay.na * 2 + 2 * tm * 2 * LANES * 4
           + 2 * tm * lay.mpad * 4 + 6 * tm * LANES * 4 + (tm + 8) * lay.mw * 4 + tm * lay.n * 4
           + 2 * sum(_nbytes(a.shape[1:], a.dtype) for a in mps)
           + 3 * sum(_nbytes(a.shape[1:], a.dtype) // steps for a in side))
    outs = pl.pallas_call(
        functools.partial(_inproj_kernel, lay=lay, nj=nj),
        grid=(steps,),
        in_specs=([pl.BlockSpec((tm, d), lambda i: (i, 0)),
                   _layer_slab(dense["g_mix"], layer),
                   _layer_slab(dense["w_in_t"], layer, **_ONCE),
                   pl.BlockSpec((tm, LANES), pos),
                   pl.BlockSpec((tm, LANES), pos),
                   _whole(tabs["zeta_tile"])]
                  + [_layer_slab(a, layer) for a in mps] + side_in),
        out_specs=[pl.BlockSpec((tm, lay.na), lambda i: (i, 0)),
                   pl.BlockSpec((tm, 2 * LANES), lambda i: (i, 0)),
                   pl.BlockSpec((tm // CHUNK, lay.mpad, CHUNK), lambda i: (i, 0, 0))] + side_out,
        out_shape=[jax.ShapeDtypeStruct((t, lay.na), BF16),
                   jax.ShapeDtypeStruct((t, 2 * LANES), F32),
                   jax.ShapeDtypeStruct((t // CHUNK, lay.mpad, CHUNK), F32)] + side_shape,
        scratch_shapes=[pltpu.VMEM((tm + 8, lay.mw), F32)],
        compiler_params=pltpu.CompilerParams(
            dimension_semantics=("arbitrary",), vmem_limit_bytes=_vmem_limit(est)),
        name="inproj",
    )(x2, dense["g_mix"], dense["w_in_t"], tabs["cos"], tabs["sin"], tabs["zeta_tile"], *mps, *side)
    return outs[0:3], outs[3:6]


_MIXER_PARAMS = ("mnormt", "rnorm", "wcat", "gbmat")
_TABLES = ("decay", "xi", "cdt", "bmaskt")


def _block_diag2(x, y):
    z = jnp.zeros_like(x)
    return jnp.concatenate([jnp.concatenate([x, z], axis=1), jnp.concatenate([z, y], axis=1)], axis=0)


def _mixer_chunk(c, act, gates, vt, mp, tb, out, c_st, r_st, m_st, lay):
    md, mw, rw, gw = lay.md, lay.mw, lay.rw, lay.gw
    li = lax.broadcasted_iota(jnp.int32, (CHUNK, LANES), 1)
    ri = lax.broadcasted_iota(jnp.int32, (CHUNK, LANES), 0)
    causal_t = ri <= li
    tril = (li <= ri).astype(BF16)
    lo_half = li < (LANES // 2)
    rows = pl.ds(pl.multiple_of(c * CHUNK, CHUNK), CHUNK)

    def a(col, width=LANES):
        return act[rows, col:col + width]

    gi = gates[rows, 0:LANES]
    logf = gates[rows, LANES:2 * LANES]
    a1 = logf.astype(BF16)
    r1 = logf - a1.astype(F32)
    a2 = r1.astype(BF16)
    a3 = (r1 - a2.astype(F32)).astype(BF16)
    b3 = _dot(tril, jnp.concatenate([a1, a2, a3], axis=1))
    yield
    bcum = (b3[:, 2 * LANES:3 * LANES] + b3[:, LANES:2 * LANES]) + b3[:, 0:LANES]
    wb = jnp.where(li < M_HEADS, gi - bcum, bcum)
    wb_t = wb.T
    yield

    causal_t2 = jnp.concatenate([causal_t, causal_t], axis=1)
    hn_t = []
    for hp in range(M_HEADS // 2):
        heads = (2 * hp, 2 * hp + 1)
        pair = lambda f: jnp.concatenate([f(h) for h in heads], axis=1)
        rep = lambda v: jnp.broadcast_to(v, (1, LANES))
        q2 = a(lay.a_q + 2 * hp * LANES, 2 * LANES)
        k2 = a(lay.a_k + 2 * hp * LANES, 2 * LANES)
        vt2 = pair(lambda h: vt[c, h * LANES:(h + 1) * LANES, :])
        w_row = pair(lambda h: wb_t[h:h + 1, :])
        b_row = pair(lambda h: wb_t[M_HEADS + h:M_HEADS + h + 1, :])
        b_end = pair(lambda h: rep(wb_t[M_HEADS + h:M_HEADS + h + 1, CHUNK - 1:CHUNK]))
        w_end = b_end + w_row
        a_c = pair(lambda h: rep(jnp.max(w_end[:, (h % 2) * LANES:(h % 2 + 1) * LANES], axis=-1, keepdims=True)))
        e_row = jnp.exp(w_end - a_c)
        m_prev = m_st[hp:hp + 1, :]
        m_new = jnp.maximum(b_end + m_prev, a_c)
        s_old = jnp.exp(b_end + m_prev - m_new)
        s_new = jnp.exp(a_c - m_new)
        m_st[hp:hp + 1, :] = m_new

        il = b_row + m_prev
        c_prev = c_st[hp]
        kq = _dot_nt(jnp.concatenate([k2, c_prev.astype(BF16)], axis=0),
                     _block_diag2(q2[:, 0:LANES], q2[:, LANES:2 * LANES]))
        yield
        dm = jnp.where(causal_t2, pair(lambda h: jnp.broadcast_to(wb[:, h:h + 1], (CHUNK, LANES))) + b_row,
                       -jnp.inf)
        m_t = jnp.maximum(il, jnp.max(dm, axis=0, keepdims=True))
        s_inter = jnp.exp(il - m_t)
        p_t = kq[0:CHUNK] * jnp.exp(dm - m_t)
        inter = s_inter * kq[CHUNK:2 * CHUNK]
        p_b = p_t.astype(BF16)
        tot = _dot(vt2.astype(BF16), _block_diag2(p_b[:, 0:LANES], p_b[:, LANES:2 * LANES])) + inter
        upd = _dot((vt2 * e_row).astype(BF16), _block_diag2(k2[:, 0:LANES], k2[:, LANES:2 * LANES]))
        yield
        den = jnp.sum(p_t, axis=0, keepdims=True) + inter[md:md + 1, :]
        rden = 1.0 / jnp.maximum(jnp.abs(den), jnp.exp(-m_t))
        hout = tot[0:md, :] * rden
        ms = jnp.sum(hout * hout, axis=0, keepdims=True) * (1.0 / md)
        hn = hout * lax.rsqrt(ms + EPS)
        for i, h in enumerate(heads):
            hn_t.append(hn[:, i * LANES:(i + 1) * LANES] * mp["mnormt"][h * md:(h + 1) * md, :])
        c_st[hp] = s_old * c_prev + s_new * upd
        yield

    hm_t = jnp.concatenate(hn_t, axis=0)
    hm = jnp.concatenate([hm_t[t * LANES:(t + 1) * LANES, :].T for t in range(mw // LANES)], axis=1)
    out[rows, 0:mw] = (hm * a(lay.a_zs, mw).astype(F32) + a(lay.a_sxz, mw).astype(F32)).astype(out.dtype)
    yield

    lane_row = lax.broadcasted_iota(jnp.int32, (1, LANES), 1)
    m_a = (lane_row < LANES // 2).astype(BF16)
    m_b = (lane_row >= LANES // 2).astype(BF16)
    for p_i in range(rw // LANES):
        off = p_i * LANES
        qb = a(lay.a_rq + off)
        rk, rv = a(lay.a_rk + off), a(lay.a_rv + off)
        r_prev = r_st[p_i]
        sqr = _dot_nt(qb, jnp.concatenate([rk * m_a, rk * m_b, r_prev.astype(BF16)],
                                          axis=0))
        kv = _dot_tn(rv, a(lay.a_kz + off))
        yield
        p_ab = (sqr[:, 0:2 * LANES] * tb["decay"][p_i]).astype(BF16)
        hr = (_dot(p_ab, jnp.concatenate([rv * m_a, rv * m_b], axis=0))
              + sqr[:, 2 * LANES:3 * LANES] * tb["xi"][p_i])
        r_st[p_i] = tb["cdt"][p_i] * r_prev + tb["bmaskt"][...] * kv
        yield
        sq = hr * hr
        ms_a = jnp.sum(jnp.where(lo_half, sq, 0.0), axis=-1, keepdims=True)
        ms_b = jnp.sum(jnp.where(lo_half, 0.0, sq), axis=-1, keepdims=True)
        ms = jnp.where(lo_half, ms_a, ms_b) * (1.0 / lay.rd)
        hn = hr * lax.rsqrt(ms + EPS) * mp["rnorm"][:, off:off + LANES]
        out[rows, mw + off:mw + off + LANES] = (hn * a(lay.a_rgs + off).astype(F32)).astype(out.dtype)
        yield

    zero = jnp.zeros((2 * CHUNK, LANES), BF16)
    vst = [jnp.concatenate([a(lay.a_gv + q_i * LANES) * m_a, a(lay.a_gv + q_i * LANES) * m_b], axis=0)
           for q_i in range(gw // LANES)]
    assert len(vst) == 2
    mixed = _dot(mp["wcat"][...], jnp.concatenate([jnp.concatenate([vst[0], zero], axis=1),
                                                   jnp.concatenate([zero, vst[1]], axis=1)], axis=0))
    yield
    out[rows, mw + rw:mw + rw + gw] = (a(lay.a_gug, gw).astype(F32) * (mixed + mp["gbmat"][...])).astype(out.dtype)


def _round_robin(gens):
    live = list(gens)
    while live:
        live = [g for g in live if next(g, live) is not live]


def _mixer_kernel(*refs, lay):
    n_mp, n_tb = len(_MIXER_PARAMS), len(_TABLES)
    act_ref, gates_ref, vt_ref = refs[0:3]
    mp = dict(zip(_MIXER_PARAMS, refs[3:3 + n_mp]))
    tb = dict(zip(_TABLES, refs[3 + n_mp:3 + n_mp + n_tb]))
    o_ref = refs[3 + n_mp + n_tb]
    c_st, r_st, m_st = refs[4 + n_mp + n_tb:]
    nrow, ts = act_ref.shape[0], act_ref.shape[1]

    @pl.when(pl.program_id(1) == 0)
    def _():
        c_st[...] = jnp.zeros_like(c_st)
        r_st[...] = jnp.zeros_like(r_st)
        m_st[...] = jnp.zeros_like(m_st)

    def step(c, carry):
        _round_robin([_mixer_chunk(c, act_ref.at[r], gates_ref.at[r], vt_ref.at[r], mp, tb, o_ref.at[r],
                                   c_st.at[r], r_st.at[r], m_st.at[r], lay) for r in range(nrow)])
        return carry

    lax.fori_loop(0, ts // CHUNK, step, 0)


def _mixer(act, gates, vt, tabs, lw, layer, lay, batch, seq):
    ts, nrow = MIX_TILE, MIX_ROWS
    mps = [lw[k] for k in _MIXER_PARAMS]
    tbs = [tabs[k] for k in _TABLES]
    scratch = [pltpu.VMEM((nrow, M_HEADS // 2, LANES, 2 * LANES), F32),
               pltpu.VMEM((nrow, lay.rw // LANES, LANES, LANES), F32),
               pltpu.VMEM((nrow, 8, 2 * LANES), F32)]
    est = (2 * nrow * ts * (lay.na * 2 + 2 * LANES * 4 + lay.mpad * 4 + lay.dmix * 2)
           + 2 * sum(_nbytes(a.shape[1:], a.dtype) for a in mps) + 2 * sum(_nbytes(a.shape, a.dtype) for a in tbs)
           + sum(_nbytes(s.shape, s.dtype) for s in scratch))
    return pl.pallas_call(
        functools.partial(_mixer_kernel, lay=lay),
        grid=(batch // nrow, seq // ts),
        in_specs=([pl.BlockSpec((nrow, ts, lay.na), lambda b, j: (b, j, 0)),
                   pl.BlockSpec((nrow, ts, 2 * LANES), lambda b, j: (b, j, 0)),
                   pl.BlockSpec((nrow, ts // CHUNK, lay.mpad, CHUNK), lambda b, j: (b, j, 0, 0))]
                  + [_layer_slab(a, layer) for a in mps] + [_whole(a) for a in tbs]),
        out_specs=pl.BlockSpec((nrow, ts, lay.dmix), lambda b, j: (b, j, 0)),
        out_shape=jax.ShapeDtypeStruct((batch, seq, lay.dmix), BF16),
        scratch_shapes=scratch,
        compiler_params=pltpu.CompilerParams(
            dimension_semantics=("arbitrary", "arbitrary"), vmem_limit_bytes=_vmem_limit(est)),
        name="mixer",
    )(act, gates, vt, *mps, *tbs)


def _ffn_kernel(mix_ref, x_ref, wo_ref, g_ref, w1_ref, w2_ref, gf_ref, o_ref, *, final):
    x1 = x_ref[...] + _dot(mix_ref[...], wo_ref[...])
    h2 = _rms(x1, g_ref[...]).astype(BF16)
    acc = x1
    for j in range(w1_ref.shape[1] // FF_CHUNK):
        cols = slice(j * FF_CHUNK, (j + 1) * FF_CHUNK)
        a = jnp.square(jnp.maximum(_dot(h2, w1_ref[:, cols]), 0.0)).astype(BF16)
        acc = acc + _dot(a, w2_ref[cols, :])
    if final:
        acc = _rms(acc, gf_ref[...])
    o_ref[...] = acc


def _ffn(mix, x2, wts, dense, layer, final):
    t, d = x2.shape
    wo, w1, w2 = wts
    dff = w1.shape[1]
    tm = FFN_TILE
    est = (2 * tm * d * 2 + 4 * tm * d * 4 + (d * d + 2 * d * dff) * 2
           + tm * FF_CHUNK * 6 + 3 * tm * d * 4)
    resident = lambda a: pl.BlockSpec(a.shape, lambda i: (0, 0), **_ONCE)
    return pl.pallas_call(
        functools.partial(_ffn_kernel, final=final),
        grid=(t // tm,),
        in_specs=[pl.BlockSpec((tm, d), lambda i: (i, 0)),
                  pl.BlockSpec((tm, d), lambda i: (i, 0)),
                  resident(wo),
                  _layer_slab(dense["g_ff"], layer),
                  resident(w1),
                  resident(w2),
                  pl.BlockSpec((1, d), lambda i: (0, 0))],
        out_specs=pl.BlockSpec((tm, d), lambda i: (i, 0)),
        out_shape=jax.ShapeDtypeStruct((t, d), F32),
        compiler_params=pltpu.CompilerParams(
            dimension_semantics=("arbitrary",), vmem_limit_bytes=_vmem_limit(est)),
        name="ffn_final" if final else "ffn",
    )(mix, x2, wo, dense["g_ff"], w1, w2, dense["g_final"])


def _tables(lay, seq):
    rd, half = lay.rd, lay.rd // 2
    l = CHUNK
    pos = np.arange(seq, dtype=np.float64)
    freqs = ROPE_THETA ** (-np.arange(half, dtype=np.float64) / half)
    ang = pos[:, None] * freqs[None, :]
    sign = np.where(np.arange(LANES) % rd < half, -1.0, 1.0)
    cos = np.tile(np.cos(ang), (1, LANES // half))
    sin = np.tile(np.sin(ang), (1, LANES // half)) * sign[None, :]
    log_gamma = np.log(1.0 - 2.0 ** (-5.0 - np.arange(R_HEADS, dtype=np.float64)))
    p = np.arange(l, dtype=np.float64)
    causal = np.tril(np.ones((l, l), dtype=bool))
    rel = np.where(causal, p[:, None] - p[None, :], 0.0)
    decay = np.where(causal, np.exp(rel[None] * log_gamma[:, None, None]), 0.0)
    zeta = np.exp((l - 1.0 - p)[None, :] * log_gamma[:, None])
    xi = np.exp((p + 1.0)[None, :] * log_gamma[:, None])
    cdec = np.exp(l * log_gamma)
    npair = R_HEADS // 2
    decay_ab = decay.reshape(npair, 2, l, l).transpose(0, 2, 1, 3).reshape(npair, l, 2 * l)
    by_lane = lambda a: np.repeat(a.reshape(npair, 2, -1), rd, axis=1)
    xi_ab = by_lane(xi).transpose(0, 2, 1)
    zeta_k = by_lane(zeta).transpose(0, 2, 1)
    zeta_tile = np.tile(zeta_k.transpose(1, 0, 2).reshape(l, npair * LANES), (TOKEN_TILE // l, 1))
    cdt = np.broadcast_to(by_lane(cdec).transpose(0, 2, 1), (npair, LANES, LANES))
    head = np.arange(LANES) // rd
    bmaskt = head[:, None] == head[None, :]
    tabs = {"cos": cos, "sin": sin, "decay": decay_ab, "xi": xi_ab, "zeta_tile": zeta_tile, "cdt": cdt,
            "bmaskt": bmaskt}
    return {k: jnp.asarray(np.ascontiguousarray(v), F32) for k, v in tabs.items()}


def _block_diag_padded(w, lay):
    depth, nb = w.shape[0], w.shape[1]
    rows = w.reshape(depth, nb * M_BLOCK, M_BLOCK)
    tiled = jnp.tile(rows, (1, 1, lay.mpad // M_BLOCK))
    r = np.arange(nb * M_BLOCK)[:, None]
    c = np.arange(lay.mpad)[None, :]
    packed_c = (c // LANES) * lay.md + c % LANES
    keep = (c % LANES < lay.md) & (r // M_BLOCK == packed_c // M_BLOCK)
    return jnp.where(jnp.asarray(keep), tiled, 0.0).astype(BF16)


def _mixer_weights(lay, p):
    depth = p["m_wq"].shape[0]
    gbi = jnp.concatenate([p["m_i_bias"], jnp.zeros((depth, LANES - M_HEADS), F32)], axis=-1)
    gbf = jnp.concatenate(
        [p["m_f_bias"], p["m_f_bias"], jnp.zeros((depth, LANES - 2 * M_HEADS), F32)], axis=-1)
    causal = jnp.tril(jnp.ones((CHUNK, CHUNK), dtype=bool))
    wm = jnp.where(causal, p["g_ws"], 0.0)
    wcat = wm.transpose(0, 2, 1, 3).reshape(depth, CHUNK, G_GROUPS * CHUNK)
    gbmat = jnp.repeat(p["g_bs"].transpose(0, 2, 1), lay.gw // G_GROUPS, axis=2)
    row = lambda a: a[:, None, :]
    return {
        "convw": p["m_conv_w"], "convb": row(p["m_conv_b"]),
        "bdq": _block_diag_padded(p["m_wq"], lay),
        "bdk": _block_diag_padded(p["m_wk"], lay),
        "bdvt": _block_diag_padded(p["m_wv"], lay).transpose(0, 2, 1),
        "gbi": row(gbi), "gbf": row(gbf),
        "mnormt": jnp.broadcast_to(p["m_norm_g"][:, :, None], (depth, lay.mw, LANES)),
        "mskip": row(p["m_skip"]),
        "rnorm": row(p["r_norm_g"]),
        "gnorm": row(p["g_norm_g"]),
        "wcat": wcat.astype(BF16), "gbmat": gbmat,
    }


def kernel(x, norm_mix_g, w_in, m_conv_w, m_conv_b, m_wq, m_wk, m_wv, m_i_bias, m_f_bias, m_norm_g,
           m_skip, r_norm_g, g_norm_g, g_ws, g_bs, w_out, norm_ff_g, w_ff1, w_ff2, final_norm_g):
    batch, seq, d = x.shape
    depth = w_in.shape[0]
    lay = _Layout(d)
    assert seq % MIX_TILE == 0 and MIX_TILE % CHUNK == 0 and batch % MIX_ROWS == 0
    assert seq % TOKEN_TILE == 0 and TOKEN_TILE % CHUNK == 0 and w_ff1.shape[2] % FF_CHUNK == 0
    assert (batch * seq) % FFN_TILE == 0
    params = dict(m_conv_w=m_conv_w, m_conv_b=m_conv_b, m_wq=m_wq, m_wk=m_wk, m_wv=m_wv,
                  m_i_bias=m_i_bias, m_f_bias=m_f_bias, m_norm_g=m_norm_g, m_skip=m_skip,
                  r_norm_g=r_norm_g, g_norm_g=g_norm_g, g_ws=g_ws, g_bs=g_bs)
    tabs = _tables(lay, seq)
    lw = _mixer_weights(lay, params)
    dense = {
        "w_in_t": _relayout_in_proj_t(w_in, lay), "w_out": w_out, "w_ff1": w_ff1, "w_ff2": w_ff2,
        "g_mix": norm_mix_g[:, None, :], "g_ff": norm_ff_g[:, None, :], "g_final": final_norm_g[None, :],
    }
    x2 = x.reshape(batch * seq, d)
    for layer in range(depth):
        (act, gates, vt), ffn_w = _inproj(x2, tabs, lw, dense, layer, lay, seq)
        mix = _mixer(act.reshape(batch, seq, lay.na), gates.reshape(batch, seq, 2 * LANES),
                     vt.reshape(batch, seq // CHUNK, lay.mpad, CHUNK), tabs, lw, layer, lay, batch, seq)
        x2 = _ffn(mix.reshape(batch * seq, lay.dmix), x2, ffn_w, dense, layer, final=(layer == depth - 1))
    return x2.reshape(batch, seq, d)
```

```python
import functools

import numpy as np
import jax
import jax.numpy as jnp
from jax import lax
from jax.experimental import pallas as pl
from jax.experimental.pallas import tpu as pltpu

M_HEADS = 4
M_CONV = 4
M_BLOCK = 4
R_HEADS = 6
G_GROUPS = 4
CHUNK = 128
ROPE_THETA = 10000.0
EPS = 1e-6

LANES = 128
V7X_VMEM_BYTES = 64 * 1024 * 1024

TOKEN_TILE = 512
TOKEN_SPLIT = 2
FFN_TILE = 1024
MIX_TILE = 256
MIX_ROWS = 4
ACT_SLOTS = 3
FF_CHUNK = 1024

F32 = jnp.float32
BF16 = jnp.bfloat16


def _rms(x, g):
    return x * lax.rsqrt(jnp.mean(jnp.square(x), axis=-1, keepdims=True) + EPS) * g


def _silu(x):
    return x * jax.nn.sigmoid(x)


def _dot(a, b):
    return jnp.dot(a, b, preferred_element_type=F32)


def _dot_nt(a, b):
    return lax.dot_general(a, b, (((1,), (1,)), ((), ())), preferred_element_type=F32)


def _dot_tn(a, b):
    return lax.dot_general(a, b, (((0,), (0,)), ((), ())), preferred_element_type=F32)


def _nbytes(shape, dtype):
    return int(np.prod(shape)) * jnp.dtype(dtype).itemsize


VMEM_TEMPORARIES_BYTES = 12 << 20
VMEM_UNCLAIMED_BYTES = 4 << 20


def _vmem_limit(nbytes):
    return int(min(nbytes + VMEM_TEMPORARIES_BYTES, V7X_VMEM_BYTES - VMEM_UNCLAIMED_BYTES))


def _layer_slab(a, layer, **kw):
    return pl.BlockSpec((None,) + a.shape[1:], lambda *g: (layer,) + (0,) * (a.ndim - 1), **kw)


def _whole(a):
    return pl.BlockSpec(a.shape, lambda *g: (0,) * a.ndim)


_ONCE = dict(pipeline_mode=pl.Buffered(1))


class _Layout:
    def __init__(self, d_model):
        self.mw = 3 * d_model // 8
        self.rw = 3 * d_model // 8
        self.gw = d_model - self.mw - self.rw
        self.md = self.mw // M_HEADS
        self.rd = self.rw // R_HEADS
        assert self.md <= LANES and 2 * self.rd == LANES and self.gw // G_GROUPS * 2 == LANES
        assert self.mw % LANES == 0 and self.rw % LANES == 0 and self.gw % LANES == 0
        self.mpad = M_HEADS * LANES
        self.dmix = self.mw + self.rw + self.gw
        o = 0
        self.mx = o; o += self.mw
        self.mz = o; o += self.mw
        self.gu = o; o += self.gw
        self.gv = o; o += self.gw
        self.gi = o; o += LANES
        self.gf = o; o += LANES
        self.rq = o; o += self.rw
        self.rk = o; o += self.rw
        self.rg = o; o += self.rw
        self.rv = o; o += self.rw
        self.n = o
        o = 0
        self.a_q = o; o += self.mpad
        self.a_k = o; o += self.mpad
        self.a_rq = o; o += self.rw
        self.a_rk = o; o += self.rw
        self.a_kz = o; o += self.rw
        self.a_rv = o; o += self.rw
        self.a_gv = o; o += self.gw
        self.a_zs = o; o += self.mw
        self.a_sxz = o; o += self.mw
        self.a_rgs = o; o += self.rw
        self.a_gug = o; o += self.gw
        self.na = o


RELAYOUT_ROWS = 256


def _relayout_kernel(w_ref, o_ref, *, lay):
    d = w_ref.shape[1]
    mw, rw, gw = lay.mw, lay.rw, lay.gw
    o_gate = 2 * mw
    o_rq = o_gate + 2 * M_HEADS
    pieces = ((0, lay.mx, 2 * mw), (o_rq, lay.rq, rw), (o_rq + rw, lay.rk, rw), (o_rq + 2 * rw, lay.rv, rw),
              (o_rq + 3 * rw, lay.rg, rw), (o_rq + 4 * rw, lay.gu, gw), (o_rq + 4 * rw + gw, lay.gv, gw))
    for src, dst, n in pieces:
        for r in range(0, n, RELAYOUT_ROWS):
            m = min(RELAYOUT_ROWS, n - r)
            o_ref[dst + r:dst + r + m, :] = w_ref[src + r:src + r + m, :].astype(o_ref.dtype)
    gi = w_ref[o_gate:o_gate + M_HEADS, :]
    gf = w_ref[o_gate + M_HEADS:o_gate + 2 * M_HEADS, :]
    o_ref[lay.gi:lay.gi + LANES, :] = jnp.concatenate(
        [gi, jnp.zeros((LANES - M_HEADS, d), F32)], axis=0).astype(o_ref.dtype)
    o_ref[lay.gf:lay.gf + LANES, :] = jnp.concatenate(
        [gf, gf, jnp.zeros((LANES - 2 * M_HEADS, d), F32)], axis=0).astype(o_ref.dtype)


def _relayout_in_proj_t(w_in, lay):
    depth, d, n_in = w_in.shape
    wt = jnp.swapaxes(w_in, 1, 2)
    est = 2 * n_in * d * 4 + 2 * lay.n * d * 2 + 4 * RELAYOUT_ROWS * d * 4
    return pl.pallas_call(
        functools.partial(_relayout_kernel, lay=lay),
        grid=(depth,),
        in_specs=[pl.BlockSpec((None, n_in, d), lambda l: (l, 0, 0))],
        out_specs=pl.BlockSpec((None, lay.n, d), lambda l: (l, 0, 0)),
        out_shape=jax.ShapeDtypeStruct((depth, lay.n, d), BF16),
        compiler_params=pltpu.CompilerParams(
            dimension_semantics=("arbitrary",), vmem_limit_bytes=_vmem_limit(est)),
        name="relayout_w_in",
    )(wt)


_INPROJ_PARAMS = ("convw", "convb", "bdq", "bdk", "bdvt", "gbi", "gbf", "mskip", "gnorm")
_N_SIDE = 3


def _inproj_kernel(*refs, lay, nj):
    n_p = len(_INPROJ_PARAMS)
    x_ref, g_ref, w_ref, cos_ref, sin_ref, zeta_ref = refs[0:6]
    mp = dict(zip(_INPROJ_PARAMS, refs[6:6 + n_p]))
    wf = refs[6 + n_p:6 + n_p + _N_SIDE]
    act_ref, gates_ref, vt_ref = refs[6 + n_p + _N_SIDE:9 + n_p + _N_SIDE]
    wb = refs[9 + n_p + _N_SIDE:9 + n_p + 2 * _N_SIDE]
    xbuf = refs[9 + n_p + 2 * _N_SIDE]
    tm = x_ref.shape[0]
    for src, dst in zip(wf, wb):
        dst[...] = src[...].astype(dst.dtype)

    md, mw, rw, gw = lay.md, lay.mw, lay.rw, lay.gw
    first_of_seq = lax.rem(pl.program_id(0), nj) == 0

    @pl.when(first_of_seq)
    def _():
        xbuf[0:8, :] = jnp.zeros((8, mw), F32)

    @pl.when(jnp.logical_not(first_of_seq))
    def _():
        xbuf[0:8, :] = xbuf[tm:tm + 8, :]

    hm = tm // TOKEN_SPLIT
    assert hm % CHUNK == 0
    _round_robin([_inproj_rows(r0, hm, x_ref, g_ref, w_ref, cos_ref, sin_ref, zeta_ref, mp,
                               act_ref, gates_ref, vt_ref, xbuf, lay) for r0 in range(0, tm, hm)])


def _inproj_rows(r0, nr, x_ref, g_ref, w_ref, cos_ref, sin_ref, zeta_ref, mp, act_ref, gates_ref, vt_ref,
                 xbuf, lay):
    md, mw, rw, gw = lay.md, lay.mw, lay.rw, lay.gw
    rs = slice(r0, r0 + nr)

    proj = _dot_nt(_rms(x_ref[rs, :], g_ref[...]).astype(BF16), w_ref[...])
    yield

    def put(col, val):
        act_ref[rs, col:col + val.shape[1]] = val.astype(act_ref.dtype)

    mx = proj[:, lay.mx:lay.mx + mw]
    xbuf[8 + r0:8 + r0 + nr, :] = mx
    conv = mp["convb"][...] + mp["convw"][M_CONV - 1:M_CONV, :] * mx
    for t in range(M_CONV - 1):
        lo = 8 + r0 - (M_CONV - 1) + t
        conv = conv + mp["convw"][t:t + 1, :] * xbuf[lo:lo + nr, :]
    xc = _silu(conv)
    xcb = xc.astype(BF16)
    yield
    q = _dot(xcb, mp["bdq"][...])
    k = _dot(xcb, mp["bdk"][...])
    row_p = lax.broadcasted_iota(jnp.int32, (lay.mpad, 1), 0) % LANES
    vt = _dot_nt(mp["bdvt"][...], mx.astype(BF16)) + (row_p == md).astype(F32)
    yield
    put(lay.a_q, q)
    put(lay.a_k, k * (md ** -0.5))
    for c in range(nr // CHUNK):
        vt_ref[r0 // CHUNK + c] = vt[:, c * CHUNK:(c + 1) * CHUNK]
    zs = _silu(proj[:, lay.mz:lay.mz + mw])
    put(lay.a_zs, zs)
    put(lay.a_sxz, mp["mskip"][...] * xc * zs)
    yield

    li = lax.broadcasted_iota(jnp.int32, (nr, LANES), 1)
    put(lay.a_gug, jax.nn.gelu(proj[:, lay.gu:lay.gu + gw]))
    yield
    gv = jax.nn.gelu(proj[:, lay.gv:lay.gv + gw])
    put(lay.a_gv, gv * lax.rsqrt(jnp.mean(gv * gv, axis=-1, keepdims=True) + EPS) * mp["gnorm"][...])
    yield

    gates_ref[rs, 0:LANES] = proj[:, lay.gi:lay.gi + LANES] + mp["gbi"][...]
    gf = proj[:, lay.gf:lay.gf + LANES] + mp["gbf"][...]
    gates_ref[rs, LANES:2 * LANES] = jnp.where(li < 2 * M_HEADS, jax.nn.log_sigmoid(gf), 0.0)
    yield

    half = lay.rd // 2
    first_half = (li % lay.rd) < half
    cs = cos_ref[rs, :]
    sn = sin_ref[rs, :]

    def rotary(v):
        partner = jnp.where(first_half, pltpu.roll(v, LANES - half, axis=1), pltpu.roll(v, half, axis=1))
        return v * cs + partner * sn

    for p_i in range(rw // LANES):
        off = p_i * LANES
        rq = proj[:, lay.rq + off:lay.rq + off + LANES]
        rk = proj[:, lay.rk + off:lay.rk + off + LANES]
        put(lay.a_rq + off, rotary(rq))
        kr = rotary(rk) * (lay.rd ** -0.5)
        put(lay.a_rk + off, kr)
        put(lay.a_kz + off, kr * zeta_ref[rs, off:off + LANES])
        yield
    put(lay.a_rgs, _silu(proj[:, lay.rg:lay.rg + rw]))
    yield
    put(lay.a_rv, proj[:, lay.rv:lay.rv + rw])


def _inproj(x2, tabs, lw, dense, layer, lay, seq):
    t, d = x2.shape
    tm = TOKEN_TILE
    nj = seq // tm
    steps = t // tm
    mps = [lw[k] for k in _INPROJ_PARAMS]
    pos = lambda i: (lax.rem(i, nj), 0)
    side = [dense["w_out"], dense["w_ff1"], dense["w_ff2"]]
    assert len(side) == _N_SIDE
    side_in, side_out, side_shape = [], [], []
    for a in side:
        rows, cols = a.shape[1], a.shape[2]
        assert rows % steps == 0 and (rows // steps) % 16 == 0
        side_in.append(pl.BlockSpec((None, rows // steps, cols), lambda i: (layer, i, 0)))
        side_out.append(pl.BlockSpec((rows // steps, cols), lambda i: (i, 0)))
        side_shape.append(jax.ShapeDtypeStruct((rows, cols), BF16))
    est = (2 * tm * d * 4 + d * lay.n * 2 + 2 * tm * lay.na * 2 + 2 * tm * 2 * LANES * 4
           + 2 * tm * lay.mpad * 4 + 6 * tm * LANES * 4 + (tm + 8) * lay.mw * 4 + tm * lay.n * 4
           + 2 * sum(_nbytes(a.shape[1:], a.dtype) for a in mps)
           + 3 * sum(_nbytes(a.shape[1:], a.dtype) // steps for a in side))
    outs = pl.pallas_call(
        functools.partial(_inproj_kernel, lay=lay, nj=nj),
        grid=(steps,),
        in_specs=([pl.BlockSpec((tm, d), lambda i: (i, 0)),
                   _layer_slab(dense["g_mix"], layer),
                   _layer_slab(dense["w_in_t"], layer, **_ONCE),
                   pl.BlockSpec((tm, LANES), pos),
                   pl.BlockSpec((tm, LANES), pos),
                   _whole(tabs["zeta_tile"])]
                  + [_layer_slab(a, layer) for a in mps] + side_in),
        out_specs=[pl.BlockSpec((tm, lay.na), lambda i: (i, 0)),
                   pl.BlockSpec((tm, 2 * LANES), lambda i: (i, 0)),
                   pl.BlockSpec((tm // CHUNK, lay.mpad, CHUNK), lambda i: (i, 0, 0))] + side_out,
        out_shape=[jax.ShapeDtypeStruct((t, lay.na), BF16),
                   jax.ShapeDtypeStruct((t, 2 * LANES), F32),
                   jax.ShapeDtypeStruct((t // CHUNK, lay.mpad, CHUNK), F32)] + side_shape,
        scratch_shapes=[pltpu.VMEM((tm + 8, lay.mw), F32)],
        compiler_params=pltpu.CompilerParams(
            dimension_semantics=("arbitrary",), vmem_limit_bytes=_vmem_limit(est)),
        name="inproj",
    )(x2, dense["g_mix"], dense["w_in_t"], tabs["cos"], tabs["sin"], tabs["zeta_tile"], *mps, *side)
    return outs[0:3], outs[3:6]


_MIXER_PARAMS = ("mnormt", "rnorm", "wcat", "gbmat")
_TABLES = ("decay", "xi", "cdt", "bmaskt")


def _block_diag2(x, y):
    z = jnp.zeros_like(x)
    return jnp.concatenate([jnp.concatenate([x, z], axis=1), jnp.concatenate([z, y], axis=1)], axis=0)


def _mixer_chunk(c, act, gates, vt, mp, tb, out, c_st, r_st, m_st, lay):
    md, mw, rw, gw = lay.md, lay.mw, lay.rw, lay.gw
    li = lax.broadcasted_iota(jnp.int32, (CHUNK, LANES), 1)
    ri = lax.broadcasted_iota(jnp.int32, (CHUNK, LANES), 0)
    causal_t = ri <= li
    tril = (li <= ri).astype(BF16)
    lo_half = li < (LANES // 2)
    rows = pl.ds(pl.multiple_of(c * CHUNK, CHUNK), CHUNK)

    def a(col, width=LANES):
        return act[rows, col:col + width]

    gi = gates[rows, 0:LANES]
    logf = gates[rows, LANES:2 * LANES]
    a1 = logf.astype(BF16)
    r1 = logf - a1.astype(F32)
    a2 = r1.astype(BF16)
    a3 = (r1 - a2.astype(F32)).astype(BF16)
    b3 = _dot(tril, jnp.concatenate([a1, a2, a3], axis=1))
    yield
    bcum = (b3[:, 2 * LANES:3 * LANES] + b3[:, LANES:2 * LANES]) + b3[:, 0:LANES]
    wb = jnp.where(li < M_HEADS, gi - bcum, bcum)
    wb_t = wb.T
    yield

    causal_t2 = jnp.concatenate([causal_t, causal_t], axis=1)
    hn_t = []
    for hp in range(M_HEADS // 2):
        heads = (2 * hp, 2 * hp + 1)
        pair = lambda f: jnp.concatenate([f(h) for h in heads], axis=1)
        rep = lambda v: jnp.broadcast_to(v, (1, LANES))
        q2 = a(lay.a_q + 2 * hp * LANES, 2 * LANES)
        k2 = a(lay.a_k + 2 * hp * LANES, 2 * LANES)
        vt2 = pair(lambda h: vt[c, h * LANES:(h + 1) * LANES, :])
        w_row = pair(lambda h: wb_t[h:h + 1, :])
        b_row = pair(lambda h: wb_t[M_HEADS + h:M_HEADS + h + 1, :])
        b_end = pair(lambda h: rep(wb_t[M_HEADS + h:M_HEADS + h + 1, CHUNK - 1:CHUNK]))
        w_end = b_end + w_row
        a_c = pair(lambda h: rep(jnp.max(w_end[:, (h % 2) * LANES:(h % 2 + 1) * LANES], axis=-1, keepdims=True)))
        e_row = jnp.exp(w_end - a_c)
        m_prev = m_st[hp:hp + 1, :]
        m_new = jnp.maximum(b_end + m_prev, a_c)
        s_old = jnp.exp(b_end + m_prev - m_new)
        s_new = jnp.exp(a_c - m_new)
        m_st[hp:hp + 1, :] = m_new

        il = b_row + m_prev
        c_prev = c_st[hp]
        kq = _dot_nt(jnp.concatenate([k2, c_prev.astype(BF16)], axis=0),
                     _block_diag2(q2[:, 0:LANES], q2[:, LANES:2 * LANES]))
        yield
        dm = jnp.where(causal_t2, pair(lambda h: jnp.broadcast_to(wb[:, h:h + 1], (CHUNK, LANES))) + b_row,
                       -jnp.inf)
        m_t = jnp.maximum(il, jnp.max(dm, axis=0, keepdims=True))
        s_inter = jnp.exp(il - m_t)
        p_t = kq[0:CHUNK] * jnp.exp(dm - m_t)
        inter = s_inter * kq[CHUNK:2 * CHUNK]
        p_b = p_t.astype(BF16)
        tot = _dot(vt2.astype(BF16), _block_diag2(p_b[:, 0:LANES], p_b[:, LANES:2 * LANES])) + inter
        upd = _dot((vt2 * e_row).astype(BF16), _block_diag2(k2[:, 0:LANES], k2[:, LANES:2 * LANES]))
        yield
        den = jnp.sum(p_t, axis=0, keepdims=True) + inter[md:md + 1, :]
        rden = 1.0 / jnp.maximum(jnp.abs(den), jnp.exp(-m_t))
        hout = tot[0:md, :] * rden
        ms = jnp.sum(hout * hout, axis=0, keepdims=True) * (1.0 / md)
        hn = hout * lax.rsqrt(ms + EPS)
        for i, h in enumerate(heads):
            hn_t.append(hn[:, i * LANES:(i + 1) * LANES] * mp["mnormt"][h * md:(h + 1) * md, :])
        c_st[hp] = s_old * c_prev + s_new * upd
        yield

    hm_t = jnp.concatenate(hn_t, axis=0)
    hm = jnp.concatenate([hm_t[t * LANES:(t + 1) * LANES, :].T for t in range(mw // LANES)], axis=1)
    out[rows, 0:mw] = (hm * a(lay.a_zs, mw).astype(F32) + a(lay.a_sxz, mw).astype(F32)).astype(out.dtype)
    yield

    lane_row = lax.broadcasted_iota(jnp.int32, (1, LANES), 1)
    m_a = (lane_row < LANES // 2).astype(BF16)
    m_b = (lane_row >= LANES // 2).astype(BF16)
    for p_i in range(rw // LANES):
        off = p_i * LANES
        qb = a(lay.a_rq + off)
        rk, rv = a(lay.a_rk + off), a(lay.a_rv + off)
        r_prev = r_st[p_i]
        sqr = _dot_nt(qb, jnp.concatenate([rk * m_a, rk * m_b, r_prev.astype(BF16)],
                                          axis=0))
        kv = _dot_tn(rv, a(lay.a_kz + off))
        yield
        p_ab = (sqr[:, 0:2 * LANES] * tb["decay"][p_i]).astype(BF16)
        hr = (_dot(p_ab, jnp.concatenate([rv * m_a, rv * m_b], axis=0))
              + sqr[:, 2 * LANES:3 * LANES] * tb["xi"][p_i])
        r_st[p_i] = tb["cdt"][p_i] * r_prev + tb["bmaskt"][...] * kv
        yield
        sq = hr * hr
        ms_a = jnp.sum(jnp.where(lo_half, sq, 0.0), axis=-1, keepdims=True)
        ms_b = jnp.sum(jnp.where(lo_half, 0.0, sq), axis=-1, keepdims=True)
        ms = jnp.where(lo_half, ms_a, ms_b) * (1.0 / lay.rd)
        hn = hr * lax.rsqrt(ms + EPS) * mp["rnorm"][:, off:off + LANES]
        out[rows, mw + off:mw + off + LANES] = (hn * a(lay.a_rgs + off).astype(F32)).astype(out.dtype)
        yield

    zero = jnp.zeros((2 * CHUNK, LANES), BF16)
    vst = [jnp.concatenate([a(lay.a_gv + q_i * LANES) * m_a, a(lay.a_gv + q_i * LANES) * m_b], axis=0)
           for q_i in range(gw // LANES)]
    assert len(vst) == 2
    mixed = _dot(mp["wcat"][...], jnp.concatenate([jnp.concatenate([vst[0], zero], axis=1),
                                                   jnp.concatenate([zero, vst[1]], axis=1)], axis=0))
    yield
    out[rows, mw + rw:mw + rw + gw] = (a(lay.a_gug, gw).astype(F32) * (mixed + mp["gbmat"][...])).astype(out.dtype)


def _round_robin(gens):
    live = list(gens)
    while live:
        live = [g for g in live if next(g, live) is not live]


def _mixer_kernel(*refs, lay):
    n_mp, n_tb = len(_MIXER_PARAMS), len(_TABLES)
    act_hbm, gates_ref, vt_ref = refs[0:3]
    mp = dict(zip(_MIXER_PARAMS, refs[3:3 + n_mp]))
    tb = dict(zip(_TABLES, refs[3 + n_mp:3 + n_mp + n_tb]))
    o_ref = refs[3 + n_mp + n_tb]
    c_st, r_st, m_st, act_buf, act_sem = refs[4 + n_mp + n_tb:]
    nrow, ts = gates_ref.shape[0], gates_ref.shape[1]
    j, nj = pl.program_id(1), pl.num_programs(1)

    def act_copy(step):
        slot = lax.rem(step, ACT_SLOTS)
        return pltpu.make_async_copy(act_hbm.at[:, pl.ds(step * ts, ts), :], act_buf.at[slot], act_sem.at[slot])

    @pl.when(j == 0)
    def _():
        c_st[...] = jnp.zeros_like(c_st)
        r_st[...] = jnp.zeros_like(r_st)
        m_st[...] = jnp.zeros_like(m_st)
        for s in range(ACT_SLOTS - 1):
            act_copy(s).start()

    @pl.when(j + (ACT_SLOTS - 1) < nj)
    def _():
        act_copy(j + (ACT_SLOTS - 1)).start()

    act_copy(j).wait()
    act_ref = act_buf.at[lax.rem(j, ACT_SLOTS)]

    def step(c, carry):
        _round_robin([_mixer_chunk(c, act_ref.at[r], gates_ref.at[r], vt_ref.at[r], mp, tb, o_ref.at[r],
                                   c_st.at[r], r_st.at[r], m_st.at[r], lay) for r in range(nrow)])
        return carry

    lax.fori_loop(0, ts // CHUNK, step, 0)


def _mixer(act, gates, vt, tabs, lw, layer, lay, batch, seq):
    ts, nrow = MIX_TILE, MIX_ROWS
    assert batch == nrow and seq // ts >= ACT_SLOTS
    mps = [lw[k] for k in _MIXER_PARAMS]
    tbs = [tabs[k] for k in _TABLES]
    scratch = [pltpu.VMEM((nrow, M_HEADS // 2, LANES, 2 * LANES), F32),
               pltpu.VMEM((nrow, lay.rw // LANES, LANES, LANES), F32),
               pltpu.VMEM((nrow, 8, 2 * LANES), F32),
               pltpu.VMEM((ACT_SLOTS, nrow, ts, lay.na), BF16),
               pltpu.SemaphoreType.DMA((ACT_SLOTS,))]
    est = (2 * nrow * ts * (2 * LANES * 4 + lay.mpad * 4 + lay.dmix * 2)
           + 2 * sum(_nbytes(a.shape[1:], a.dtype) for a in mps) + 2 * sum(_nbytes(a.shape, a.dtype) for a in tbs)
           + sum(_nbytes(s.shape, s.dtype) for s in scratch[:-1]))
    return pl.pallas_call(
        functools.partial(_mixer_kernel, lay=lay),
        grid=(batch // nrow, seq // ts),
        in_specs=([pl.BlockSpec(memory_space=pl.ANY),
                   pl.BlockSpec((nrow, ts, 2 * LANES), lambda b, j: (b, j, 0)),
                   pl.BlockSpec((nrow, ts // CHUNK, lay.mpad, CHUNK), lambda b, j: (b, j, 0, 0))]
                  + [_layer_slab(a, layer) for a in mps] + [_whole(a) for a in tbs]),
        out_specs=pl.BlockSpec((nrow, ts, lay.dmix), lambda b, j: (b, j, 0)),
        out_shape=jax.ShapeDtypeStruct((batch, seq, lay.dmix), BF16),
        scratch_shapes=scratch,
        compiler_params=pltpu.CompilerParams(
            dimension_semantics=("arbitrary", "arbitrary"), vmem_limit_bytes=_vmem_limit(est)),
        name="mixer",
    )(act, gates, vt, *mps, *tbs)


def _ffn_kernel(mix_ref, x_ref, wo_ref, g_ref, w1_ref, w2_ref, gf_ref, o_ref, *, final):
    x1 = x_ref[...] + _dot(mix_ref[...], wo_ref[...])
    h2 = _rms(x1, g_ref[...]).astype(BF16)
    acc = x1
    for j in range(w1_ref.shape[1] // FF_CHUNK):
        cols = slice(j * FF_CHUNK, (j + 1) * FF_CHUNK)
        a = jnp.square(jnp.maximum(_dot(h2, w1_ref[:, cols]), 0.0)).astype(BF16)
        acc = acc + _dot(a, w2_ref[cols, :])
    if final:
        acc = _rms(acc, gf_ref[...])
    o_ref[...] = acc


def _ffn(mix, x2, wts, dense, layer, final):
    t, d = x2.shape
    wo, w1, w2 = wts
    dff = w1.shape[1]
    tm = FFN_TILE
    est = (2 * tm * d * 2 + 4 * tm * d * 4 + (d * d + 2 * d * dff) * 2
           + tm * FF_CHUNK * 6 + 3 * tm * d * 4)
    resident = lambda a: pl.BlockSpec(a.shape, lambda i: (0, 0), **_ONCE)
    return pl.pallas_call(
        functools.partial(_ffn_kernel, final=final),
        grid=(t // tm,),
        in_specs=[pl.BlockSpec((tm, d), lambda i: (i, 0)),
                  pl.BlockSpec((tm, d), lambda i: (i, 0)),
                  resident(wo),
                  _layer_slab(dense["g_ff"], layer),
                  resident(w1),
                  resident(w2),
                  pl.BlockSpec((1, d), lambda i: (0, 0))],
        out_specs=pl.BlockSpec((tm, d), lambda i: (i, 0)),
        out_shape=jax.ShapeDtypeStruct((t, d), F32),
        compiler_params=pltpu.CompilerParams(
            dimension_semantics=("arbitrary",), vmem_limit_bytes=_vmem_limit(est)),
        name="ffn_final" if final else "ffn",
    )(mix, x2, wo, dense["g_ff"], w1, w2, dense["g_final"])


def _tables(lay, seq):
    rd, half = lay.rd, lay.rd // 2
    l = CHUNK
    pos = np.arange(seq, dtype=np.float64)
    freqs = ROPE_THETA ** (-np.arange(half, dtype=np.float64) / half)
    ang = pos[:, None] * freqs[None, :]
    sign = np.where(np.arange(LANES) % rd < half, -1.0, 1.0)
    cos = np.tile(np.cos(ang), (1, LANES // half))
    sin = np.tile(np.sin(ang), (1, LANES // half)) * sign[None, :]
    log_gamma = np.log(1.0 - 2.0 ** (-5.0 - np.arange(R_HEADS, dtype=np.float64)))
    p = np.arange(l, dtype=np.float64)
    causal = np.tril(np.ones((l, l), dtype=bool))
    rel = np.where(causal, p[:, None] - p[None, :], 0.0)
    decay = np.where(causal, np.exp(rel[None] * log_gamma[:, None, None]), 0.0)
    zeta = np.exp((l - 1.0 - p)[None, :] * log_gamma[:, None])
    xi = np.exp((p + 1.0)[None, :] * log_gamma[:, None])
    cdec = np.exp(l * log_gamma)
    npair = R_HEADS // 2
    decay_ab = decay.reshape(npair, 2, l, l).transpose(0, 2, 1, 3).reshape(npair, l, 2 * l)
    by_lane = lambda a: np.repeat(a.reshape(npair, 2, -1), rd, axis=1)
    xi_ab = by_lane(xi).transpose(0, 2, 1)
    zeta_k = by_lane(zeta).transpose(0, 2, 1)
    zeta_tile = np.tile(zeta_k.transpose(1, 0, 2).reshape(l, npair * LANES), (TOKEN_TILE // l, 1))
    cdt = np.broadcast_to(by_lane(cdec).transpose(0, 2, 1), (npair, LANES, LANES))
    head = np.arange(LANES) // rd
    bmaskt = head[:, None] == head[None, :]
    tabs = {"cos": cos, "sin": sin, "decay": decay_ab, "xi": xi_ab, "zeta_tile": zeta_tile, "cdt": cdt,
            "bmaskt": bmaskt}
    return {k: jnp.asarray(np.ascontiguousarray(v), F32) for k, v in tabs.items()}


def _block_diag_padded(w, lay):
    depth, nb = w.shape[0], w.shape[1]
    rows = w.reshape(depth, nb * M_BLOCK, M_BLOCK)
    tiled = jnp.tile(rows, (1, 1, lay.mpad // M_BLOCK))
    r = np.arange(nb * M_BLOCK)[:, None]
    c = np.arange(lay.mpad)[None, :]
    packed_c = (c // LANES) * lay.md + c % LANES
    keep = (c % LANES < lay.md) & (r // M_BLOCK == packed_c // M_BLOCK)
    return jnp.where(jnp.asarray(keep), tiled, 0.0).astype(BF16)


def _mixer_weights(lay, p):
    depth = p["m_wq"].shape[0]
    gbi = jnp.concatenate([p["m_i_bias"], jnp.zeros((depth, LANES - M_HEADS), F32)], axis=-1)
    gbf = jnp.concatenate(
        [p["m_f_bias"], p["m_f_bias"], jnp.zeros((depth, LANES - 2 * M_HEADS), F32)], axis=-1)
    causal = jnp.tril(jnp.ones((CHUNK, CHUNK), dtype=bool))
    wm = jnp.where(causal, p["g_ws"], 0.0)
    wcat = wm.transpose(0, 2, 1, 3).reshape(depth, CHUNK, G_GROUPS * CHUNK)
    gbmat = jnp.repeat(p["g_bs"].transpose(0, 2, 1), lay.gw // G_GROUPS, axis=2)
    row = lambda a: a[:, None, :]
    return {
        "convw": p["m_conv_w"], "convb": row(p["m_conv_b"]),
        "bdq": _block_diag_padded(p["m_wq"], lay),
        "bdk": _block_diag_padded(p["m_wk"], lay),
        "bdvt": _block_diag_padded(p["m_wv"], lay).transpose(0, 2, 1),
        "gbi": row(gbi), "gbf": row(gbf),
        "mnormt": jnp.broadcast_to(p["m_norm_g"][:, :, None], (depth, lay.mw, LANES)),
        "mskip": row(p["m_skip"]),
        "rnorm": row(p["r_norm_g"]),
        "gnorm": row(p["g_norm_g"]),
        "wcat": wcat.astype(BF16), "gbmat": gbmat,
    }


def kernel(x, norm_mix_g, w_in, m_conv_w, m_conv_b, m_wq, m_wk, m_wv, m_i_bias, m_f_bias, m_norm_g,
           m_skip, r_norm_g, g_norm_g, g_ws, g_bs, w_out, norm_ff_g, w_ff1, w_ff2, final_norm_g):
    batch, seq, d = x.shape
    depth = w_in.shape[0]
    lay = _Layout(d)
    assert seq % MIX_TILE == 0 and MIX_TILE % CHUNK == 0 and batch % MIX_ROWS == 0
    assert seq % TOKEN_TILE == 0 and TOKEN_TILE % CHUNK == 0 and w_ff1.shape[2] % FF_CHUNK == 0
    assert (batch * seq) % FFN_TILE == 0
    params = dict(m_conv_w=m_conv_w, m_conv_b=m_conv_b, m_wq=m_wq, m_wk=m_wk, m_wv=m_wv,
                  m_i_bias=m_i_bias, m_f_bias=m_f_bias, m_norm_g=m_norm_g, m_skip=m_skip,
                  r_norm_g=r_norm_g, g_norm_g=g_norm_g, g_ws=g_ws, g_bs=g_bs)
    tabs = _tables(lay, seq)
    lw = _mixer_weights(lay, params)
    dense = {
        "w_in_t": _relayout_in_proj_t(w_in, lay), "w_out": w_out, "w_ff1": w_ff1, "w_ff2": w_ff2,
        "g_mix": norm_mix_g[:, None, :], "g_ff": norm_ff_g[:, None, :], "g_final": final_norm_g[None, :],
    }
    x2 = x.reshape(batch * seq, d)
    for layer in range(depth):
        (act, gates, vt), ffn_w = _inproj(x2, tabs, lw, dense, layer, lay, seq)
        mix = _mixer(act.reshape(batch, seq, lay.na), gates.reshape(batch, seq, 2 * LANES),
                     vt.reshape(batch, seq // CHUNK, lay.mpad, CHUNK), tabs, lw, layer, lay, batch, seq)
        x2 = _ffn(mix.reshape(batch * seq, lay.dmix), x2, ffn_w, dense, layer, final=(layer == depth - 1))
    return x2.reshape(batch, seq, d)
```
